```python
import math
import jax, jax.numpy as jnp
from jax import lax
import numpy as np

D_MODEL = 1024
BATCH = 32
SEQ = 2048
DEPTH = 2

GRID_W = 64
CTX_LEN = 256
D_S5 = 512
S5_GROUP = 16
S5_GROUPS = D_S5 // S5_GROUP
S5_STATE = 64
DT_MIN = 0.001
DT_MAX = 0.1
D_LRU = D_MODEL - D_S5
LRU_HEADS = 8
LRU_HEAD_DIM = D_LRU // LRU_HEADS
LRU_CONV = 4
CONV_LEFT = 2
LRU_C = 8.0
D_IN = D_S5 + 2 * D_LRU
N_GROUPS = 4
EXPERTS_PER_GROUP = 4
N_EXPERTS = N_GROUPS * EXPERTS_PER_GROUP
TOP_K = 2
D_EXPERT = 512
EPS = 1e-6
F32 = jnp.float32

kernel_name = 'hybrid_s5_rglru_hmoe_prefix_dit'


def rmsnorm(x, g):
    xf = x.astype(F32)
    y = xf * lax.rsqrt(jnp.mean(xf * xf, axis=-1, keepdims=True) + EPS)
    return (y * g.astype(F32)).astype(x.dtype)


def modulate(x, shift, scale):
    return x * (1 + scale) + shift


def to_cols(x, rows):
    b, l, ch = x.shape
    return x.reshape(b, rows, GRID_W, ch).transpose(0, 2, 1, 3).reshape(b, l, ch)


def from_cols(x, rows):
    b, l, ch = x.shape
    return x.reshape(b, GRID_W, rows, ch).transpose(0, 2, 1, 3).reshape(b, l, ch)


def linear_scan(a, b, h0, reverse):
    def combine(left, right):
        a_l, b_l = left
        a_r, b_r = right
        return a_r * a_l, a_r * b_l + b_r
    a_cum, h = lax.associative_scan(combine, (a, b), axis=1, reverse=reverse)
    if h0 is not None:
        h = h + a_cum * h0[:, None]
    return h


def s5_discretize(a_re, a_im, log_dt, b_re, b_im):
    lam = lax.complex(a_re.astype(F32), a_im.astype(F32))
    dt = jnp.exp(log_dt.astype(F32))[:, None]
    a_bar = jnp.exp(lam * dt)
    b = lax.complex(b_re.astype(F32), b_im.astype(F32))
    b_bar = ((a_bar - 1) / lam)[..., None] * b
    return a_bar, b_bar


def s5_states(u, h0, a_bar, b_bar, reverse):
    bu = jnp.einsum('blgh,gph->blgp', u.astype(jnp.complex64), b_bar)
    a = jnp.broadcast_to(a_bar, (1, u.shape[1]) + a_bar.shape)
    return linear_scan(a, bu, h0, reverse)


def s5_readout(y, u, d_skip, w_glu, b_glu, dtype):
    b, l = u.shape[:2]
    y = (y + d_skip.astype(F32).reshape(S5_GROUPS, S5_GROUP) * u).reshape(b, l, D_S5)
    z = jax.nn.gelu(y)
    return (z * jax.nn.sigmoid(z @ w_glu.astype(F32) + b_glu.astype(F32))).astype(dtype)


def s5_mixer(u_ctx, u_lat, a_re, a_im, log_dt, b_re, b_im, c_re, c_im, d_skip, w_glu, b_glu,
             need_ctx, dtype):
    uc = u_ctx.astype(F32).reshape(u_ctx.shape[:2] + (S5_GROUPS, S5_GROUP))
    ul = u_lat.astype(F32).reshape(u_lat.shape[:2] + (S5_GROUPS, S5_GROUP))
    y_lat = 0.0
    y_ctx = 0.0
    for d, reverse in enumerate((False, True)):
        a_bar, b_bar = s5_discretize(a_re[d], a_im[d], log_dt[d], b_re[d], b_im[d])
        cmat = lax.complex(c_re[d].astype(F32), c_im[d].astype(F32))
        h_ctx = s5_states(uc, None, a_bar, b_bar, reverse)
        h0 = h_ctx[:, 0] if reverse else h_ctx[:, -1]
        h_lat = s5_states(ul, h0, a_bar, b_bar, reverse)
        y_lat = y_lat + jnp.einsum('blgp,ghp->blgh', h_lat, cmat).real
        if need_ctx:
            y_ctx = y_ctx + jnp.einsum('blgp,ghp->blgh', h_ctx, cmat).real
    out_lat = s5_readout(y_lat, ul, d_skip, w_glu, b_glu, dtype)
    out_ctx = s5_readout(y_ctx, uc, d_skip, w_glu, b_glu, dtype) if need_ctx else None
    return out_lat, out_ctx


def dwconv_centred(x, w, b):
    l = x.shape[1]
    xp = jnp.pad(x, ((0, 0), (CONV_LEFT, LRU_CONV - 1 - CONV_LEFT), (0, 0)))
    y = b
    for k in range(LRU_CONV):
        y = y + w[k] * xp[:, k:k + l]
    return y


def rglru_coeffs(xc, w_a, b_a, w_x, b_x, lam):
    bsz, l, _ = xc.shape
    xh = xc.reshape(bsz, l, LRU_HEADS, LRU_HEAD_DIM)
    r = jax.nn.sigmoid(jnp.einsum('blhi,hij->blhj', xh, w_a.astype(F32)).reshape(bsz, l, D_LRU)
                       + b_a.astype(F32))
    i = jax.nn.sigmoid(jnp.einsum('blhi,hij->blhj', xh, w_x.astype(F32)).reshape(bsz, l, D_LRU)
                       + b_x.astype(F32))
    log_a = -LRU_C * r * jax.nn.softplus(-lam.astype(F32))
    a = jnp.exp(log_a)
    b = jnp.sqrt(-jnp.expm1(2 * log_a)) * (i * xc)
    return a, b


def rglru_mixer(x_ctx, g_ctx, x_lat, g_lat, rows, conv_w, conv_b, w_a, b_a, w_x, b_x, lam,
                need_ctx, dtype):
    cw = conv_w.astype(F32)
    cb = conv_b.astype(F32)
    xc_ctx = dwconv_centred(x_ctx.astype(F32), cw, cb)
    xc_lat = dwconv_centred(to_cols(x_lat.astype(F32), rows), cw, cb)
    h_lat = 0.0
    h_ctx = 0.0
    for d, reverse in enumerate((False, True)):
        a, b = rglru_coeffs(xc_ctx, w_a[d], b_a[d], w_x[d], b_x[d], lam[d])
        hc = linear_scan(a, b, None, reverse)
        h0 = hc[:, 0] if reverse else hc[:, -1]
        a, b = rglru_coeffs(xc_lat, w_a[d], b_a[d], w_x[d], b_x[d], lam[d])
        h_lat = h_lat + linear_scan(a, b, h0, reverse)
        if need_ctx:
            h_ctx = h_ctx + hc
    out_lat = (from_cols(h_lat, rows) * jax.nn.gelu(g_lat.astype(F32))).astype(dtype)
    out_ctx = (h_ctx * jax.nn.gelu(g_ctx.astype(F32))).astype(dtype) if need_ctx else None
    return out_lat, out_ctx


def hier_moe(h, w_group, b_group, w_router, b_router, w1, w3, w2):
    shp = h.shape
    t = h.reshape(-1, shp[-1])
    tf = t.astype(F32)
    group_prob = jax.nn.softmax(tf @ w_group.astype(F32) + b_group.astype(F32), axis=-1)
    g_p, g_idx = lax.top_k(group_prob, 1)
    exp_logits = jnp.einsum('td,gde->tge', tf, w_router.astype(F32)) + b_router.astype(F32)
    sel = jnp.einsum('tge,tg->te', exp_logits, jax.nn.one_hot(g_idx[:, 0], N_GROUPS, dtype=F32))
    e_p, e_idx = lax.top_k(jax.nn.softmax(sel, axis=-1), TOP_K)
    weights = g_p * e_p / jnp.sum(e_p, axis=-1, keepdims=True)
    expert_id = g_idx * EXPERTS_PER_GROUP + e_idx
    gates = jnp.einsum('tk,tke->te', weights,
                       jax.nn.one_hot(expert_id, N_EXPERTS, dtype=F32)).astype(h.dtype)
    out = jnp.zeros_like(t)
    for e in range(N_EXPERTS):
        hid = jax.nn.silu(t @ w1[e]) * (t @ w3[e])
        out = out + gates[:, e:e + 1] * (hid @ w2[e])
    return out.reshape(shp)


def setup_inputs(seed: int = 0) -> dict:
    key = jax.random.key(seed)
    ks = iter(jax.random.split(key, 48))

    def nrm(shape, std):
        return std * jax.random.normal(next(ks), shape, F32)

    L, D = DEPTH, D_MODEL
    G, P, H = S5_GROUPS, S5_STATE, S5_GROUP
    x = nrm((BATCH, SEQ, D), 1.0)
    c = nrm((BATCH, D), 1.0)
    ctx = nrm((BATCH, CTX_LEN, D), 1.0)
    c_ctx = nrm((D,), 1.0)
    w_mod = nrm((L, D, 6 * D), 0.5 * D ** -0.5)
    b_mod = nrm((L, 6 * D), 0.02)
    norm1_g = 1.0 + nrm((L, D), 0.02)
    norm2_g = 1.0 + nrm((L, D), 0.02)
    w_in = nrm((L, D, D_IN), D ** -0.5)
    w_out = nrm((L, D, D), D ** -0.5)
    n = jnp.arange(P, dtype=F32)
    s5_a_re = -0.5 + nrm((L, 2, G, P), 0.02)
    s5_a_im = math.pi * n + nrm((L, 2, G, P), 0.02)
    s5_log_dt = jax.random.uniform(next(ks), (L, 2, G), F32, math.log(DT_MIN), math.log(DT_MAX))
    s5_b_re = nrm((L, 2, G, P, H), (2 * H) ** -0.5)
    s5_b_im = nrm((L, 2, G, P, H), (2 * H) ** -0.5)
    s5_c_re = nrm((L, 2, G, H, P), P ** -0.5)
    s5_c_im = nrm((L, 2, G, H, P), P ** -0.5)
    s5_d = nrm((L, D_S5), 1.0)
    s5_w_glu = nrm((L, D_S5, D_S5), D_S5 ** -0.5)
    s5_b_glu = nrm((L, D_S5), 0.02)
    lru_conv_w = nrm((L, LRU_CONV, D_LRU), LRU_CONV ** -0.5)
    lru_conv_b = nrm((L, D_LRU), 0.02)
    lru_w_a = nrm((L, 2, LRU_HEADS, LRU_HEAD_DIM, LRU_HEAD_DIM), LRU_HEAD_DIM ** -0.5)
    lru_b_a = nrm((L, 2, D_LRU), 0.02)
    lru_w_x = nrm((L, 2, LRU_HEADS, LRU_HEAD_DIM, LRU_HEAD_DIM), LRU_HEAD_DIM ** -0.5)
    lru_b_x = nrm((L, 2, D_LRU), 0.02)
    a_c = jax.random.uniform(next(ks), (L, 2, D_LRU), F32, 0.9, 0.999)
    s = a_c ** (1.0 / LRU_C)
    lru_lam = jnp.log(s) - jnp.log1p(-s)
    moe_w_group = nrm((L, D, N_GROUPS), D ** -0.5)
    moe_b_group = nrm((L, N_GROUPS), 0.01)
    moe_w_router = nrm((L, N_GROUPS, D, EXPERTS_PER_GROUP), D ** -0.5)
    moe_b_router = nrm((L, N_GROUPS, EXPERTS_PER_GROUP), 0.01)
    moe_w1 = nrm((L, N_EXPERTS, D, D_EXPERT), D ** -0.5)
    moe_w3 = nrm((L, N_EXPERTS, D, D_EXPERT), D ** -0.5)
    moe_w2 = nrm((L, N_EXPERTS, D_EXPERT, D), D_EXPERT ** -0.5)
    final_g = 1.0 + nrm((D,), 0.02)
    return {'x': x, 'c': c, 'ctx': ctx, 'c_ctx': c_ctx, 'w_mod': w_mod, 'b_mod': b_mod,
            'norm1_g': norm1_g, 'norm2_g': norm2_g, 'w_in': w_in, 'w_out': w_out,
            's5_a_re': s5_a_re, 's5_a_im': s5_a_im, 's5_log_dt': s5_log_dt,
            's5_b_re': s5_b_re, 's5_b_im': s5_b_im, 's5_c_re': s5_c_re, 's5_c_im': s5_c_im,
            's5_d': s5_d, 's5_w_glu': s5_w_glu, 's5_b_glu': s5_b_glu,
            'lru_conv_w': lru_conv_w, 'lru_conv_b': lru_conv_b, 'lru_w_a': lru_w_a,
            'lru_b_a': lru_b_a, 'lru_w_x': lru_w_x, 'lru_b_x': lru_b_x, 'lru_lam': lru_lam,
            'moe_w_group': moe_w_group, 'moe_b_group': moe_b_group,
            'moe_w_router': moe_w_router, 'moe_b_router': moe_b_router,
            'moe_w1': moe_w1, 'moe_w3': moe_w3, 'moe_w2': moe_w2, 'final_g': final_g}


def reference(x, c, ctx, c_ctx, w_mod, b_mod, norm1_g, norm2_g, w_in, w_out,
              s5_a_re, s5_a_im, s5_log_dt, s5_b_re, s5_b_im, s5_c_re, s5_c_im,
              s5_d, s5_w_glu, s5_b_glu,
              lru_conv_w, lru_conv_b, lru_w_a, lru_b_a, lru_w_x, lru_b_x, lru_lam,
              moe_w_group, moe_b_group, moe_w_router, moe_b_router,
              moe_w1, moe_w3, moe_w2, final_g):
    rows = x.shape[1] // GRID_W
    dtype = x.dtype
    silu_c = jax.nn.silu(c)
    silu_cc = jax.nn.silu(c_ctx)
    xc = ctx
    n_ctx = ctx.shape[1]
    for l in range(DEPTH):
        need_ctx = l < DEPTH - 1
        mod = (silu_c @ w_mod[l] + b_mod[l])[:, None, :]
        modc = (silu_cc @ w_mod[l] + b_mod[l])[None, None, :]
        sh1, sc1, ga1, sh2, sc2, ga2 = jnp.split(mod, 6, axis=-1)
        sh1c, sc1c, ga1c, sh2c, sc2c, ga2c = jnp.split(modc, 6, axis=-1)

        p_lat = modulate(rmsnorm(x, norm1_g[l]), sh1, sc1) @ w_in[l]
        p_ctx = modulate(rmsnorm(xc, norm1_g[l]), sh1c, sc1c) @ w_in[l]
        u_lat, r_lat, q_lat = jnp.split(p_lat, [D_S5, D_S5 + D_LRU], axis=-1)
        u_ctx, r_ctx, q_ctx = jnp.split(p_ctx, [D_S5, D_S5 + D_LRU], axis=-1)
        s5_lat, s5_ctx = s5_mixer(u_ctx, u_lat, s5_a_re[l], s5_a_im[l], s5_log_dt[l],
                                  s5_b_re[l], s5_b_im[l], s5_c_re[l], s5_c_im[l],
                                  s5_d[l], s5_w_glu[l], s5_b_glu[l], need_ctx, dtype)
        lru_lat, lru_ctx = rglru_mixer(r_ctx, q_ctx, r_lat, q_lat, rows,
                                       lru_conv_w[l], lru_conv_b[l], lru_w_a[l], lru_b_a[l],
                                       lru_w_x[l], lru_b_x[l], lru_lam[l], need_ctx, dtype)
        x = x + ga1 * (jnp.concatenate([s5_lat, lru_lat], axis=-1) @ w_out[l])

        if need_ctx:
            xc = xc + ga1c * (jnp.concatenate([s5_ctx, lru_ctx], axis=-1) @ w_out[l])
            hm = jnp.concatenate([modulate(rmsnorm(xc, norm2_g[l]), sh2c, sc2c),
                                  modulate(rmsnorm(x, norm2_g[l]), sh2, sc2)], axis=1)
            ym = hier_moe(hm, moe_w_group[l], moe_b_group[l], moe_w_router[l], moe_b_router[l],
                          moe_w1[l], moe_w3[l], moe_w2[l])
            xc = xc + ga2c * ym[:, :n_ctx]
            x = x + ga2 * ym[:, n_ctx:]
        else:
            hm = modulate(rmsnorm(x, norm2_g[l]), sh2, sc2)
            x = x + ga2 * hier_moe(hm, moe_w_group[l], moe_b_group[l], moe_w_router[l],
                                   moe_b_router[l], moe_w1[l], moe_w3[l], moe_w2[l])
    return rmsnorm(x, final_g)
```

```python
import functools
import math

import jax
import jax.numpy as jnp
from jax import lax
from jax.experimental import pallas as pl
from jax.experimental.pallas import tpu as pltpu

F32 = jnp.float32
BF16 = jnp.bfloat16
HIGHEST = lax.Precision.HIGHEST

EPS = 1e-6
GRID_W = 64
LRU_C = 8.0
CONV_LEFT = 2
S5_CHUNK = 16
LRU_STEPS = 32
LANES = 128
VMEM_LIMIT = 56 * 1024 * 1024


def _cparams(*sem):
    return pltpu.CompilerParams(dimension_semantics=sem, vmem_limit_bytes=VMEM_LIMIT)


def _rms(x, g):
    return x * lax.rsqrt(jnp.mean(x * x, axis=-1, keepdims=True) + EPS) * g


def _mod_kernel(c_ref, w_ref, b_ref, o_ref):
    c = c_ref[...]
    s = c * jax.nn.sigmoid(c)
    o_ref[...] = jnp.dot(s, w_ref[...], preferred_element_type=F32, precision=HIGHEST) + b_ref[...]


def _modulation(c_rows, w_mod, b_mod):
    depth, d, n6 = w_mod.shape
    rows = c_rows.shape[0]
    tn = n6 // 4
    return pl.pallas_call(
        _mod_kernel,
        grid=(depth, n6 // tn),
        in_specs=[pl.BlockSpec((rows, d), lambda l, j: (0, 0)),
                  pl.BlockSpec((None, d, tn), lambda l, j: (l, 0, j)),
                  pl.BlockSpec((None, 1, tn), lambda l, j: (l, 0, j))],
        out_specs=pl.BlockSpec((None, rows, tn), lambda l, j: (l, 0, j)),
        out_shape=jax.ShapeDtypeStruct((depth, rows, n6), F32),
        compiler_params=_cparams("parallel", "parallel"),
        name="modulation",
    )(c_rows, w_mod, b_mod.reshape(depth, 1, n6))


def _mod_spec(layer, part, d, nb):
    return pl.BlockSpec((None, None, 1, d),
                        lambda b, i: (layer, jnp.where(i == 0, nb, b), 0, part))


def _mod_spec_lat(layer, part, d):
    return pl.BlockSpec((None, None, 1, d), lambda b, i: (layer, b, 0, part))


def _inproj_kernel(has_y, d_s5, d_lru, *refs):
    if has_y:
        x_ref, y_ref, ga_ref, sh_ref, sc_ref, g_ref, w_ref, xo_ref, u_ref, r_ref, q_ref = refs
        x = x_ref[...] + ga_ref[...] * y_ref[...]
        xo_ref[...] = x
    else:
        x_ref, sh_ref, sc_ref, g_ref, w_ref, u_ref, r_ref, q_ref = refs
        x = x_ref[...]
    h = _rms(x, g_ref[...]) * (1.0 + sc_ref[...]) + sh_ref[...]
    p = jnp.dot(h.astype(BF16), w_ref[...], preferred_element_type=F32)
    u_ref[...] = p[:, :d_s5]
    r_ref[...] = p[:, d_s5:d_s5 + d_lru]
    q_ref[...] = p[:, d_s5 + d_lru:]


def _inproj(xall, y, mods, layer, g1, w_in_bf, d_s5, d_lru, tm):
    nb, n, d = xall.shape
    has_y = y is not None
    tok = lambda w: pl.BlockSpec((None, tm, w), lambda b, i: (b, i, 0))
    in_specs = [tok(d)]
    args = [xall]
    if has_y:
        in_specs += [tok(d), _mod_spec(layer - 1, 5, d, nb)]
        args += [y, mods]
    in_specs += [_mod_spec(layer, 0, d, nb), _mod_spec(layer, 1, d, nb),
                 pl.BlockSpec((1, d), lambda b, i: (0, 0)),
                 pl.BlockSpec(w_in_bf.shape, lambda b, i: (0, 0))]
    args += [mods, mods, g1.reshape(1, d), w_in_bf]
    out_specs = [tok(d_s5), tok(d_lru), tok(d_lru)]
    out_shape = [jax.ShapeDtypeStruct((nb, n, d_s5), F32),
                 jax.ShapeDtypeStruct((nb, n, d_lru), F32),
                 jax.ShapeDtypeStruct((nb, n, d_lru), F32)]
    if has_y:
        out_specs = [tok(d)] + out_specs
        out_shape = [jax.ShapeDtypeStruct((nb, n, d), F32)] + out_shape
    res = pl.pallas_call(
        functools.partial(_inproj_kernel, has_y, d_s5, d_lru),
        grid=(nb, n // tm), in_specs=in_specs, out_specs=out_specs, out_shape=out_shape,
        compiler_params=_cparams("parallel", "parallel"),
        name="inproj",
    )(*args)
    if has_y:
        return res
    return [xall] + list(res)


def _s5_weights(a_re, a_im, log_dt, b_re, b_im, c_re, c_im):
    q = S5_CHUNK
    lam = lax.complex(a_re.astype(F32), a_im.astype(F32))
    dt = jnp.exp(log_dt.astype(F32))[..., None]
    a_bar = jnp.exp(lam * dt)
    bmat = lax.complex(b_re.astype(F32), b_im.astype(F32))
    b_bar = ((a_bar - 1) / lam)[..., None] * bmat
    cmat = lax.complex(c_re.astype(F32), c_im.astype(F32))
    k = jnp.arange(q + 1, dtype=F32)
    apow = jnp.exp((lam * dt)[..., None] * k)
    ngrp, nst = a_re.shape[1], a_re.shape[2]
    nh = b_re.shape[-1]

    kern = jnp.einsum('dgop,dgpk,dgpi->dgkoi', cmat, apow[..., :q], b_bar, precision=HIGHEST).real
    sig = jnp.arange(q)[:, None]
    tau = jnp.arange(q)[None, :]
    lag_f = jnp.clip(tau - sig, 0, q - 1)
    lag_b = jnp.clip(sig - tau, 0, q - 1)
    tf = jnp.where((sig <= tau)[None, :, :, None, None], kern[0][:, lag_f], 0.0)
    tb = jnp.where((sig >= tau)[None, :, :, None, None], kern[1][:, lag_b], 0.0)
    toep = (tf + tb).transpose(0, 1, 4, 2, 3).reshape(ngrp, q * nh, q * nh)

    win_f = jnp.einsum('gps,gpi->gsip', apow[0][..., :q][..., ::-1], b_bar[0])
    win_b = jnp.einsum('gps,gpi->gsip', apow[1][..., :q], b_bar[1])
    win = jnp.concatenate([win_f.real, win_b.real, win_f.imag, win_b.imag], axis=-1)
    win = win.reshape(ngrp, q * nh, 4 * nst)
    w1 = jnp.concatenate([toep, win], axis=-1)

    cf = jnp.einsum('gop,gpt->gpto', cmat[0], apow[0][..., 1:])
    cb = jnp.einsum('gop,gpt->gpto', cmat[1], apow[1][..., 1:][..., ::-1])
    z = jnp.zeros_like(cf.real)
    wof = jnp.concatenate([cf.real, z, -cf.imag, z], axis=1).reshape(ngrp, 4 * nst, q * nh)
    wob = jnp.concatenate([z, cb.real, z, -cb.imag], axis=1).reshape(ngrp, 4 * nst, q * nh)

    aq = apow[..., q]
    a16 = jnp.stack([jnp.concatenate([aq[0].real, aq[1].real], -1),
                     jnp.concatenate([aq[0].imag, aq[1].imag], -1)], axis=1)
    return w1.astype(BF16), wof.astype(BF16), wob.astype(BF16), a16.astype(F32)


def _s5_kernel(nb, nc_ctx, nc_lat, mt, u_ref, w1_ref, wof_ref, wob_ref, a_ref, y_ref, s_ref, hf_ref, hb_ref):
    m, qh = u_ref.shape
    ns2 = a_ref.shape[-1]
    w1 = w1_ref[...]

    def intra(i, c):
        r0 = pl.multiple_of(i * mt, mt)
        res = jnp.dot(u_ref[pl.ds(r0, mt), :], w1, preferred_element_type=F32)
        y_ref[pl.ds(r0, mt), :] = res[:, :qh]
        s_ref[pl.ds(r0, mt), :] = res[:, qh:]
        return c

    lax.fori_loop(0, m // mt, intra, 0)

    are = jnp.broadcast_to(a_ref[0:1, :], (nb, ns2))
    aim = jnp.broadcast_to(a_ref[1:2, :], (nb, ns2))
    fwd_lane = lax.broadcasted_iota(jnp.int32, (nb, ns2), 1) < ns2 // 2

    def scan(base, n, carry):
        def step(k, hc):
            hre, him = hc
            rf = pl.multiple_of((base + k) * nb, nb)
            rb = pl.multiple_of((base + n - 1 - k) * nb, nb)
            hf_ref[pl.ds(rf, nb), 0:ns2] = hre
            hf_ref[pl.ds(rf, nb), ns2:2 * ns2] = him
            hb_ref[pl.ds(rb, nb), 0:ns2] = hre
            hb_ref[pl.ds(rb, nb), ns2:2 * ns2] = him
            sre = jnp.where(fwd_lane, s_ref[pl.ds(rf, nb), 0:ns2], s_ref[pl.ds(rb, nb), 0:ns2])
            sim = jnp.where(fwd_lane, s_ref[pl.ds(rf, nb), ns2:2 * ns2], s_ref[pl.ds(rb, nb), ns2:2 * ns2])
            return (are * hre - aim * him + sre, are * him + aim * hre + sim)
        return lax.fori_loop(0, n, step, carry)

    zero = jnp.zeros((nb, ns2), F32)
    carry = scan(0, nc_ctx, (zero, zero))
    scan(nc_ctx, nc_lat, carry)

    wof = wof_ref[...]
    wob = wob_ref[...]

    def inter(i, c):
        r0 = pl.multiple_of(i * mt, mt)
        y_ref[pl.ds(r0, mt), :] += (
            jnp.dot(hf_ref[pl.ds(r0, mt), :].astype(BF16), wof, preferred_element_type=F32)
            + jnp.dot(hb_ref[pl.ds(r0, mt), :].astype(BF16), wob, preferred_element_type=F32))
        return c

    lax.fori_loop(0, m // mt, inter, 0)


def _s5_scan(u_g, w1, wof, wob, a16, nb, nc_ctx, nc_lat):
    ngrp, m, qh = u_g.shape
    ns4 = wof.shape[1]
    mt = nb * math.gcd(nc_ctx + nc_lat, 16)
    grp = lambda a: pl.BlockSpec((None,) + a.shape[1:], lambda g: (g, 0, 0))
    return pl.pallas_call(
        functools.partial(_s5_kernel, nb, nc_ctx, nc_lat, mt),
        grid=(ngrp,),
        in_specs=[grp(u_g), grp(w1), grp(wof), grp(wob), grp(a16)],
        out_specs=pl.BlockSpec((None, m, qh), lambda g: (g, 0, 0)),
        out_shape=jax.ShapeDtypeStruct((ngrp, m, qh), F32),
        scratch_shapes=[pltpu.VMEM((m, ns4), F32), pltpu.VMEM((m, ns4), F32), pltpu.VMEM((m, ns4), F32)],
        compiler_params=_cparams("parallel"),
        name="s5_chunked",
    )(u_g, w1, wof, wob, a16)


def _lru_chunk(s, reverse, nc_ctx, nc):
    if not reverse:
        return s
    return jnp.where(s < nc_ctx, nc_ctx - 1 - s, nc - 1 - (s - nc_ctx))


def _lru_kernel(reverse, combine, nc_ctx, nc, *refs):
    if combine:
        (r_ref, rp_ref, rn_ref, cw_ref, cb_ref, wa_ref, ba_ref, wx_ref, bx_ref, cn_ref,
         q_ref, ho_ref, o_ref, h_ref, a_s, b_s) = refs
    else:
        (r_ref, rp_ref, rn_ref, cw_ref, cb_ref, wa_ref, ba_ref, wx_ref, bx_ref, cn_ref,
         o_ref, h_ref, a_s, b_s) = refs
    jt, nb, ch = r_ref.shape
    s = pl.program_id(0)
    c = _lru_chunk(s, reverse, nc_ctx, nc)
    first = jnp.logical_or(c == 0, c == nc_ctx)
    last = jnp.logical_or(c == nc_ctx - 1, c == nc - 1)

    @pl.when(s == 0)
    def _():
        h_ref[...] = jnp.zeros_like(h_ref)

    prev = jnp.where(first, 0.0, rp_ref[...])
    nxt = jnp.where(last, 0.0, rn_ref[...])
    xe = jnp.concatenate([prev, r_ref[...], nxt], axis=0)
    cw = cw_ref[...]
    xc = cb_ref[...].reshape(1, 1, ch)
    for k in range(cw.shape[0]):
        xc = xc + cw[k:k + 1, :].reshape(1, 1, ch) * xe[k:k + jt]
    xc2 = xc.reshape(jt * nb, ch)

    xb = xc2.astype(BF16)
    nblk, wb = wa_ref.shape[0], wa_ref.shape[1]
    za = jnp.concatenate([jnp.dot(xb[:, i * wb:(i + 1) * wb], wa_ref[i], preferred_element_type=F32)
                          for i in range(nblk)], axis=-1)
    zx = jnp.concatenate([jnp.dot(xb[:, i * wb:(i + 1) * wb], wx_ref[i], preferred_element_type=F32)
                          for i in range(nblk)], axis=-1)
    rg = jax.nn.sigmoid(za + ba_ref[...])
    ig = jax.nn.sigmoid(zx + bx_ref[...])
    log_a = cn_ref[...] * rg
    a = jnp.exp(log_a)
    bb = jnp.sqrt(1.0 - a * a) * (ig * xc2)
    a_s[...] = a.reshape(jt, nb, ch)
    b_s[...] = bb.reshape(jt, nb, ch)

    def step(t, h):
        tt = jt - 1 - t if reverse else t
        h = a_s[tt] * h + b_s[tt]
        if combine:
            o_ref[tt] = (h + ho_ref[tt]) * jax.nn.gelu(q_ref[tt])
        else:
            o_ref[tt] = h
        return h

    h_ref[...] = lax.fori_loop(0, jt, step, h_ref[...])


def _lru_pass(reverse, r_tm, q_tm, h_other, conv_w, conv_b, wa, ba, wx, bx, cneg, n_ctx):
    j, nb, ch = r_tm.shape
    jt = LRU_STEPS
    nc, nc_ctx = j // jt, n_ctx // jt
    combine = h_other is not None
    cidx = lambda s: _lru_chunk(s, reverse, nc_ctx, nc)
    main = pl.BlockSpec((jt, nb, ch), lambda s: (cidx(s), 0, 0))
    prev = pl.BlockSpec((CONV_LEFT, nb, ch), lambda s: (jnp.maximum(cidx(s) * (jt // CONV_LEFT) - 1, 0), 0, 0))
    nxt = pl.BlockSpec((1, nb, ch), lambda s: (jnp.minimum((cidx(s) + 1) * jt, j - 1), 0, 0))
    full = lambda a: pl.BlockSpec(a.shape, lambda s: (0,) * a.ndim)
    consts = [conv_w, conv_b, wa, ba, wx, bx, cneg]
    in_specs = [main, prev, nxt] + [full(a) for a in consts]
    args = [r_tm, r_tm, r_tm] + consts
    if combine:
        in_specs += [main, main]
        args += [q_tm, h_other]
    return pl.pallas_call(
        functools.partial(_lru_kernel, reverse, combine, nc_ctx, nc),
        grid=(nc,), in_specs=in_specs, out_specs=main,
        out_shape=jax.ShapeDtypeStruct((j, nb, ch), F32),
        scratch_shapes=[pltpu.VMEM((nb, ch), F32), pltpu.VMEM((jt, nb, ch), F32), pltpu.VMEM((jt, nb, ch), F32)],
        compiler_params=_cparams("arbitrary"),
        name="lru_bwd" if reverse else "lru_fwd",
    )(*args)


def _block_diag_halves(w):
    nh, hd, _ = w.shape
    half = nh // 2
    eye = jnp.eye(half, dtype=w.dtype)
    wh = w.reshape(2, half, hd, hd)
    bd = jnp.einsum('bhij,hk->bhikj', wh, eye).reshape(2, half * hd, half * hd)
    return bd.astype(BF16)


def _post_kernel(n_groups, n_per, ys_ref, u_ref, lr_ref, x_ref, ga_ref, sh_ref, sc_ref, d_ref, wg_ref, bg_ref,
                 wo_ref, g2_ref, wr_ref, br_ref, x1_ref, hm_ref, meta_ref):
    tm = x_ref.shape[0]
    ys = ys_ref[...] + d_ref[...] * u_ref[...]
    z = jax.nn.gelu(ys)
    s5o = z * jax.nn.sigmoid(jnp.dot(z.astype(BF16), wg_ref[...], preferred_element_type=F32) + bg_ref[...])
    mix = jnp.concatenate([s5o.astype(BF16), lr_ref[...].astype(BF16)], axis=-1)
    x1 = x_ref[...] + ga_ref[...] * jnp.dot(mix, wo_ref[...], preferred_element_type=F32)
    x1_ref[...] = x1
    hm = _rms(x1, g2_ref[...]) * (1.0 + sc_ref[...]) + sh_ref[...]
    hm_ref[...] = hm

    hi = hm.astype(BF16)
    lo = (hm - hi.astype(F32)).astype(BF16)
    two = jnp.dot(jnp.concatenate([hi, lo], axis=-1), wr_ref[...], preferred_element_type=F32)
    logits = two[:, :LANES] + two[:, LANES:] + br_ref[...]

    lane = lax.broadcasted_iota(jnp.int32, (tm, LANES), 1).astype(F32)
    neg = jnp.float32(-jnp.inf)
    big = jnp.float32(LANES)
    lg = jnp.where(lane < n_groups, logits, neg)
    mg = jnp.max(lg, axis=-1, keepdims=True)
    g_p = 1.0 / jnp.sum(jnp.exp(lg - mg), axis=-1, keepdims=True)
    gidx = jnp.min(jnp.where(lg == mg, lane, big), axis=-1, keepdims=True)
    e0 = n_groups + n_per * gidx
    le = jnp.where(jnp.logical_and(lane >= e0, lane < e0 + n_per), logits, neg)
    m1 = jnp.max(le, axis=-1, keepdims=True)
    i1 = jnp.min(jnp.where(le == m1, lane, big), axis=-1, keepdims=True)
    le2 = jnp.where(lane == i1, neg, le)
    m2 = jnp.max(le2, axis=-1, keepdims=True)
    i2 = jnp.min(jnp.where(le2 == m2, lane, big), axis=-1, keepdims=True)
    r2 = jnp.exp(m2 - m1)
    w1 = g_p / (1.0 + r2)
    w2 = g_p * r2 / (1.0 + r2)
    j1 = i1 - e0
    j2 = i2 - e0
    jlo = jnp.minimum(j1, j2)
    jhi = jnp.maximum(j1, j2)
    wlo = jnp.where(j1 < j2, w1, w2)
    whi = jnp.where(j1 < j2, w2, w1)
    pair = jlo * (2 * n_per - 1 - jlo) * 0.5 + (jhi - jlo - 1.0)
    bucket = gidx * (n_per * (n_per - 1) // 2) + pair
    meta = jnp.where(lane == 0, bucket, jnp.where(lane == 1, wlo, jnp.where(lane == 2, whi, 0.0)))
    meta_ref[...] = meta.T[0:8, :]


def _post(lat_only, ys, u, lr, xall, mods, layer, s5_d, w_glu_bf, b_glu, w_out_bf, g2, wr2, br, n_groups, n_per,
          tm):
    nb, n, d = xall.shape
    d_s5 = ys.shape[-1]
    off = 1 if lat_only else 0
    nt = n // tm - off
    tok = lambda w: pl.BlockSpec((None, tm, w), lambda b, i: (b, i + off, 0))
    otok = pl.BlockSpec((None, tm, d), lambda b, i: (b, i, 0))
    if lat_only:
        ms = lambda part: pl.BlockSpec((None, None, 1, d), lambda b, i: (layer, b, 0, part))
    else:
        ms = lambda part: _mod_spec(layer, part, d, nb)
    full = lambda a: pl.BlockSpec(a.shape, lambda b, i: (0,) * a.ndim)
    consts = [s5_d.reshape(1, d_s5), w_glu_bf, b_glu.reshape(1, d_s5), w_out_bf, g2.reshape(1, d), wr2, br]
    return pl.pallas_call(
        functools.partial(_post_kernel, n_groups, n_per),
        grid=(nb, nt),
        in_specs=[tok(d_s5), tok(d_s5), tok(lr.shape[-1]), tok(d), ms(2), ms(3), ms(4)] + [full(a) for a in consts],
        out_specs=[otok, otok, pl.BlockSpec((None, None, 8, tm), lambda b, i: (b, i, 0, 0))],
        out_shape=[jax.ShapeDtypeStruct((nb, nt * tm, d), F32), jax.ShapeDtypeStruct((nb, nt * tm, d), F32),
                   jax.ShapeDtypeStruct((nb, nt, 8, tm), F32)],
        compiler_params=_cparams("parallel", "parallel"),
        name="post_mixer",
    )(ys, u, lr, xall, mods, mods, mods, *consts)


def _moe_kernel(n_rows, tb_ref, src_ref, hm_hbm, wt_ref, w1a, w3a, w2a, w1b, w3b, w2b, y_hbm, xbuf, ybuf, gsem,
                ssem):
    t = pl.program_id(0)
    tmr = xbuf.shape[0]

    @pl.when(t == 0)
    def _():
        xbuf[...] = jnp.zeros_like(xbuf)

    @pl.when(tb_ref[2, t] > 0)
    def _():
        def gather(r):
            return pltpu.make_async_copy(hm_hbm.at[pl.ds(src_ref[0, r], 1), :], xbuf.at[pl.ds(r, 1), :], gsem)

        def scatter(r):
            return pltpu.make_async_copy(ybuf.at[pl.ds(r, 1), :], y_hbm.at[pl.ds(src_ref[0, r], 1), :], ssem)

        def each(fn):
            def body(r, c):
                @pl.when(src_ref[0, r] < n_rows)
                def _():
                    fn(r)
                return c
            lax.fori_loop(0, tmr, body, 0)

        each(lambda r: gather(r).start())
        each(lambda r: gather(r).wait())

        xb = xbuf[...].astype(BF16)
        wt = wt_ref[...]

        def expert(w1, w3, w2, gate):
            h1 = jnp.dot(xb, w1[...], preferred_element_type=F32)
            h3 = jnp.dot(xb, w3[...], preferred_element_type=F32)
            hid = (h1 * jax.nn.sigmoid(h1)) * h3
            return gate * jnp.dot(hid.astype(BF16), w2[...], preferred_element_type=F32)

        half = LANES // 2
        ybuf[...] = expert(w1a, w3a, w2a, wt[:, 0:1]) + expert(w1b, w3b, w2b, wt[:, half:half + 1])

        each(lambda r: scatter(r).start())
        each(lambda r: scatter(r).wait())


def _moe(hm2, bucket, wlo, whi, tok_rows, w1_bf, w3_bf, w2_bf, n_per, tmr):
    n_rows, d = hm2.shape
    t = bucket.shape[0]
    npairs = n_per * (n_per - 1) // 2
    nbuck = (w1_bf.shape[0] // n_per) * npairs
    ntiles = t // tmr + nbuck

    order = jnp.argsort(bucket, stable=True)
    sb = bucket[order]
    counts = jnp.zeros((nbuck,), jnp.int32).at[bucket].add(1)
    padded = ((counts + tmr - 1) // tmr) * tmr
    pstart = jnp.cumsum(padded) - padded
    cstart = jnp.cumsum(counts) - counts
    slot = pstart[sb] + (jnp.arange(t, dtype=jnp.int32) - cstart[sb])
    src = jnp.full((ntiles * tmr,), n_rows, jnp.int32).at[slot].set(tok_rows[order])
    wts = jnp.zeros((ntiles * tmr, 2), F32).at[slot].set(jnp.stack([wlo[order], whi[order]], -1))
    wts = jnp.repeat(wts, LANES // 2, axis=1)
    tile_start = jnp.arange(ntiles, dtype=jnp.int32) * tmr
    pend = pstart + padded
    tb = jnp.minimum(jnp.searchsorted(pend, tile_start, side='right'), nbuck - 1).astype(jnp.int32)
    valid = (tile_start < pend[-1]).astype(jnp.int32)
    last_used = jnp.maximum(pend[-1] // tmr - 1, 0)
    tb = jnp.where(valid > 0, tb, tb[last_used])
    grp, pr = tb // npairs, tb % npairs
    pairs = [(i, j) for i in range(n_per) for j in range(i + 1, n_per)]
    plo = jnp.array([p[0] for p in pairs], jnp.int32)[pr]
    phi = jnp.array([p[1] for p in pairs], jnp.int32)[pr]
    tinfo = jnp.stack([grp * n_per + plo, grp * n_per + phi, valid])

    wspec = lambda a, row: pl.BlockSpec((None,) + a.shape[1:], lambda i, tb_: (tb_[row, i], 0, 0))
    grid_spec = pltpu.PrefetchScalarGridSpec(
        num_scalar_prefetch=1,
        grid=(ntiles,),
        in_specs=[pl.BlockSpec((None, 1, tmr), lambda i, tb_: (i, 0, 0), memory_space=pltpu.SMEM),
                  pl.BlockSpec(memory_space=pl.ANY),
                  pl.BlockSpec((tmr, LANES), lambda i, tb_: (i, 0)),
                  wspec(w1_bf, 0), wspec(w3_bf, 0), wspec(w2_bf, 0),
                  wspec(w1_bf, 1), wspec(w3_bf, 1), wspec(w2_bf, 1)],
        out_specs=pl.BlockSpec(memory_space=pl.ANY),
        scratch_shapes=[pltpu.VMEM((tmr, d), F32), pltpu.VMEM((tmr, d), F32),
                        pltpu.SemaphoreType.DMA(()), pltpu.SemaphoreType.DMA(())],
    )
    return pl.pallas_call(
        functools.partial(_moe_kernel, n_rows),
        grid_spec=grid_spec,
        out_shape=jax.ShapeDtypeStruct((n_rows, d), F32),
        compiler_params=_cparams("arbitrary"),
        name="moe_pairs",
    )(tinfo, src.reshape(ntiles, 1, tmr), hm2, wts, w1_bf, w3_bf, w2_bf, w1_bf, w3_bf, w2_bf)


def _final_kernel(x_ref, y_ref, ga_ref, g_ref, o_ref):
    o_ref[...] = _rms(x_ref[...] + ga_ref[...] * y_ref[...], g_ref[...])


def _final(x1, y, mods, layer, final_g, tm):
    nb, seq, d = x1.shape
    tok = pl.BlockSpec((None, tm, d), lambda b, i: (b, i, 0))
    return pl.pallas_call(
        _final_kernel,
        grid=(nb, seq // tm),
        in_specs=[tok, tok, _mod_spec_lat(layer, 5, d), pl.BlockSpec((1, d), lambda b, i: (0, 0))],
        out_specs=tok,
        out_shape=jax.ShapeDtypeStruct((nb, seq, d), F32),
        compiler_params=_cparams("parallel", "parallel"),
        name="final_norm",
    )(x1, y, mods, final_g.reshape(1, d))


def kernel(x, c, ctx, c_ctx, w_mod, b_mod, norm1_g, norm2_g, w_in, w_out, s5_a_re, s5_a_im, s5_log_dt, s5_b_re, s5_b_im, s5_c_re, s5_c_im, s5_d, s5_w_glu, s5_b_glu, lru_conv_w, lru_conv_b, lru_w_a, lru_b_a, lru_w_x, lru_b_x, lru_lam, moe_w_group, moe_b_group, moe_w_router, moe_b_router, moe_w1, moe_w3, moe_w2, final_g):
    nb, seq, d = x.shape
    n_ctx = ctx.shape[1]
    depth = w_mod.shape[0]
    n = n_ctx + seq
    d_s5 = s5_d.shape[-1]
    d_lru = lru_conv_b.shape[-1]
    ngrp, nh = s5_b_re.shape[2], s5_b_re.shape[4]
    n_groups, n_per = moe_w_router.shape[1], moe_w_router.shape[3]
    rows = seq // GRID_W
    tm = n_ctx
    q = S5_CHUNK
    nc_ctx, nc_lat = n_ctx // q, seq // q
    moe_rows = 256 if nb * seq >= 16384 else 32
    assert seq % GRID_W == 0 and seq % tm == 0 and n_ctx % LRU_STEPS == 0 and seq % LRU_STEPS == 0
    assert d_s5 == ngrp * nh and q * nh == 2 * LANES and n_groups + n_groups * n_per <= LANES

    pad = (-(nb + 1)) % 8
    c_rows = jnp.concatenate([c, c_ctx[None, :], jnp.zeros((pad, d), F32)], axis=0)
    mods = _modulation(c_rows, w_mod, b_mod).reshape(depth, nb + 1 + pad, 1, 6 * d)

    xall = jnp.concatenate([ctx, x], axis=1)
    y = None
    for l in range(depth):
        need_ctx = l < depth - 1
        xall, u, r, qg = _inproj(xall, y, mods, l, norm1_g[l], w_in[l].astype(BF16), d_s5, d_lru, tm)

        u_g = u.reshape(nb, nc_ctx + nc_lat, q, ngrp, nh).transpose(3, 1, 0, 2, 4)
        u_g = u_g.reshape(ngrp, (nc_ctx + nc_lat) * nb, q * nh).astype(BF16)
        w1, wof, wob, a16 = _s5_weights(s5_a_re[l], s5_a_im[l], s5_log_dt[l], s5_b_re[l], s5_b_im[l],
                                         s5_c_re[l], s5_c_im[l])
        y_g = _s5_scan(u_g, w1, wof, wob, a16, nb, nc_ctx, nc_lat)
        ys = y_g.reshape(ngrp, nc_ctx + nc_lat, nb, q, nh).transpose(2, 1, 3, 0, 4).reshape(nb, n, d_s5)

        def to_tm(a):
            a_ctx = a[:, :n_ctx].transpose(1, 0, 2)
            a_lat = a[:, n_ctx:].reshape(nb, rows, GRID_W, d_lru).transpose(2, 1, 0, 3).reshape(seq, nb, d_lru)
            return jnp.concatenate([a_ctx, a_lat], axis=0)

        r_tm, q_tm = to_tm(r), to_tm(qg)
        cneg = (-LRU_C * jax.nn.softplus(-lru_lam[l].astype(F32))).reshape(2, 1, d_lru)
        cw, cb = lru_conv_w[l].astype(F32), lru_conv_b[l].reshape(1, d_lru).astype(F32)
        gates = [(_block_diag_halves(lru_w_a[l, dd]), lru_b_a[l, dd].reshape(1, d_lru),
                  _block_diag_halves(lru_w_x[l, dd]), lru_b_x[l, dd].reshape(1, d_lru), cneg[dd]) for dd in range(2)]
        h_b = _lru_pass(True, r_tm, None, None, cw, cb, *gates[1], n_ctx)
        lr_tm = _lru_pass(False, r_tm, q_tm, h_b, cw, cb, *gates[0], n_ctx)
        lr_ctx = lr_tm[:n_ctx].transpose(1, 0, 2)
        lr_lat = lr_tm[n_ctx:].reshape(GRID_W, rows, nb, d_lru).transpose(2, 1, 0, 3).reshape(nb, seq, d_lru)
        lr = jnp.concatenate([lr_ctx, lr_lat], axis=1)

        wr = jnp.concatenate([moe_w_group[l], moe_w_router[l].transpose(1, 0, 2).reshape(d, n_groups * n_per)], -1)
        wr = jnp.pad(wr.astype(F32), ((0, 0), (0, LANES - wr.shape[-1])))
        wr_hi = wr.astype(BF16)
        wr_lo = (wr - wr_hi.astype(F32)).astype(BF16)
        wr2 = jnp.concatenate([jnp.concatenate([wr_hi, wr_lo], -1),
                               jnp.concatenate([wr_hi, jnp.zeros_like(wr_lo)], -1)], axis=0)
        br = jnp.concatenate([moe_b_group[l], moe_b_router[l].reshape(-1)])
        br = jnp.pad(br.astype(F32), (0, LANES - br.shape[0])).reshape(1, LANES)

        x1, hm, meta = _post(not need_ctx, ys, u, lr, xall, mods, l, s5_d[l], s5_w_glu[l].astype(BF16),
                             s5_b_glu[l], w_out[l].astype(BF16), norm2_g[l], wr2, br, n_groups, n_per, tm)

        n_tok = hm.shape[0] * hm.shape[1]
        meta_t = meta[:, :, :3, :].transpose(2, 0, 1, 3).reshape(3, n_tok)
        bucket = meta_t[0].astype(jnp.int32)
        y = _moe(hm.reshape(n_tok, d), bucket, meta_t[1], meta_t[2], jnp.arange(n_tok, dtype=jnp.int32),
                 moe_w1[l].astype(BF16), moe_w3[l].astype(BF16), moe_w2[l].astype(BF16), n_per, moe_rows)
        y = y.reshape(x1.shape)
        xall = x1
    return _final(xall, y, mods, depth - 1, final_g, tm)
```

```python
import functools
import math

import jax
import jax.numpy as jnp
from jax import lax
from jax.experimental import pallas as pl
from jax.experimental.pallas import tpu as pltpu

F32 = jnp.float32
BF16 = jnp.bfloat16
HIGHEST = lax.Precision.HIGHEST

EPS = 1e-6
GRID_W = 64
LRU_C = 8.0
CONV_LEFT = 2
S5_CHUNK = 16
LRU_STEPS = 32
LANES = 128
VMEM_LIMIT = 56 * 1024 * 1024


def _cparams(*sem):
    return pltpu.CompilerParams(dimension_semantics=sem, vmem_limit_bytes=VMEM_LIMIT)


def _rms(x, g):
    return x * lax.rsqrt(jnp.mean(x * x, axis=-1, keepdims=True) + EPS) * g


def _mod_kernel(c_ref, w_ref, b_ref, o_ref):
    c = c_ref[...]
    s = c * jax.nn.sigmoid(c)
    o_ref[...] = jnp.dot(s, w_ref[...], preferred_element_type=F32, precision=HIGHEST) + b_ref[...]


def _modulation(c_rows, w_mod, b_mod):
    depth, d, n6 = w_mod.shape
    rows = c_rows.shape[0]
    tn = n6 // 4
    return pl.pallas_call(
        _mod_kernel,
        grid=(depth, n6 // tn),
        in_specs=[pl.BlockSpec((rows, d), lambda l, j: (0, 0)),
                  pl.BlockSpec((None, d, tn), lambda l, j: (l, 0, j)),
                  pl.BlockSpec((None, 1, tn), lambda l, j: (l, 0, j))],
        out_specs=pl.BlockSpec((None, rows, tn), lambda l, j: (l, 0, j)),
        out_shape=jax.ShapeDtypeStruct((depth, rows, n6), F32),
        compiler_params=_cparams("parallel", "parallel"),
        name="modulation",
    )(c_rows, w_mod, b_mod.reshape(depth, 1, n6))


def _mod_spec(layer, part, d, nb):
    return pl.BlockSpec((None, None, 1, d),
                        lambda b, i: (layer, jnp.where(i == 0, nb, b), 0, part))


def _mod_spec_lat(layer, part, d):
    return pl.BlockSpec((None, None, 1, d), lambda b, i: (layer, b, 0, part))


def _tile_of(split, refs):
    if split:
        return jnp.where(pl.program_id(1) == 0, refs[0][...], refs[1][...]), refs[2:]
    return refs[0][...], refs[1:]


def _tile_specs(xs, tm):
    d = xs[0].shape[-1]
    if len(xs) == 2:
        return [pl.BlockSpec((None, tm, d), lambda b, i: (b, 0, 0)),
                pl.BlockSpec((None, tm, d), lambda b, i: (b, jnp.maximum(i - 1, 0), 0))]
    return [pl.BlockSpec((None, tm, d), lambda b, i: (b, i, 0))]


def _inproj_kernel(has_y, split, d_s5, d_lru, *refs):
    x, refs = _tile_of(split, refs)
    if has_y:
        y_ref, ga_ref, sh_ref, sc_ref, g_ref, w_ref, xo_ref, u_ref, r_ref, q_ref = refs
        x = x + ga_ref[...] * y_ref[...]
        xo_ref[...] = x
    else:
        sh_ref, sc_ref, g_ref, w_ref, u_ref, r_ref, q_ref = refs
    h = _rms(x, g_ref[...]) * (1.0 + sc_ref[...]) + sh_ref[...]
    p = jnp.dot(h.astype(BF16), w_ref[...], preferred_element_type=F32)
    u_ref[...] = p[:, :d_s5]
    r_ref[...] = p[:, d_s5:d_s5 + d_lru]
    q_ref[...] = p[:, d_s5 + d_lru:]


def _inproj(xs, y, mods, layer, g1, w_in_bf, d_s5, d_lru, tm):
    nb, d = xs[0].shape[0], xs[0].shape[-1]
    n = sum(a.shape[1] for a in xs)
    has_y = y is not None
    tok = lambda w: pl.BlockSpec((None, tm, w), lambda b, i: (b, i, 0))
    in_specs = _tile_specs(xs, tm)
    args = list(xs)
    if has_y:
        in_specs += [pl.BlockSpec((tm, d), lambda b, i: (b * (n // tm) + i, 0)), _mod_spec(layer - 1, 5, d, nb)]
        args += [y, mods]
    in_specs += [_mod_spec(layer, 0, d, nb), _mod_spec(layer, 1, d, nb),
                 pl.BlockSpec((1, d), lambda b, i: (0, 0)),
                 pl.BlockSpec(w_in_bf.shape, lambda b, i: (0, 0))]
    args += [mods, mods, g1.reshape(1, d), w_in_bf]
    out_specs = [tok(d_s5), tok(d_lru), tok(d_lru)]
    out_shape = [jax.ShapeDtypeStruct((nb, n, d_s5), F32),
                 jax.ShapeDtypeStruct((nb, n, d_lru), F32),
                 jax.ShapeDtypeStruct((nb, n, d_lru), F32)]
    if has_y:
        out_specs = [tok(d)] + out_specs
        out_shape = [jax.ShapeDtypeStruct((nb, n, d), F32)] + out_shape
    res = pl.pallas_call(
        functools.partial(_inproj_kernel, has_y, len(xs) == 2, d_s5, d_lru),
        grid=(nb, n // tm), in_specs=in_specs, out_specs=out_specs, out_shape=out_shape,
        compiler_params=_cparams("parallel", "parallel"),
        name="inproj",
    )(*args)
    if has_y:
        return [[res[0]]] + list(res[1:])
    return [xs] + list(res)


def _s5_weights(a_re, a_im, log_dt, b_re, b_im, c_re, c_im):
    q = S5_CHUNK
    lam = lax.complex(a_re.astype(F32), a_im.astype(F32))
    dt = jnp.exp(log_dt.astype(F32))[..., None]
    a_bar = jnp.exp(lam * dt)
    bmat = lax.complex(b_re.astype(F32), b_im.astype(F32))
    b_bar = ((a_bar - 1) / lam)[..., None] * bmat
    cmat = lax.complex(c_re.astype(F32), c_im.astype(F32))
    k = jnp.arange(q + 1, dtype=F32)
    apow = jnp.exp((lam * dt)[..., None] * k)
    ngrp, nst = a_re.shape[1], a_re.shape[2]
    nh = b_re.shape[-1]

    kern = jnp.einsum('dgop,dgpk,dgpi->dgkoi', cmat, apow[..., :q], b_bar, precision=HIGHEST).real
    sig = jnp.arange(q)[:, None]
    tau = jnp.arange(q)[None, :]
    lag_f = jnp.clip(tau - sig, 0, q - 1)
    lag_b = jnp.clip(sig - tau, 0, q - 1)
    tf = jnp.where((sig <= tau)[None, :, :, None, None], kern[0][:, lag_f], 0.0)
    tb = jnp.where((sig >= tau)[None, :, :, None, None], kern[1][:, lag_b], 0.0)
    toep = (tf + tb).transpose(0, 1, 4, 2, 3).reshape(ngrp, q * nh, q * nh)

    win_f = jnp.einsum('gps,gpi->gsip', apow[0][..., :q][..., ::-1], b_bar[0])
    win_b = jnp.einsum('gps,gpi->gsip', apow[1][..., :q], b_bar[1])
    win = jnp.concatenate([win_f.real, win_b.real, win_f.imag, win_b.imag], axis=-1)
    win = win.reshape(ngrp, q * nh, 4 * nst)
    w1 = jnp.concatenate([toep, win], axis=-1)

    cf = jnp.einsum('gop,gpt->gpto', cmat[0], apow[0][..., 1:])
    cb = jnp.einsum('gop,gpt->gpto', cmat[1], apow[1][..., 1:][..., ::-1])
    z = jnp.zeros_like(cf.real)
    wof = jnp.concatenate([cf.real, z, -cf.imag, z], axis=1).reshape(ngrp, 4 * nst, q * nh)
    wob = jnp.concatenate([z, cb.real, z, -cb.imag], axis=1).reshape(ngrp, 4 * nst, q * nh)

    aq = apow[..., q]
    a16 = jnp.stack([jnp.concatenate([aq[0].real, aq[1].real], -1),
                     jnp.concatenate([aq[0].imag, aq[1].imag], -1)], axis=1)
    return w1.astype(BF16), wof.astype(BF16), wob.astype(BF16), a16.astype(F32)


def _s5_kernel(nb, nc_ctx, nc_lat, mt, u_ref, w1_ref, wof_ref, wob_ref, a_ref, y_ref, s_ref, hf_ref, hb_ref):
    m, qh = u_ref.shape
    ns2 = a_ref.shape[-1]
    w1 = w1_ref[...]

    def intra(i, c):
        r0 = pl.multiple_of(i * mt, mt)
        res = jnp.dot(u_ref[pl.ds(r0, mt), :], w1, preferred_element_type=F32)
        y_ref[pl.ds(r0, mt), :] = res[:, :qh]
        s_ref[pl.ds(r0, mt), :] = res[:, qh:]
        return c

    lax.fori_loop(0, m // mt, intra, 0)

    are = jnp.broadcast_to(a_ref[0:1, :], (nb, ns2))
    aim = jnp.broadcast_to(a_ref[1:2, :], (nb, ns2))
    fwd_lane = lax.broadcasted_iota(jnp.int32, (nb, ns2), 1) < ns2 // 2

    def scan(base, n, carry):
        def step(k, hc):
            hre, him = hc
            rf = pl.multiple_of((base + k) * nb, nb)
            rb = pl.multiple_of((base + n - 1 - k) * nb, nb)
            hf_ref[pl.ds(rf, nb), 0:ns2] = hre
            hf_ref[pl.ds(rf, nb), ns2:2 * ns2] = him
            hb_ref[pl.ds(rb, nb), 0:ns2] = hre
            hb_ref[pl.ds(rb, nb), ns2:2 * ns2] = him
            sre = jnp.where(fwd_lane, s_ref[pl.ds(rf, nb), 0:ns2], s_ref[pl.ds(rb, nb), 0:ns2])
            sim = jnp.where(fwd_lane, s_ref[pl.ds(rf, nb), ns2:2 * ns2], s_ref[pl.ds(rb, nb), ns2:2 * ns2])
            return (are * hre - aim * him + sre, are * him + aim * hre + sim)
        return lax.fori_loop(0, n, step, carry)

    zero = jnp.zeros((nb, ns2), F32)
    carry = scan(0, nc_ctx, (zero, zero))
    scan(nc_ctx, nc_lat, carry)

    wof = wof_ref[...]
    wob = wob_ref[...]

    def inter(i, c):
        r0 = pl.multiple_of(i * mt, mt)
        y_ref[pl.ds(r0, mt), :] += (
            jnp.dot(hf_ref[pl.ds(r0, mt), :].astype(BF16), wof, preferred_element_type=F32)
            + jnp.dot(hb_ref[pl.ds(r0, mt), :].astype(BF16), wob, preferred_element_type=F32))
        return c

    lax.fori_loop(0, m // mt, inter, 0)


def _s5_scan(u_g, w1, wof, wob, a16, nb, nc_ctx, nc_lat):
    ngrp, m, qh = u_g.shape
    ns4 = wof.shape[1]
    mt = nb * math.gcd(nc_ctx + nc_lat, 16)
    grp = lambda a: pl.BlockSpec((None,) + a.shape[1:], lambda g: (g, 0, 0))
    return pl.pallas_call(
        functools.partial(_s5_kernel, nb, nc_ctx, nc_lat, mt),
        grid=(ngrp,),
        in_specs=[grp(u_g), grp(w1), grp(wof), grp(wob), grp(a16)],
        out_specs=pl.BlockSpec((None, m, qh), lambda g: (g, 0, 0)),
        out_shape=jax.ShapeDtypeStruct((ngrp, m, qh), F32),
        scratch_shapes=[pltpu.VMEM((m, ns4), F32), pltpu.VMEM((m, ns4), F32), pltpu.VMEM((m, ns4), F32)],
        compiler_params=_cparams("parallel"),
        name="s5_chunked",
    )(u_g, w1, wof, wob, a16)


def _lru_chunk(s, reverse, nc_ctx, nc):
    if not reverse:
        return s
    return jnp.where(s < nc_ctx, nc_ctx - 1 - s, nc - 1 - (s - nc_ctx))


def _lru_kernel(reverse, combine, nc_ctx, nc, rows, *refs):
    if combine:
        (r_hbm, q_hbm, ho_hbm, cw_ref, cb_ref, wa_ref, ba_ref, wx_ref, bx_ref, cn_ref, o_hbm,
         rbuf, obuf, h_ref, a_s, b_s, isem, osem, qbuf, hbuf) = refs
    else:
        (r_hbm, cw_ref, cb_ref, wa_ref, ba_ref, wx_ref, bx_ref, cn_ref, o_hbm,
         rbuf, obuf, h_ref, a_s, b_s, isem, osem) = refs
    _, jt, nb, ch = obuf.shape
    gw = r_hbm.shape[2]
    cblk = (nc_ctx * jt) // gw
    ncol = jt // rows
    s = pl.program_id(0)
    slot = lax.rem(s, 2)

    def chunk_of(step):
        return _lru_chunk(jnp.clip(step, 0, nc - 1), reverse, nc_ctx, nc)

    def main_copies(hbm, buf, row0, c, sl, sem, fn):
        @pl.when(c < nc_ctx)
        def _():
            j0 = c * jt
            for b in range(nb):
                fn(hbm.at[b, j0 // gw, pl.ds(j0 % gw, jt), :], buf.at[sl, pl.ds(row0, jt), b, :], sem.at[sl])

        @pl.when(c >= nc_ctx)
        def _():
            w0 = (c - nc_ctx) * ncol
            for b in range(nb):
                for k in range(ncol):
                    fn(hbm.at[b, pl.ds(cblk, rows), w0 + k, :],
                       buf.at[sl, pl.ds(row0 + k * rows, rows), b, :], sem.at[sl])

    def halo_copies(c, sl, fn):
        first = jnp.logical_or(c == 0, c == nc_ctx)
        last = jnp.logical_or(c == nc_ctx - 1, c == nc - 1)
        in_ctx = c < nc_ctx

        @pl.when(jnp.logical_and(in_ctx, jnp.logical_not(first)))
        def _():
            j = c * jt - CONV_LEFT
            for b in range(nb):
                fn(r_hbm.at[b, j // gw, pl.ds(j % gw, CONV_LEFT), :], rbuf.at[sl, pl.ds(0, CONV_LEFT), b, :],
                   isem.at[sl])

        @pl.when(jnp.logical_and(in_ctx, jnp.logical_not(last)))
        def _():
            j = c * jt + jt
            for b in range(nb):
                fn(r_hbm.at[b, j // gw, pl.ds(j % gw, 1), :], rbuf.at[sl, pl.ds(jt + CONV_LEFT, 1), b, :],
                   isem.at[sl])

        @pl.when(jnp.logical_and(jnp.logical_not(in_ctx), jnp.logical_not(first)))
        def _():
            w = (c - nc_ctx) * ncol - 1
            for b in range(nb):
                fn(r_hbm.at[b, pl.ds(cblk + rows - CONV_LEFT, CONV_LEFT), w, :],
                   rbuf.at[sl, pl.ds(0, CONV_LEFT), b, :], isem.at[sl])

        @pl.when(jnp.logical_and(jnp.logical_not(in_ctx), jnp.logical_not(last)))
        def _():
            w = (c - nc_ctx) * ncol + ncol
            for b in range(nb):
                fn(r_hbm.at[b, pl.ds(cblk, 1), w, :], rbuf.at[sl, pl.ds(jt + CONV_LEFT, 1), b, :], isem.at[sl])

    def loads(c, sl, fn):
        main_copies(r_hbm, rbuf, CONV_LEFT, c, sl, isem, fn)
        halo_copies(c, sl, fn)
        if combine:
            main_copies(q_hbm, qbuf, 0, c, sl, isem, fn)
            main_copies(ho_hbm, hbuf, 0, c, sl, isem, fn)

    def stores(c, sl, fn):
        main_copies(o_hbm, obuf, 0, c, sl, osem, lambda hbm, buf, sem: fn(buf, hbm, sem))

    start = lambda src, dst, sem: pltpu.make_async_copy(src, dst, sem).start()
    wait = lambda src, dst, sem: pltpu.make_async_copy(src, dst, sem).wait()

    c = chunk_of(s)
    first = jnp.logical_or(c == 0, c == nc_ctx)
    last = jnp.logical_or(c == nc_ctx - 1, c == nc - 1)

    @pl.when(s == 0)
    def _():
        h_ref[...] = jnp.zeros_like(h_ref)
        rbuf[...] = jnp.zeros_like(rbuf)
        loads(c, 0, start)

    @pl.when(s + 1 < nc)
    def _():
        loads(chunk_of(s + 1), 1 - slot, start)

    loads(c, slot, wait)

    @pl.when(s >= 2)
    def _():
        stores(chunk_of(s - 2), slot, wait)

    rb = rbuf[slot]
    prev = jnp.where(first, 0.0, rb[0:CONV_LEFT])
    nxt = jnp.where(last, 0.0, rb[jt + CONV_LEFT:jt + CONV_LEFT + 1])
    xe = jnp.concatenate([prev, rb[CONV_LEFT:jt + CONV_LEFT], nxt], axis=0)
    cw = cw_ref[...]
    xc = cb_ref[...].reshape(1, 1, ch)
    for k in range(cw.shape[0]):
        xc = xc + cw[k:k + 1, :].reshape(1, 1, ch) * xe[k:k + jt]
    xc2 = xc.reshape(jt * nb, ch)

    xb = xc2.astype(BF16)
    nblk, wb = wa_ref.shape[0], wa_ref.shape[1]
    za = jnp.concatenate([jnp.dot(xb[:, i * wb:(i + 1) * wb], wa_ref[i], preferred_element_type=F32)
                          for i in range(nblk)], axis=-1)
    zx = jnp.concatenate([jnp.dot(xb[:, i * wb:(i + 1) * wb], wx_ref[i], preferred_element_type=F32)
                          for i in range(nblk)], axis=-1)
    rg = jax.nn.sigmoid(za + ba_ref[...])
    ig = jax.nn.sigmoid(zx + bx_ref[...])
    log_a = cn_ref[...] * rg
    a = jnp.exp(log_a)
    bb = jnp.sqrt(1.0 - a * a) * (ig * xc2)
    a_s[...] = a.reshape(jt, nb, ch)
    b_s[...] = bb.reshape(jt, nb, ch)

    def step(t, h):
        tt = jt - 1 - t if reverse else t
        h = a_s[tt] * h + b_s[tt]
        if combine:
            obuf[slot, tt] = (h + hbuf[slot, tt]) * jax.nn.gelu(qbuf[slot, tt])
        else:
            obuf[slot, tt] = h
        return h

    h_ref[...] = lax.fori_loop(0, jt, step, h_ref[...])

    stores(c, slot, start)

    @pl.when(s == nc - 1)
    def _():
        if nc >= 2:
            stores(chunk_of(s - 1), 1 - slot, wait)
        stores(c, slot, wait)


def _lru_pass(reverse, r4, q4, h_other4, conv_w, conv_b, wa, ba, wx, bx, cneg, n_ctx, rows):
    nb, nblk, gw, ch = r4.shape
    jt = LRU_STEPS
    nc, nc_ctx = (nblk * gw) // jt, n_ctx // jt
    combine = h_other4 is not None
    assert gw % jt == 0 and n_ctx % gw == 0 and jt % rows == 0 and rows >= CONV_LEFT
    anyspec = pl.BlockSpec(memory_space=pl.ANY)
    full = lambda a: pl.BlockSpec(a.shape, lambda s: (0,) * a.ndim)
    consts = [conv_w, conv_b, wa, ba, wx, bx, cneg]
    big = [r4, q4, h_other4] if combine else [r4]
    buf = lambda n: pltpu.VMEM((2, n, nb, ch), F32)
    scratch = [buf(jt + CONV_LEFT + 1), buf(jt), pltpu.VMEM((nb, ch), F32),
               pltpu.VMEM((jt, nb, ch), F32), pltpu.VMEM((jt, nb, ch), F32),
               pltpu.SemaphoreType.DMA((2,)), pltpu.SemaphoreType.DMA((2,))]
    if combine:
        scratch += [buf(jt), buf(jt)]
    return pl.pallas_call(
        functools.partial(_lru_kernel, reverse, combine, nc_ctx, nc, rows),
        grid=(nc,), in_specs=[anyspec] * len(big) + [full(a) for a in consts], out_specs=anyspec,
        out_shape=jax.ShapeDtypeStruct(r4.shape, F32),
        scratch_shapes=scratch,
        compiler_params=_cparams("arbitrary"),
        name="lru_bwd" if reverse else "lru_fwd",
    )(*big, *consts)


def _block_diag_halves(w):
    nh, hd, _ = w.shape
    half = nh // 2
    eye = jnp.eye(half, dtype=w.dtype)
    wh = w.reshape(2, half, hd, hd)
    bd = jnp.einsum('bhij,hk->bhikj', wh, eye).reshape(2, half * hd, half * hd)
    return bd.astype(BF16)


def _post_kernel(n_groups, n_per, split, *refs):
    x_in, refs = _tile_of(split, refs)
    (ys_ref, u_ref, lr_ref, ga_ref, sh_ref, sc_ref, d_ref, wg_ref, bg_ref,
     wo_ref, g2_ref, wr_ref, br_ref, x1_ref, hm_ref, meta_ref) = refs
    tm = x_in.shape[0]
    ys = ys_ref[...] + d_ref[...] * u_ref[...]
    z = jax.nn.gelu(ys)
    s5o = z * jax.nn.sigmoid(jnp.dot(z.astype(BF16), wg_ref[...], preferred_element_type=F32) + bg_ref[...])
    mix = jnp.concatenate([s5o.astype(BF16), lr_ref[...].astype(BF16)], axis=-1)
    x1 = x_in + ga_ref[...] * jnp.dot(mix, wo_ref[...], preferred_element_type=F32)
    x1_ref[...] = x1
    hm = _rms(x1, g2_ref[...]) * (1.0 + sc_ref[...]) + sh_ref[...]
    hm_ref[...] = hm

    hi = hm.astype(BF16)
    lo = (hm - hi.astype(F32)).astype(BF16)
    two = jnp.dot(jnp.concatenate([hi, lo], axis=-1), wr_ref[...], preferred_element_type=F32)
    logits = two[:, :LANES] + two[:, LANES:] + br_ref[...]

    lane = lax.broadcasted_iota(jnp.int32, (tm, LANES), 1).astype(F32)
    neg = jnp.float32(-jnp.inf)
    big = jnp.float32(LANES)
    lg = jnp.where(lane < n_groups, logits, neg)
    mg = jnp.max(lg, axis=-1, keepdims=True)
    g_p = 1.0 / jnp.sum(jnp.exp(lg - mg), axis=-1, keepdims=True)
    gidx = jnp.min(jnp.where(lg == mg, lane, big), axis=-1, keepdims=True)
    e0 = n_groups + n_per * gidx
    le = jnp.where(jnp.logical_and(lane >= e0, lane < e0 + n_per), logits, neg)
    m1 = jnp.max(le, axis=-1, keepdims=True)
    i1 = jnp.min(jnp.where(le == m1, lane, big), axis=-1, keepdims=True)
    le2 = jnp.where(lane == i1, neg, le)
    m2 = jnp.max(le2, axis=-1, keepdims=True)
    i2 = jnp.min(jnp.where(le2 == m2, lane, big), axis=-1, keepdims=True)
    r2 = jnp.exp(m2 - m1)
    w1 = g_p / (1.0 + r2)
    w2 = g_p * r2 / (1.0 + r2)
    j1 = i1 - e0
    j2 = i2 - e0
    jlo = jnp.minimum(j1, j2)
    jhi = jnp.maximum(j1, j2)
    wlo = jnp.where(j1 < j2, w1, w2)
    whi = jnp.where(j1 < j2, w2, w1)
    pair = jlo * (2 * n_per - 1 - jlo) * 0.5 + (jhi - jlo - 1.0)
    bucket = gidx * (n_per * (n_per - 1) // 2) + pair
    meta = jnp.where(lane == 0, bucket, jnp.where(lane == 1, wlo, jnp.where(lane == 2, whi, 0.0)))
    meta_ref[...] = meta.T[0:8, :]


def _post(lat_only, ys, u, lr, xs, mods, layer, s5_d, w_glu_bf, b_glu, w_out_bf, g2, wr2, br, n_groups, n_per,
          tm):
    nb, d = xs[0].shape[0], xs[0].shape[-1]
    n = sum(a.shape[1] for a in xs)
    d_s5 = ys.shape[-1]
    off = 1 if lat_only else 0
    assert not (lat_only and len(xs) == 2)
    nt = n // tm - off
    xspecs = [pl.BlockSpec((None, tm, d), lambda b, i: (b, i + off, 0))] if len(xs) == 1 else _tile_specs(xs, tm)
    tok = lambda w: pl.BlockSpec((None, tm, w), lambda b, i: (b, i + off, 0))
    otok = pl.BlockSpec((None, tm, d), lambda b, i: (b, i, 0))
    if lat_only:
        ms = lambda part: pl.BlockSpec((None, None, 1, d), lambda b, i: (layer, b, 0, part))
    else:
        ms = lambda part: _mod_spec(layer, part, d, nb)
    full = lambda a: pl.BlockSpec(a.shape, lambda b, i: (0,) * a.ndim)
    consts = [s5_d.reshape(1, d_s5), w_glu_bf, b_glu.reshape(1, d_s5), w_out_bf, g2.reshape(1, d), wr2, br]
    return pl.pallas_call(
        functools.partial(_post_kernel, n_groups, n_per, len(xs) == 2),
        grid=(nb, nt),
        in_specs=xspecs + [tok(d_s5), tok(d_s5), tok(lr.shape[-1]), ms(2), ms(3), ms(4)] + [full(a) for a in consts],
        out_specs=[otok, otok, pl.BlockSpec((None, None, 8, tm), lambda b, i: (b, i, 0, 0))],
        out_shape=[jax.ShapeDtypeStruct((nb, nt * tm, d), F32), jax.ShapeDtypeStruct((nb, nt * tm, d), F32),
                   jax.ShapeDtypeStruct((nb, nt, 8, tm), F32)],
        compiler_params=_cparams("parallel", "parallel"),
        name="post_mixer",
    )(*xs, ys, u, lr, mods, mods, mods, *consts)


def _moe_kernel(n_rows, nt, tb_ref, src0_ref, srcn_ref, dst_ref, hm_hbm, wt_ref, w1a, w3a, w2a, w1b, w3b, w2b,
                y_hbm, xbuf, ybuf, gsem, ssem):
    i = pl.program_id(0)
    slot = lax.rem(i, 2)
    tmr = xbuf.shape[1]

    def valid(j):
        return tb_ref[2, jnp.clip(j, 0, nt - 1)] > 0

    def start_gathers(ids_ref, s):
        for r in range(tmr):
            pltpu.make_async_copy(hm_hbm.at[pl.ds(ids_ref[0, r], 1), :], xbuf.at[s, pl.ds(r, 1), :],
                                  gsem.at[s]).start(priority=r % 2)

    def start_scatters(s):
        for r in range(tmr):
            pltpu.make_async_copy(ybuf.at[s, pl.ds(r, 1), :], y_hbm.at[pl.ds(dst_ref[0, r], 1), :],
                                  ssem.at[s]).start(priority=r % 2)

    def wait_gathers(s):
        pltpu.make_async_copy(hm_hbm.at[pl.ds(0, tmr), :], xbuf.at[s], gsem.at[s]).wait()

    def wait_scatters(s):
        pltpu.make_async_copy(ybuf.at[s], y_hbm.at[pl.ds(0, tmr), :], ssem.at[s]).wait()

    @pl.when(i == 0)
    def _():
        xbuf[...] = jnp.zeros_like(xbuf)
        ybuf[...] = jnp.zeros_like(ybuf)
        for s in range(2):
            cp = pltpu.make_async_copy(ybuf.at[s], y_hbm.at[pl.ds(n_rows + s * tmr, tmr), :], ssem.at[s])
            cp.start()
            cp.wait()
        start_gathers(src0_ref, 0)

    @pl.when(jnp.logical_and(i + 1 < nt, valid(i + 1)))
    def _():
        start_gathers(srcn_ref, 1 - slot)

    @pl.when(jnp.logical_and(i >= 2, valid(i - 2)))
    def _():
        wait_scatters(slot)

    @pl.when(valid(i))
    def _():
        wait_gathers(slot)
        xb = xbuf[slot].astype(BF16)
        wt = wt_ref[...]

        def expert(w1, w3, w2, gate):
            h1 = jnp.dot(xb, w1[...], preferred_element_type=F32)
            h3 = jnp.dot(xb, w3[...], preferred_element_type=F32)
            hid = (h1 * jax.nn.sigmoid(h1)) * h3
            return gate * jnp.dot(hid.astype(BF16), w2[...], preferred_element_type=F32)

        half = LANES // 2
        ybuf[slot] = expert(w1a, w3a, w2a, wt[:, 0:1]) + expert(w1b, w3b, w2b, wt[:, half:half + 1])
        start_scatters(slot)

    @pl.when(i == nt - 1)
    def _():
        @pl.when(jnp.logical_and(nt >= 2, valid(i - 1)))
        def _():
            wait_scatters(1 - slot)

        @pl.when(valid(i))
        def _():
            wait_scatters(slot)


def _moe(hm2, bucket, wlo, whi, tok_rows, w1_bf, w3_bf, w2_bf, n_per, tmr):
    n_rows, d = hm2.shape
    t = bucket.shape[0]
    npairs = n_per * (n_per - 1) // 2
    nbuck = (w1_bf.shape[0] // n_per) * npairs
    ntiles = t // tmr + nbuck

    _, s_tok, s_lo, s_hi = lax.sort((bucket, tok_rows, wlo, whi), num_keys=1, is_stable=True)
    counts = jnp.sum((bucket[None, :] == jnp.arange(nbuck, dtype=jnp.int32)[:, None]).astype(jnp.int32), axis=1)
    padded = ((counts + tmr - 1) // tmr) * tmr
    pend = jnp.cumsum(padded)
    cend = jnp.cumsum(counts)
    shift = (pend - padded) - (cend - counts)
    tile_start = jnp.arange(ntiles, dtype=jnp.int32) * tmr
    valid = (tile_start < pend[-1]).astype(jnp.int32)
    tb = jnp.sum((tile_start[:, None] >= pend[None, :]).astype(jnp.int32), axis=1)
    tb = jnp.minimum(tb, jnp.sum((pend < pend[-1]).astype(jnp.int32)))
    tb = jnp.minimum(tb, nbuck - 1)
    pos = (tile_start[:, None] + jnp.arange(tmr, dtype=jnp.int32)[None, :]) - shift[tb][:, None]
    real = jnp.logical_and(pos < cend[tb][:, None], valid[:, None] > 0)
    pos = jnp.clip(pos, 0, t - 1)
    src = jnp.where(real, s_tok[pos], 0)
    spare = n_rows + (jnp.arange(ntiles, dtype=jnp.int32) % 2)[:, None] * tmr + jnp.arange(tmr, dtype=jnp.int32)
    dst = jnp.where(real, s_tok[pos], spare)
    g_lo = jnp.where(real, s_lo[pos], 0.0).reshape(-1, 1)
    g_hi = jnp.where(real, s_hi[pos], 0.0).reshape(-1, 1)
    half = LANES // 2
    wts = jnp.concatenate([jnp.broadcast_to(g_lo, (ntiles * tmr, half)),
                           jnp.broadcast_to(g_hi, (ntiles * tmr, half))], axis=1)
    grp, pr = tb // npairs, tb % npairs
    pairs = [(i, j) for i in range(n_per) for j in range(i + 1, n_per)]
    plo = jnp.array([p[0] for p in pairs], jnp.int32)[pr]
    phi = jnp.array([p[1] for p in pairs], jnp.int32)[pr]
    tinfo = jnp.stack([grp * n_per + plo, grp * n_per + phi, valid])

    wspec = lambda a, row: pl.BlockSpec((None,) + a.shape[1:], lambda i, tb_: (tb_[row, i], 0, 0))
    ids = lambda f: pl.BlockSpec((None, 1, tmr), lambda i, tb_: (f(i), 0, 0), memory_space=pltpu.SMEM)
    grid_spec = pltpu.PrefetchScalarGridSpec(
        num_scalar_prefetch=1,
        grid=(ntiles,),
        in_specs=[ids(lambda i: 0), ids(lambda i: jnp.minimum(i + 1, ntiles - 1)), ids(lambda i: i),
                  pl.BlockSpec(memory_space=pl.ANY),
                  pl.BlockSpec((tmr, LANES), lambda i, tb_: (i, 0)),
                  wspec(w1_bf, 0), wspec(w3_bf, 0), wspec(w2_bf, 0),
                  wspec(w1_bf, 1), wspec(w3_bf, 1), wspec(w2_bf, 1)],
        out_specs=pl.BlockSpec(memory_space=pl.ANY),
        scratch_shapes=[pltpu.VMEM((2, tmr, d), F32), pltpu.VMEM((2, tmr, d), F32),
                        pltpu.SemaphoreType.DMA((2,)), pltpu.SemaphoreType.DMA((2,))],
    )
    src3 = src.reshape(ntiles, 1, tmr)
    return pl.pallas_call(
        functools.partial(_moe_kernel, n_rows, ntiles),
        grid_spec=grid_spec,
        out_shape=jax.ShapeDtypeStruct((n_rows + 2 * tmr, d), F32),
        compiler_params=_cparams("arbitrary"),
        name="moe_pairs",
    )(tinfo, src3, src3, dst.reshape(ntiles, 1, tmr), hm2, wts, w1_bf, w3_bf, w2_bf, w1_bf, w3_bf, w2_bf)


def _final_kernel(x_ref, y_ref, ga_ref, g_ref, o_ref):
    o_ref[...] = _rms(x_ref[...] + ga_ref[...] * y_ref[...], g_ref[...])


def _final(x1, y, mods, layer, final_g, tm):
    nb, seq, d = x1.shape
    tok = pl.BlockSpec((None, tm, d), lambda b, i: (b, i, 0))
    return pl.pallas_call(
        _final_kernel,
        grid=(nb, seq // tm),
        in_specs=[tok, pl.BlockSpec((tm, d), lambda b, i: (b * (seq // tm) + i, 0)),
                  _mod_spec_lat(layer, 5, d), pl.BlockSpec((1, d), lambda b, i: (0, 0))],
        out_specs=tok,
        out_shape=jax.ShapeDtypeStruct((nb, seq, d), F32),
        compiler_params=_cparams("parallel", "parallel"),
        name="final_norm",
    )(x1, y, mods, final_g.reshape(1, d))


def kernel(x, c, ctx, c_ctx, w_mod, b_mod, norm1_g, norm2_g, w_in, w_out, s5_a_re, s5_a_im, s5_log_dt, s5_b_re, s5_b_im, s5_c_re, s5_c_im, s5_d, s5_w_glu, s5_b_glu, lru_conv_w, lru_conv_b, lru_w_a, lru_b_a, lru_w_x, lru_b_x, lru_lam, moe_w_group, moe_b_group, moe_w_router, moe_b_router, moe_w1, moe_w3, moe_w2, final_g):
    nb, seq, d = x.shape
    n_ctx = ctx.shape[1]
    depth = w_mod.shape[0]
    n = n_ctx + seq
    d_s5 = s5_d.shape[-1]
    d_lru = lru_conv_b.shape[-1]
    ngrp, nh = s5_b_re.shape[2], s5_b_re.shape[4]
    n_groups, n_per = moe_w_router.shape[1], moe_w_router.shape[3]
    rows = seq // GRID_W
    tm = n_ctx
    q = S5_CHUNK
    nc_ctx, nc_lat = n_ctx // q, seq // q
    moe_rows = 256 if nb * seq >= 16384 else 32
    assert seq % GRID_W == 0 and seq % tm == 0 and n_ctx % GRID_W == 0 and seq % LRU_STEPS == 0
    assert d_s5 == ngrp * nh and q * nh == 2 * LANES and n_groups + n_groups * n_per <= LANES

    pad = (-(nb + 1)) % 8
    c_rows = jnp.concatenate([c, c_ctx[None, :], jnp.zeros((pad, d), F32)], axis=0)
    mods = _modulation(c_rows, w_mod, b_mod).reshape(depth, nb + 1 + pad, 1, 6 * d)

    xs = [ctx, x]
    y = None
    for l in range(depth):
        need_ctx = l < depth - 1
        xs, u, r, qg = _inproj(xs, y, mods, l, norm1_g[l], w_in[l].astype(BF16), d_s5, d_lru, tm)

        u_g = u.reshape(nb, nc_ctx + nc_lat, q, ngrp, nh).transpose(3, 1, 0, 2, 4)
        u_g = u_g.reshape(ngrp, (nc_ctx + nc_lat) * nb, q * nh).astype(BF16)
        w1, wof, wob, a16 = _s5_weights(s5_a_re[l], s5_a_im[l], s5_log_dt[l], s5_b_re[l], s5_b_im[l],
                                         s5_c_re[l], s5_c_im[l])
        y_g = _s5_scan(u_g, w1, wof, wob, a16, nb, nc_ctx, nc_lat)
        ys = y_g.reshape(ngrp, nc_ctx + nc_lat, nb, q, nh).transpose(2, 1, 3, 0, 4).reshape(nb, n, d_s5)

        view = lambda a: a.reshape(nb, n // GRID_W, GRID_W, d_lru)
        cneg = (-LRU_C * jax.nn.softplus(-lru_lam[l].astype(F32))).reshape(2, 1, d_lru)
        cw, cb = lru_conv_w[l].astype(F32), lru_conv_b[l].reshape(1, d_lru).astype(F32)
        gates = [(_block_diag_halves(lru_w_a[l, dd]), lru_b_a[l, dd].reshape(1, d_lru),
                  _block_diag_halves(lru_w_x[l, dd]), lru_b_x[l, dd].reshape(1, d_lru), cneg[dd]) for dd in range(2)]
        h_b = _lru_pass(True, view(r), None, None, cw, cb, *gates[1], n_ctx, rows)
        lr = _lru_pass(False, view(r), view(qg), h_b, cw, cb, *gates[0], n_ctx, rows).reshape(nb, n, d_lru)

        wr = jnp.concatenate([moe_w_group[l], moe_w_router[l].transpose(1, 0, 2).reshape(d, n_groups * n_per)], -1)
        wr = jnp.pad(wr.astype(F32), ((0, 0), (0, LANES - wr.shape[-1])))
        wr_hi = wr.astype(BF16)
        wr_lo = (wr - wr_hi.astype(F32)).astype(BF16)
        wr2 = jnp.concatenate([jnp.concatenate([wr_hi, wr_lo], -1),
                               jnp.concatenate([wr_hi, jnp.zeros_like(wr_lo)], -1)], axis=0)
        br = jnp.concatenate([moe_b_group[l], moe_b_router[l].reshape(-1)])
        br = jnp.pad(br.astype(F32), (0, LANES - br.shape[0])).reshape(1, LANES)

        x1, hm, meta = _post(not need_ctx, ys, u, lr, xs, mods, l, s5_d[l], s5_w_glu[l].astype(BF16),
                             s5_b_glu[l], w_out[l].astype(BF16), norm2_g[l], wr2, br, n_groups, n_per, tm)

        n_tok = hm.shape[0] * hm.shape[1]
        meta_t = meta[:, :, :3, :].transpose(2, 0, 1, 3).reshape(3, n_tok)
        bucket = meta_t[0].astype(jnp.int32)
        y = _moe(hm.reshape(n_tok, d), bucket, meta_t[1], meta_t[2], jnp.arange(n_tok, dtype=jnp.int32),
                 moe_w1[l].astype(BF16), moe_w3[l].astype(BF16), moe_w2[l].astype(BF16), n_per, moe_rows)
        xs = [x1]
    return _final(xs[0], y, mods, depth - 1, final_g, tm)
```

```python
import functools

import jax
import jax.numpy as jnp
from jax import lax
from jax.experimental import pallas as pl
from jax.experimental.pallas import tpu as pltpu

F32 = jnp.float32
BF16 = jnp.bfloat16
HIGHEST = lax.Precision.HIGHEST

EPS = 1e-6
GRID_W = 64
LRU_C = 8.0
CONV_LEFT = 2
S5_CHUNK = 16
LRU_STEPS = 32
LANES = 128
VMEM_LIMIT = 56 * 1024 * 1024


def _cparams(*sem):
    return pltpu.CompilerParams(dimension_semantics=sem, vmem_limit_bytes=VMEM_LIMIT)


def _rms(x, g):
    return x * lax.rsqrt(jnp.mean(x * x, axis=-1, keepdims=True) + EPS) * g


def _mod_kernel(c_ref, w_ref, b_ref, o_ref):
    c = c_ref[...]
    s = c * jax.nn.sigmoid(c)
    o_ref[...] = jnp.dot(s, w_ref[...], preferred_element_type=F32, precision=HIGHEST) + b_ref[...]


def _modulation(c_rows, w_mod, b_mod):
    depth, d, n6 = w_mod.shape
    rows = c_rows.shape[0]
    tn = n6 // 4
    return pl.pallas_call(
        _mod_kernel,
        grid=(depth, n6 // tn),
        in_specs=[pl.BlockSpec((rows, d), lambda l, j: (0, 0)),
                  pl.BlockSpec((None, d, tn), lambda l, j: (l, 0, j)),
                  pl.BlockSpec((None, 1, tn), lambda l, j: (l, 0, j))],
        out_specs=pl.BlockSpec((None, rows, tn), lambda l, j: (l, 0, j)),
        out_shape=jax.ShapeDtypeStruct((depth, rows, n6), F32),
        compiler_params=_cparams("parallel", "parallel"),
        name="modulation",
    )(c_rows, w_mod, b_mod.reshape(depth, 1, n6))


def _mod_specs(layer, part, d, nb):
    return [pl.BlockSpec((None, nb, 1, d), lambda ct, t: (layer, 0, 0, part)),
            pl.BlockSpec((None, None, 1, d), lambda ct, t: (layer, nb, 0, part))]


def _mod_pick(ct, batch_ref, ctx_ref):
    return jnp.where(ct == 0, ctx_ref[...][None], batch_ref[...])


def _mod_spec_lat(layer, part, d):
    return pl.BlockSpec((None, None, 1, d), lambda b, i: (layer, b, 0, part))


def _tview(a, q):
    nb, n, w = a.shape
    return a.reshape(nb, n // q, q * w)


def _tile_of(split, ct, refs):
    if split:
        return jnp.where(ct == 0, refs[0][...], refs[1][...]), refs[2:]
    return refs[0][...], refs[1:]


def _tile_specs(xs, cc, off=0):
    nb, _, qd = xs[0].shape
    d = qd // S5_CHUNK
    if len(xs) == 2:
        return [pl.BlockSpec((nb, cc, d), lambda ct, t: (0, 0, t)),
                pl.BlockSpec((nb, cc, d), lambda ct, t: (0, jnp.maximum(ct - 1, 0), t))]
    return [pl.BlockSpec((nb, cc, d), lambda ct, t: (0, ct + off, t))]


def _inproj_kernel(has_y, split, ngrp, *refs):
    ct = pl.program_id(0)
    x, refs = _tile_of(split, ct, refs)
    if has_y:
        y_ref, gab_ref, gac_ref = refs[:3]
        refs = refs[3:]
        x = x + _mod_pick(ct, gab_ref, gac_ref) * y_ref[...]
    shb_ref, shc_ref, scb_ref, scc_ref, g_ref, wu_ref, wrq_ref = refs[:7]
    outs = refs[7:]
    if has_y:
        outs[0][...] = x
        outs = outs[1:]
    ut_ref, r_ref, q_ref = outs
    nb, cc, d = x.shape
    d_lru = r_ref.shape[-1]
    h = _rms(x, g_ref[...]) * (1.0 + _mod_pick(ct, scb_ref, scc_ref)) + _mod_pick(ct, shb_ref, shc_ref)
    hb = h.reshape(nb * cc, d).astype(BF16)
    ut = lax.dot_general(wu_ref[...], hb, (((1,), (1,)), ((), ())), preferred_element_type=F32)
    ut_ref[...] = ut.reshape(ngrp, ut.shape[0] // ngrp, nb * cc)
    p = jnp.dot(hb, wrq_ref[...], preferred_element_type=F32)
    r_ref[...] = p[:, :d_lru].reshape(nb, cc, d_lru)
    q_ref[...] = p[:, d_lru:].reshape(nb, cc, d_lru)


def _inproj(xs, y, mods, layer, g1, w_in, ngrp, d_s5, d_lru, cc):
    q = S5_CHUNK
    nb, d = xs[0].shape[0], xs[0].shape[-1] // q
    nchunk = sum(a.shape[1] for a in xs)
    has_y = y is not None
    nh = d_s5 // ngrp
    in_specs = _tile_specs(xs, cc)
    args = list(xs)
    if has_y:
        in_specs += [pl.BlockSpec((nb, cc, d), lambda ct, t: (0, ct, t))] + _mod_specs(layer - 1, 5, d, nb)
        args += [y, mods, mods]
    wu = w_in[:, :d_s5].T.astype(BF16)
    wrq = w_in[:, d_s5:].astype(BF16)
    full = lambda a: pl.BlockSpec(a.shape, lambda ct, t: (0,) * a.ndim)
    in_specs += _mod_specs(layer, 0, d, nb) + _mod_specs(layer, 1, d, nb) + [
        pl.BlockSpec((1, d), lambda ct, t: (0, 0)), full(wu), full(wrq)]
    args += [mods, mods, mods, mods, g1.reshape(1, d), wu, wrq]
    tok = lambda w: pl.BlockSpec((nb, cc, w), lambda ct, t: (0, ct, t))
    out_specs = [pl.BlockSpec((ngrp, nh, nb * cc), lambda ct, t: (0, t, ct)), tok(d_lru), tok(d_lru)]
    out_shape = [jax.ShapeDtypeStruct((ngrp, q * nh, nb * nchunk), F32),
                 jax.ShapeDtypeStruct((nb, nchunk, q * d_lru), F32),
                 jax.ShapeDtypeStruct((nb, nchunk, q * d_lru), F32)]
    if has_y:
        out_specs = [tok(d)] + out_specs
        out_shape = [jax.ShapeDtypeStruct((nb, nchunk, q * d), F32)] + out_shape
    res = pl.pallas_call(
        functools.partial(_inproj_kernel, has_y, len(xs) == 2, ngrp),
        grid=(nchunk // cc, q), in_specs=in_specs, out_specs=out_specs, out_shape=out_shape,
        compiler_params=_cparams("parallel", "parallel"),
        name="inproj",
    )(*args)
    if has_y:
        return [[res[0]]] + list(res[1:])
    return [xs] + list(res)


def _s5_weights(a_re, a_im, log_dt, b_re, b_im, c_re, c_im):
    q = S5_CHUNK
    lam = lax.complex(a_re.astype(F32), a_im.astype(F32))
    dt = jnp.exp(log_dt.astype(F32))[..., None]
    a_bar = jnp.exp(lam * dt)
    bmat = lax.complex(b_re.astype(F32), b_im.astype(F32))
    b_bar = ((a_bar - 1) / lam)[..., None] * bmat
    cmat = lax.complex(c_re.astype(F32), c_im.astype(F32))
    k = jnp.arange(q + 1, dtype=F32)
    apow = jnp.exp((lam * dt)[..., None] * k)
    ngrp, nst = a_re.shape[1], a_re.shape[2]
    nh = b_re.shape[-1]

    kern = jnp.einsum('dgop,dgpk,dgpi->dgkoi', cmat, apow[..., :q], b_bar, precision=HIGHEST).real
    sig = jnp.arange(q)[:, None]
    tau = jnp.arange(q)[None, :]
    lag_f = jnp.clip(tau - sig, 0, q - 1)
    lag_b = jnp.clip(sig - tau, 0, q - 1)
    tf = jnp.where((sig <= tau)[None, :, :, None, None], kern[0][:, lag_f], 0.0)
    tb = jnp.where((sig >= tau)[None, :, :, None, None], kern[1][:, lag_b], 0.0)
    toep = (tf + tb).transpose(0, 1, 4, 2, 3).reshape(ngrp, q * nh, q * nh)

    win_f = jnp.einsum('gps,gpi->gsip', apow[0][..., :q][..., ::-1], b_bar[0])
    win_b = jnp.einsum('gps,gpi->gsip', apow[1][..., :q], b_bar[1])
    win = jnp.concatenate([win_f.real, win_b.real, win_f.imag, win_b.imag], axis=-1)
    win = win.reshape(ngrp, q * nh, 4 * nst)

    cf = jnp.einsum('gop,gpt->gpto', cmat[0], apow[0][..., 1:])
    cb = jnp.einsum('gop,gpt->gpto', cmat[1], apow[1][..., 1:][..., ::-1])
    z = jnp.zeros_like(cf.real)
    wof = jnp.concatenate([cf.real, z, -cf.imag, z], axis=1).reshape(ngrp, 4 * nst, q * nh)
    wob = jnp.concatenate([z, cb.real, z, -cb.imag], axis=1).reshape(ngrp, 4 * nst, q * nh)

    aq = apow[..., q]
    a16 = jnp.stack([jnp.concatenate([aq[0].real, aq[1].real], -1),
                     jnp.concatenate([aq[0].imag, aq[1].imag], -1)], axis=1)
    tr = lambda a: a.transpose(0, 2, 1).astype(BF16)
    return tr(toep), tr(win), tr(wof), tr(wob), a16.astype(F32)


def _s5_kernel(nb, cc, nc_ctx, nc_lat, u_ref, toep_ref, win_ref, wof_ref, wob_ref, a_ref, y_ref,
               sre_ref, sim_ref, fre_ref, fim_ref, bre_ref, bim_ref):
    qh, m = u_ref.shape
    ns2 = a_ref.shape[-1]
    mt = nb * cc
    toep = toep_ref[...]
    win = win_ref[...]

    def intra(i, c):
        r0 = pl.multiple_of(i * mt, mt)
        ub = u_ref[:, pl.ds(r0, mt)].astype(BF16)
        y_ref[:, pl.ds(r0, mt)] = jnp.dot(toep, ub, preferred_element_type=F32)
        s = jnp.dot(win, ub, preferred_element_type=F32).T
        sre_ref[pl.ds(r0, mt), :] = s[:, :ns2]
        sim_ref[pl.ds(r0, mt), :] = s[:, ns2:]
        return c

    lax.fori_loop(0, m // mt, intra, 0)

    are = jnp.broadcast_to(a_ref[0:1, :], (nb, ns2))
    aim = jnp.broadcast_to(a_ref[1:2, :], (nb, ns2))
    fwd_lane = lax.broadcasted_iota(jnp.int32, (nb, ns2), 1) < ns2 // 2

    def rows(c):
        tile = lax.div(c, cc)
        return pl.ds(tile * mt + (c - tile * cc), nb, stride=cc)

    def scan(base, n, carry):
        def step(k, hc):
            hre, him = hc
            rf = rows(base + k)
            rb = rows(base + n - 1 - k)
            fre_ref[rf, :] = hre
            fim_ref[rf, :] = him
            bre_ref[rb, :] = hre
            bim_ref[rb, :] = him
            sre = jnp.where(fwd_lane, sre_ref[rf, :], sre_ref[rb, :])
            sim = jnp.where(fwd_lane, sim_ref[rf, :], sim_ref[rb, :])
            return (are * hre - aim * him + sre, are * him + aim * hre + sim)
        return lax.fori_loop(0, n, step, carry)

    zero = jnp.zeros((nb, ns2), F32)
    carry = scan(0, nc_ctx, (zero, zero))
    scan(nc_ctx, nc_lat, carry)

    wof = wof_ref[...]
    wob = wob_ref[...]
    nt_dims = (((1,), (1,)), ((), ()))

    def inter(i, c):
        r0 = pl.multiple_of(i * mt, mt)
        hf = jnp.concatenate([fre_ref[pl.ds(r0, mt), :], fim_ref[pl.ds(r0, mt), :]], axis=1).astype(BF16)
        hb = jnp.concatenate([bre_ref[pl.ds(r0, mt), :], bim_ref[pl.ds(r0, mt), :]], axis=1).astype(BF16)
        y_ref[:, pl.ds(r0, mt)] += (lax.dot_general(wof, hf, nt_dims, preferred_element_type=F32)
                                    + lax.dot_general(wob, hb, nt_dims, preferred_element_type=F32))
        return c

    lax.fori_loop(0, m // mt, inter, 0)


def _s5_scan(u_t, toep, win, wof, wob, a16, nb, cc, nc_ctx, nc_lat):
    ngrp, qh, m = u_t.shape
    ns2 = a16.shape[-1]
    grp = lambda a: pl.BlockSpec((None,) + a.shape[1:], lambda g: (g, 0, 0))
    return pl.pallas_call(
        functools.partial(_s5_kernel, nb, cc, nc_ctx, nc_lat),
        grid=(ngrp,),
        in_specs=[grp(u_t), grp(toep), grp(win), grp(wof), grp(wob), grp(a16)],
        out_specs=grp(u_t),
        out_shape=jax.ShapeDtypeStruct(u_t.shape, F32),
        scratch_shapes=[pltpu.VMEM((m, ns2), F32)] * 6,
        compiler_params=_cparams("parallel"),
        name="s5_chunked",
    )(u_t, toep, win, wof, wob, a16)


def _lru_chunk(s, reverse, nc_ctx, nc):
    if not reverse:
        return s
    return jnp.where(s < nc_ctx, nc_ctx - 1 - s, nc - 1 - (s - nc_ctx))


def _lru_kernel(reverse, combine, nc_ctx, nc, rows, *refs):
    if combine:
        (r_hbm, q_hbm, ho_hbm, cw_ref, cb_ref, wa_ref, ba_ref, wx_ref, bx_ref, cn_ref, o_hbm,
         rbuf, obuf, h_ref, a_s, b_s, isem, osem, qbuf, hbuf) = refs
    else:
        (r_hbm, cw_ref, cb_ref, wa_ref, ba_ref, wx_ref, bx_ref, cn_ref, o_hbm,
         rbuf, obuf, h_ref, a_s, b_s, isem, osem) = refs
    _, jt, nb, ch = obuf.shape
    gw = r_hbm.shape[2]
    cblk = (nc_ctx * jt) // gw
    ncol = jt // rows
    s = pl.program_id(0)
    slot = lax.rem(s, 2)

    def chunk_of(step):
        return _lru_chunk(jnp.clip(step, 0, nc - 1), reverse, nc_ctx, nc)

    def main_copies(hbm, buf, row0, c, sl, sem, fn):
        @pl.when(c < nc_ctx)
        def _():
            j0 = c * jt
            for b in range(nb):
                fn(hbm.at[b, j0 // gw, pl.ds(j0 % gw, jt), :], buf.at[sl, pl.ds(row0, jt), b, :], sem.at[sl])

        @pl.when(c >= nc_ctx)
        def _():
            w0 = (c - nc_ctx) * ncol
            for b in range(nb):
                for k in range(ncol):
                    fn(hbm.at[b, pl.ds(cblk, rows), w0 + k, :],
                       buf.at[sl, pl.ds(row0 + k * rows, rows), b, :], sem.at[sl])

    def halo_copies(c, sl, fn):
        first = jnp.logical_or(c == 0, c == nc_ctx)
        last = jnp.logical_or(c == nc_ctx - 1, c == nc - 1)
        in_ctx = c < nc_ctx

        @pl.when(jnp.logical_and(in_ctx, jnp.logical_not(first)))
        def _():
            j = c * jt - CONV_LEFT
            for b in range(nb):
                fn(r_hbm.at[b, j // gw, pl.ds(j % gw, CONV_LEFT), :], rbuf.at[sl, pl.ds(0, CONV_LEFT), b, :],
                   isem.at[sl])

        @pl.when(jnp.logical_and(in_ctx, jnp.logical_not(last)))
        def _():
            j = c * jt + jt
            for b in range(nb):
                fn(r_hbm.at[b, j // gw, pl.ds(j % gw, 1), :], rbuf.at[sl, pl.ds(jt + CONV_LEFT, 1), b, :],
                   isem.at[sl])

        @pl.when(jnp.logical_and(jnp.logical_not(in_ctx), jnp.logical_not(first)))
        def _():
            w = (c - nc_ctx) * ncol - 1
            for b in range(nb):
                fn(r_hbm.at[b, pl.ds(cblk + rows - CONV_LEFT, CONV_LEFT), w, :],
                   rbuf.at[sl, pl.ds(0, CONV_LEFT), b, :], isem.at[sl])

        @pl.when(jnp.logical_and(jnp.logical_not(in_ctx), jnp.logical_not(last)))
        def _():
            w = (c - nc_ctx) * ncol + ncol
            for b in range(nb):
                fn(r_hbm.at[b, pl.ds(cblk, 1), w, :], rbuf.at[sl, pl.ds(jt + CONV_LEFT, 1), b, :], isem.at[sl])

    def loads(c, sl, fn):
        main_copies(r_hbm, rbuf, CONV_LEFT, c, sl, isem, fn)
        halo_copies(c, sl, fn)
        if combine:
            main_copies(q_hbm, qbuf, 0, c, sl, isem, fn)
            main_copies(ho_hbm, hbuf, 0, c, sl, isem, fn)

    def stores(c, sl, fn):
        main_copies(o_hbm, obuf, 0, c, sl, osem, lambda hbm, buf, sem: fn(buf, hbm, sem))

    start = lambda src, dst, sem: pltpu.make_async_copy(src, dst, sem).start()
    wait = lambda src, dst, sem: pltpu.make_async_copy(src, dst, sem).wait()

    c = chunk_of(s)
    first = jnp.logical_or(c == 0, c == nc_ctx)
    last = jnp.logical_or(c == nc_ctx - 1, c == nc - 1)

    @pl.when(s == 0)
    def _():
        h_ref[...] = jnp.zeros_like(h_ref)
        rbuf[...] = jnp.zeros_like(rbuf)
        loads(c, 0, start)

    @pl.when(s + 1 < nc)
    def _():
        loads(chunk_of(s + 1), 1 - slot, start)

    loads(c, slot, wait)

    @pl.when(s >= 2)
    def _():
        stores(chunk_of(s - 2), slot, wait)

    rb = rbuf[slot]
    prev = jnp.where(first, 0.0, rb[0:CONV_LEFT])
    nxt = jnp.where(last, 0.0, rb[jt + CONV_LEFT:jt + CONV_LEFT + 1])
    xe = jnp.concatenate([prev, rb[CONV_LEFT:jt + CONV_LEFT], nxt], axis=0)
    cw = cw_ref[...]
    xc = cb_ref[...].reshape(1, 1, ch)
    for k in range(cw.shape[0]):
        xc = xc + cw[k:k + 1, :].reshape(1, 1, ch) * xe[k:k + jt]
    xc2 = xc.reshape(jt * nb, ch)

    xb = xc2.astype(BF16)
    nblk, wb = wa_ref.shape[0], wa_ref.shape[1]
    za = jnp.concatenate([jnp.dot(xb[:, i * wb:(i + 1) * wb], wa_ref[i], preferred_element_type=F32)
                          for i in range(nblk)], axis=-1)
    zx = jnp.concatenate([jnp.dot(xb[:, i * wb:(i + 1) * wb], wx_ref[i], preferred_element_type=F32)
                          for i in range(nblk)], axis=-1)
    rg = jax.nn.sigmoid(za + ba_ref[...])
    ig = jax.nn.sigmoid(zx + bx_ref[...])
    log_a = cn_ref[...] * rg
    a = jnp.exp(log_a)
    bb = jnp.sqrt(1.0 - a * a) * (ig * xc2)
    a_s[...] = a.reshape(jt, nb, ch)
    b_s[...] = bb.reshape(jt, nb, ch)

    def step(t, h):
        tt = jt - 1 - t if reverse else t
        h = a_s[tt] * h + b_s[tt]
        if combine:
            obuf[slot, tt] = (h + hbuf[slot, tt]) * jax.nn.gelu(qbuf[slot, tt])
        else:
            obuf[slot, tt] = h
        return h

    h_ref[...] = lax.fori_loop(0, jt, step, h_ref[...])

    stores(c, slot, start)

    @pl.when(s == nc - 1)
    def _():
        if nc >= 2:
            stores(chunk_of(s - 1), 1 - slot, wait)
        stores(c, slot, wait)


def _lru_pass(reverse, r4, q4, h_other4, conv_w, conv_b, wa, ba, wx, bx, cneg, n_ctx, rows):
    nb, nblk, gw, ch = r4.shape
    jt = LRU_STEPS
    nc, nc_ctx = (nblk * gw) // jt, n_ctx // jt
    combine = h_other4 is not None
    assert gw % jt == 0 and n_ctx % gw == 0 and jt % rows == 0 and rows >= CONV_LEFT
    anyspec = pl.BlockSpec(memory_space=pl.ANY)
    full = lambda a: pl.BlockSpec(a.shape, lambda s: (0,) * a.ndim)
    consts = [conv_w, conv_b, wa, ba, wx, bx, cneg]
    big = [r4, q4, h_other4] if combine else [r4]
    buf = lambda n: pltpu.VMEM((2, n, nb, ch), F32)
    scratch = [buf(jt + CONV_LEFT + 1), buf(jt), pltpu.VMEM((nb, ch), F32),
               pltpu.VMEM((jt, nb, ch), F32), pltpu.VMEM((jt, nb, ch), F32),
               pltpu.SemaphoreType.DMA((2,)), pltpu.SemaphoreType.DMA((2,))]
    if combine:
        scratch += [buf(jt), buf(jt)]
    return pl.pallas_call(
        functools.partial(_lru_kernel, reverse, combine, nc_ctx, nc, rows),
        grid=(nc,), in_specs=[anyspec] * len(big) + [full(a) for a in consts], out_specs=anyspec,
        out_shape=jax.ShapeDtypeStruct(r4.shape, F32),
        scratch_shapes=scratch,
        compiler_params=_cparams("arbitrary"),
        name="lru_bwd" if reverse else "lru_fwd",
    )(*big, *consts)


def _block_diag_halves(w):
    nh, hd, _ = w.shape
    half = nh // 2
    eye = jnp.eye(half, dtype=w.dtype)
    wh = w.reshape(2, half, hd, hd)
    bd = jnp.einsum('bhij,hk->bhikj', wh, eye).reshape(2, half * hd, half * hd)
    return bd.astype(BF16)


def _post_kernel(n_groups, n_per, split, off, *refs):
    ct = pl.program_id(0) + off
    x_in, refs = _tile_of(split, ct, refs)
    (yt_ref, ut_ref, lr_ref, gab_ref, gac_ref, shb_ref, shc_ref, scb_ref, scc_ref, d_ref, wg_ref, bg_ref,
     wo_ref, g2_ref, wr_ref, br_ref, x1_ref, hm_ref, meta_ref) = refs
    nb, cc, d = x_in.shape
    tm = nb * cc
    d_s5 = d_ref.shape[0]
    ys = yt_ref[...].reshape(d_s5, tm) + d_ref[...] * ut_ref[...].reshape(d_s5, tm)
    z = jax.nn.gelu(ys)
    s5t = z * jax.nn.sigmoid(jnp.dot(wg_ref[...], z.astype(BF16), preferred_element_type=F32) + bg_ref[...])
    mix = jnp.concatenate([s5t.T.astype(BF16), lr_ref[...].reshape(tm, -1).astype(BF16)], axis=-1)
    proj = jnp.dot(mix, wo_ref[...], preferred_element_type=F32).reshape(nb, cc, d)
    x1 = x_in + _mod_pick(ct, gab_ref, gac_ref) * proj
    x1_ref[...] = x1
    hm3 = _rms(x1, g2_ref[...]) * (1.0 + _mod_pick(ct, scb_ref, scc_ref)) + _mod_pick(ct, shb_ref, shc_ref)
    hm_ref[...] = hm3
    hm = hm3.reshape(tm, d)

    hi = hm.astype(BF16)
    lo = (hm - hi.astype(F32)).astype(BF16)
    two = jnp.dot(jnp.concatenate([hi, lo], axis=-1), wr_ref[...], preferred_element_type=F32)
    logits = two[:, :LANES] + two[:, LANES:] + br_ref[...]

    lane = lax.broadcasted_iota(jnp.int32, (tm, LANES), 1).astype(F32)
    neg = jnp.float32(-jnp.inf)
    big = jnp.float32(LANES)
    lg = jnp.where(lane < n_groups, logits, neg)
    mg = jnp.max(lg, axis=-1, keepdims=True)
    g_p = 1.0 / jnp.sum(jnp.exp(lg - mg), axis=-1, keepdims=True)
    gidx = jnp.min(jnp.where(lg == mg, lane, big), axis=-1, keepdims=True)
    e0 = n_groups + n_per * gidx
    le = jnp.where(jnp.logical_and(lane >= e0, lane < e0 + n_per), logits, neg)
    m1 = jnp.max(le, axis=-1, keepdims=True)
    i1 = jnp.min(jnp.where(le == m1, lane, big), axis=-1, keepdims=True)
    le2 = jnp.where(lane == i1, neg, le)
    m2 = jnp.max(le2, axis=-1, keepdims=True)
    i2 = jnp.min(jnp.where(le2 == m2, lane, big), axis=-1, keepdims=True)
    r2 = jnp.exp(m2 - m1)
    w1 = g_p / (1.0 + r2)
    w2 = g_p * r2 / (1.0 + r2)
    j1 = i1 - e0
    j2 = i2 - e0
    jlo = jnp.minimum(j1, j2)
    jhi = jnp.maximum(j1, j2)
    wlo = jnp.where(j1 < j2, w1, w2)
    whi = jnp.where(j1 < j2, w2, w1)
    pair = jlo * (2 * n_per - 1 - jlo) * 0.5 + (jhi - jlo - 1.0)
    bucket = gidx * (n_per * (n_per - 1) // 2) + pair
    meta = jnp.where(lane == 0, bucket, jnp.where(lane == 1, wlo, jnp.where(lane == 2, whi, 0.0)))
    meta_ref[...] = meta.T[0:8, :]


def _post(lat_only, yt, ut, lr, xs, mods, layer, s5_d, w_glu, b_glu, w_out_bf, g2, wr2, br, n_groups, n_per, cc):
    q = S5_CHUNK
    nb, d = xs[0].shape[0], xs[0].shape[-1] // q
    nchunk = sum(a.shape[1] for a in xs)
    ngrp, qh, _ = yt.shape
    nh = qh // q
    d_s5 = ngrp * nh
    d_lru = lr.shape[-1] // q
    off = 1 if lat_only else 0
    assert not (lat_only and len(xs) == 2)
    nct = nchunk // cc - off
    tm = nb * cc
    tspec = pl.BlockSpec((ngrp, nh, tm), lambda ct, t: (0, t, ct + off))
    full = lambda a: pl.BlockSpec(a.shape, lambda ct, t: (0,) * a.ndim)
    consts = [s5_d.reshape(d_s5, 1), w_glu.T.astype(BF16), b_glu.reshape(d_s5, 1), w_out_bf, g2.reshape(1, d),
              wr2, br]
    otok = pl.BlockSpec((nb, cc, d), lambda ct, t: (0, ct, t))
    return pl.pallas_call(
        functools.partial(_post_kernel, n_groups, n_per, len(xs) == 2, off),
        grid=(nct, q),
        in_specs=(_tile_specs(xs, cc, off)
                  + [tspec, tspec, pl.BlockSpec((nb, cc, d_lru), lambda ct, t: (0, ct + off, t))]
                  + _mod_specs(layer, 2, d, nb) + _mod_specs(layer, 3, d, nb) + _mod_specs(layer, 4, d, nb)
                  + [full(a) for a in consts]),
        out_specs=[otok, otok, pl.BlockSpec((None, None, 8, tm), lambda ct, t: (t, ct, 0, 0))],
        out_shape=[jax.ShapeDtypeStruct((nb, nct * cc, q * d), F32), jax.ShapeDtypeStruct((nb, nct * cc, q * d), F32),
                   jax.ShapeDtypeStruct((q, nct, 8, tm), F32)],
        compiler_params=_cparams("parallel", "parallel"),
        name="post_mixer",
    )(*xs, yt, ut, lr, mods, mods, mods, mods, mods, mods, *consts)


def _moe_kernel(n_rows, nt, tb_ref, src0_ref, srcn_ref, dst_ref, hm_hbm, wt_ref, w1a, w3a, w2a, w1b, w3b, w2b,
                y_hbm, xbuf, ybuf, gsem, ssem):
    i = pl.program_id(0)
    slot = lax.rem(i, 2)
    tmr = xbuf.shape[1]

    def valid(j):
        return tb_ref[2, jnp.clip(j, 0, nt - 1)] > 0

    def start_gathers(ids_ref, s):
        for r in range(tmr):
            pltpu.make_async_copy(hm_hbm.at[pl.ds(ids_ref[0, r], 1), :], xbuf.at[s, pl.ds(r, 1), :],
                                  gsem.at[s]).start(priority=r % 2)

    def start_scatters(s):
        for r in range(tmr):
            pltpu.make_async_copy(ybuf.at[s, pl.ds(r, 1), :], y_hbm.at[pl.ds(dst_ref[0, r], 1), :],
                                  ssem.at[s]).start(priority=r % 2)

    def wait_gathers(s):
        pltpu.make_async_copy(hm_hbm.at[pl.ds(0, tmr), :], xbuf.at[s], gsem.at[s]).wait()

    def wait_scatters(s):
        pltpu.make_async_copy(ybuf.at[s], y_hbm.at[pl.ds(0, tmr), :], ssem.at[s]).wait()

    @pl.when(i == 0)
    def _():
        xbuf[...] = jnp.zeros_like(xbuf)
        ybuf[...] = jnp.zeros_like(ybuf)
        for s in range((y_hbm.shape[0] - n_rows) // tmr):
            cp = pltpu.make_async_copy(ybuf.at[0], y_hbm.at[pl.ds(n_rows + s * tmr, tmr), :], ssem.at[0])
            cp.start()
            cp.wait()
        start_gathers(src0_ref, 0)

    @pl.when(jnp.logical_and(i + 1 < nt, valid(i + 1)))
    def _():
        start_gathers(srcn_ref, 1 - slot)

    @pl.when(jnp.logical_and(i >= 2, valid(i - 2)))
    def _():
        wait_scatters(slot)

    @pl.when(valid(i))
    def _():
        wait_gathers(slot)
        xb = xbuf[slot].astype(BF16)
        wt = wt_ref[...]

        def expert(w1, w3, w2, gate):
            h1 = jnp.dot(xb, w1[...], preferred_element_type=F32)
            h3 = jnp.dot(xb, w3[...], preferred_element_type=F32)
            hid = (h1 * jax.nn.sigmoid(h1)) * h3
            return gate * jnp.dot(hid.astype(BF16), w2[...], preferred_element_type=F32)

        half = LANES // 2
        ybuf[slot] = expert(w1a, w3a, w2a, wt[:, 0:1]) + expert(w1b, w3b, w2b, wt[:, half:half + 1])
        start_scatters(slot)

    @pl.when(i == nt - 1)
    def _():
        @pl.when(jnp.logical_and(nt >= 2, valid(i - 1)))
        def _():
            wait_scatters(1 - slot)

        @pl.when(valid(i))
        def _():
            wait_scatters(slot)


def _moe(hm2, bucket, wlo, whi, tok_rows, w1_bf, w3_bf, w2_bf, n_per, tmr, spare_rows):
    n_rows, d = hm2.shape
    assert spare_rows >= 2 * tmr and spare_rows % tmr == 0
    t = bucket.shape[0]
    npairs = n_per * (n_per - 1) // 2
    nbuck = (w1_bf.shape[0] // n_per) * npairs
    ntiles = t // tmr + nbuck

    _, s_tok, s_lo, s_hi = lax.sort((bucket, tok_rows, wlo, whi), num_keys=1, is_stable=True)
    counts = jnp.sum((bucket[None, :] == jnp.arange(nbuck, dtype=jnp.int32)[:, None]).astype(jnp.int32), axis=1)
    padded = ((counts + tmr - 1) // tmr) * tmr
    pend = jnp.cumsum(padded)
    cend = jnp.cumsum(counts)
    shift = (pend - padded) - (cend - counts)
    tile_start = jnp.arange(ntiles, dtype=jnp.int32) * tmr
    valid = (tile_start < pend[-1]).astype(jnp.int32)
    tb = jnp.sum((tile_start[:, None] >= pend[None, :]).astype(jnp.int32), axis=1)
    tb = jnp.minimum(tb, jnp.sum((pend < pend[-1]).astype(jnp.int32)))
    tb = jnp.minimum(tb, nbuck - 1)
    pos = (tile_start[:, None] + jnp.arange(tmr, dtype=jnp.int32)[None, :]) - shift[tb][:, None]
    real = jnp.logical_and(pos < cend[tb][:, None], valid[:, None] > 0)
    pos = jnp.clip(pos, 0, t - 1)
    src = jnp.where(real, s_tok[pos], 0)
    spare = n_rows + (jnp.arange(ntiles, dtype=jnp.int32) % 2)[:, None] * tmr + jnp.arange(tmr, dtype=jnp.int32)
    dst = jnp.where(real, s_tok[pos], spare)
    g_lo = jnp.where(real, s_lo[pos], 0.0).reshape(-1, 1)
    g_hi = jnp.where(real, s_hi[pos], 0.0).reshape(-1, 1)
    half = LANES // 2
    wts = jnp.concatenate([jnp.broadcast_to(g_lo, (ntiles * tmr, half)),
                           jnp.broadcast_to(g_hi, (ntiles * tmr, half))], axis=1)
    grp, pr = tb // npairs, tb % npairs
    pairs = [(i, j) for i in range(n_per) for j in range(i + 1, n_per)]
    plo = jnp.array([p[0] for p in pairs], jnp.int32)[pr]
    phi = jnp.array([p[1] for p in pairs], jnp.int32)[pr]
    tinfo = jnp.stack([grp * n_per + plo, grp * n_per + phi, valid])

    wspec = lambda a, row: pl.BlockSpec((None,) + a.shape[1:], lambda i, tb_: (tb_[row, i], 0, 0))
    ids = lambda f: pl.BlockSpec((None, 1, tmr), lambda i, tb_: (f(i), 0, 0), memory_space=pltpu.SMEM)
    grid_spec = pltpu.PrefetchScalarGridSpec(
        num_scalar_prefetch=1,
        grid=(ntiles,),
        in_specs=[ids(lambda i: 0), ids(lambda i: jnp.minimum(i + 1, ntiles - 1)), ids(lambda i: i),
                  pl.BlockSpec(memory_space=pl.ANY),
                  pl.BlockSpec((tmr, LANES), lambda i, tb_: (i, 0)),
                  wspec(w1_bf, 0), wspec(w3_bf, 0), wspec(w2_bf, 0),
                  wspec(w1_bf, 1), wspec(w3_bf, 1), wspec(w2_bf, 1)],
        out_specs=pl.BlockSpec(memory_space=pl.ANY),
        scratch_shapes=[pltpu.VMEM((2, tmr, d), F32), pltpu.VMEM((2, tmr, d), F32),
                        pltpu.SemaphoreType.DMA((2,)), pltpu.SemaphoreType.DMA((2,))],
    )
    src3 = src.reshape(ntiles, 1, tmr)
    return pl.pallas_call(
        functools.partial(_moe_kernel, n_rows, ntiles),
        grid_spec=grid_spec,
        out_shape=jax.ShapeDtypeStruct((n_rows + spare_rows, d), F32),
        compiler_params=_cparams("arbitrary"),
        name="moe_pairs",
    )(tinfo, src3, src3, dst.reshape(ntiles, 1, tmr), hm2, wts, w1_bf, w3_bf, w2_bf, w1_bf, w3_bf, w2_bf)


def _final_kernel(x_ref, y_ref, ga_ref, g_ref, o_ref):
    o_ref[...] = _rms(x_ref[...] + ga_ref[...] * y_ref[...], g_ref[...])


def _final(x1, y, mods, layer, final_g, tm):
    nb, seq, d = x1.shape
    tok = pl.BlockSpec((None, tm, d), lambda b, i: (b, i, 0))
    return pl.pallas_call(
        _final_kernel,
        grid=(nb, seq // tm),
        in_specs=[tok, pl.BlockSpec((tm, d), lambda b, i: (b * (seq // tm) + i, 0)),
                  _mod_spec_lat(layer, 5, d), pl.BlockSpec((1, d), lambda b, i: (0, 0))],
        out_specs=tok,
        out_shape=jax.ShapeDtypeStruct((nb, seq, d), F32),
        compiler_params=_cparams("parallel", "parallel"),
        name="final_norm",
    )(x1, y, mods, final_g.reshape(1, d))


def kernel(x, c, ctx, c_ctx, w_mod, b_mod, norm1_g, norm2_g, w_in, w_out, s5_a_re, s5_a_im, s5_log_dt, s5_b_re, s5_b_im, s5_c_re, s5_c_im, s5_d, s5_w_glu, s5_b_glu, lru_conv_w, lru_conv_b, lru_w_a, lru_b_a, lru_w_x, lru_b_x, lru_lam, moe_w_group, moe_b_group, moe_w_router, moe_b_router, moe_w1, moe_w3, moe_w2, final_g):
    nb, seq, d = x.shape
    n_ctx = ctx.shape[1]
    depth = w_mod.shape[0]
    n = n_ctx + seq
    d_s5 = s5_d.shape[-1]
    d_lru = lru_conv_b.shape[-1]
    ngrp, nh = s5_b_re.shape[2], s5_b_re.shape[4]
    n_groups, n_per = moe_w_router.shape[1], moe_w_router.shape[3]
    rows = seq // GRID_W
    tm = n_ctx
    q = S5_CHUNK
    nc_ctx, nc_lat = n_ctx // q, seq // q
    moe_rows = 256 if nb * seq >= 16384 else 32
    assert seq % GRID_W == 0 and seq % tm == 0 and n_ctx % GRID_W == 0 and seq % LRU_STEPS == 0
    assert d_s5 == ngrp * nh and q * nh == 2 * LANES and n_groups + n_groups * n_per <= LANES

    pad = (-(nb + 1)) % 8
    c_rows = jnp.concatenate([c, c_ctx[None, :], jnp.zeros((pad, d), F32)], axis=0)
    mods = _modulation(c_rows, w_mod, b_mod).reshape(depth, nb + 1 + pad, 1, 6 * d)

    cc = nc_ctx
    xs = [_tview(ctx, q), _tview(x, q)]
    y = None
    for l in range(depth):
        need_ctx = l < depth - 1
        xs, ut, r, qg = _inproj(xs, y, mods, l, norm1_g[l], w_in[l], ngrp, d_s5, d_lru, cc)

        toep, win, wof, wob, a16 = _s5_weights(s5_a_re[l], s5_a_im[l], s5_log_dt[l], s5_b_re[l], s5_b_im[l],
                                               s5_c_re[l], s5_c_im[l])
        yt = _s5_scan(ut, toep, win, wof, wob, a16, nb, cc, nc_ctx, nc_lat)

        view = lambda a: a.reshape(nb, n // GRID_W, GRID_W, d_lru)
        cneg = (-LRU_C * jax.nn.softplus(-lru_lam[l].astype(F32))).reshape(2, 1, d_lru)
        cw, cb = lru_conv_w[l].astype(F32), lru_conv_b[l].reshape(1, d_lru).astype(F32)
        gates = [(_block_diag_halves(lru_w_a[l, dd]), lru_b_a[l, dd].reshape(1, d_lru),
                  _block_diag_halves(lru_w_x[l, dd]), lru_b_x[l, dd].reshape(1, d_lru), cneg[dd]) for dd in range(2)]
        h_b = _lru_pass(True, view(r), None, None, cw, cb, *gates[1], n_ctx, rows)
        lr = _lru_pass(False, view(r), view(qg), h_b, cw, cb, *gates[0], n_ctx, rows).reshape(nb, n // q, q * d_lru)

        wr = jnp.concatenate([moe_w_group[l], moe_w_router[l].transpose(1, 0, 2).reshape(d, n_groups * n_per)], -1)
        wr = jnp.pad(wr.astype(F32), ((0, 0), (0, LANES - wr.shape[-1])))
        wr_hi = wr.astype(BF16)
        wr_lo = (wr - wr_hi.astype(F32)).astype(BF16)
        wr2 = jnp.concatenate([jnp.concatenate([wr_hi, wr_lo], -1),
                               jnp.concatenate([wr_hi, jnp.zeros_like(wr_lo)], -1)], axis=0)
        br = jnp.concatenate([moe_b_group[l], moe_b_router[l].reshape(-1)])
        br = jnp.pad(br.astype(F32), (0, LANES - br.shape[0])).reshape(1, LANES)

        x1, hm, meta = _post(not need_ctx, yt, ut, lr, xs, mods, l, s5_d[l], s5_w_glu[l], s5_b_glu[l],
                             w_out[l].astype(BF16), norm2_g[l], wr2, br, n_groups, n_per, cc)

        n_out = x1.shape[1] * q
        nct = x1.shape[1] // cc
        meta_t = meta.reshape(q, nct, 8, nb, cc).transpose(2, 3, 1, 4, 0).reshape(8, nb * n_out)
        bucket = meta_t[0].astype(jnp.int32)
        y = _moe(hm.reshape(nb * n_out, d), bucket, meta_t[1], meta_t[2], jnp.arange(nb * n_out, dtype=jnp.int32),
                 moe_w1[l].astype(BF16), moe_w3[l].astype(BF16), moe_w2[l].astype(BF16), n_per, moe_rows, n_out)
        xs = [x1]
        if need_ctx:
            y = y.reshape(nb + 1, n_out // q, q * d)
    return _final(xs[0].reshape(nb, -1, d), y, mods, depth - 1, final_g, tm)
```

```python
import functools

import jax
import jax.numpy as jnp
from jax import lax
from jax.experimental import pallas as pl
from jax.experimental.pallas import tpu as pltpu

F32 = jnp.float32
BF16 = jnp.bfloat16
HIGHEST = lax.Precision.HIGHEST

EPS = 1e-6
GRID_W = 64
LRU_C = 8.0
CONV_LEFT = 2
S5_CHUNK = 16
LRU_STEPS = 32
LANES = 128
VMEM_LIMIT = 56 * 1024 * 1024


def _cparams(*sem):
    return pltpu.CompilerParams(dimension_semantics=sem, vmem_limit_bytes=VMEM_LIMIT)


def _rms(x, g):
    return x * lax.rsqrt(jnp.mean(x * x, axis=-1, keepdims=True) + EPS) * g


def _mod_kernel(c_ref, w_ref, b_ref, o_ref):
    c = c_ref[...]
    s = c * jax.nn.sigmoid(c)
    o_ref[...] = jnp.dot(s, w_ref[...], preferred_element_type=F32, precision=HIGHEST) + b_ref[...]


def _modulation(c_rows, w_mod, b_mod):
    depth, d, n6 = w_mod.shape
    rows = c_rows.shape[0]
    tn = n6 // 4
    return pl.pallas_call(
        _mod_kernel,
        grid=(depth, n6 // tn),
        in_specs=[pl.BlockSpec((rows, d), lambda l, j: (0, 0)),
                  pl.BlockSpec((None, d, tn), lambda l, j: (l, 0, j)),
                  pl.BlockSpec((None, 1, tn), lambda l, j: (l, 0, j))],
        out_specs=pl.BlockSpec((None, rows, tn), lambda l, j: (l, 0, j)),
        out_shape=jax.ShapeDtypeStruct((depth, rows, n6), F32),
        compiler_params=_cparams("parallel", "parallel"),
        name="modulation",
    )(c_rows, w_mod, b_mod.reshape(depth, 1, n6))


def _mod_specs(layer, part, d, nb):
    return [pl.BlockSpec((None, nb, 1, d), lambda ct, t: (layer, 0, 0, part)),
            pl.BlockSpec((None, None, 1, d), lambda ct, t: (layer, nb, 0, part))]


def _mod_pick(ct, batch_ref, ctx_ref):
    return jnp.where(ct == 0, ctx_ref[...][None], batch_ref[...])


def _mod_spec_lat(layer, part, d):
    return pl.BlockSpec((None, None, 1, d), lambda b, i: (layer, b, 0, part))


def _view4(a):
    nb, n, w = a.shape
    return a.reshape(nb, n // S5_CHUNK, S5_CHUNK, w)


def _tile_slice(a4, nb, ct, t, cc, off=0):
    return a4.at[pl.ds(0, nb), pl.ds((ct - off) * cc, cc), t, :]


class _TileStreams:
    def __init__(self, ins, outs, isem, osem, cc, in_off, nsteps):
        self.ins, self.outs, self.isem, self.osem = ins, outs, isem, osem
        self.cc, self.in_off, self.nsteps = cc, in_off, nsteps
        self.nb = ins[0][1].shape[1]
        self.ct = pl.program_id(0) + in_off
        self.t = pl.program_id(1)
        self.q = pl.num_programs(1)
        self.s = pl.program_id(0) * self.q + self.t
        self.slot = lax.rem(self.s, 2)

    def _start_inputs(self, ct, t, sl):
        for srcs, buf in self.ins:
            if len(srcs) == 2:
                @pl.when(ct == 0)
                def _():
                    pltpu.make_async_copy(_tile_slice(srcs[0], self.nb,0, t, self.cc), buf.at[sl], self.isem.at[sl]).start()

                @pl.when(ct > 0)
                def _():
                    pltpu.make_async_copy(_tile_slice(srcs[1], self.nb,ct, t, self.cc, 1), buf.at[sl],
                                          self.isem.at[sl]).start()
            else:
                pltpu.make_async_copy(_tile_slice(srcs[0], self.nb,ct, t, self.cc), buf.at[sl], self.isem.at[sl]).start()

    def _wait_outputs(self, sl):
        for arr, buf, _ in self.outs:
            pltpu.make_async_copy(buf.at[sl], _tile_slice(arr, self.nb,0, 0, self.cc), self.osem.at[sl]).wait()

    def begin(self):
        ct, t, s, slot = self.ct, self.t, self.s, self.slot

        @pl.when(s == 0)
        def _():
            self._start_inputs(ct, t, 0)

        @pl.when(s + 1 < self.nsteps)
        def _():
            wrap = t == self.q - 1
            self._start_inputs(jnp.where(wrap, ct + 1, ct), jnp.where(wrap, 0, t + 1), 1 - slot)

        for srcs, buf in self.ins:
            pltpu.make_async_copy(_tile_slice(srcs[0], self.nb,0, 0, self.cc), buf.at[slot], self.isem.at[slot]).wait()

        @pl.when(s >= 2)
        def _():
            self._wait_outputs(slot)

    def end(self):
        for arr, buf, off in self.outs:
            pltpu.make_async_copy(buf.at[self.slot], _tile_slice(arr, self.nb,self.ct, self.t, self.cc, off),
                                  self.osem.at[self.slot]).start()

        @pl.when(self.s == self.nsteps - 1)
        def _():
            if self.nsteps >= 2:
                self._wait_outputs(1 - self.slot)
            self._wait_outputs(self.slot)


def _inproj_kernel(has_y, nsrc, ngrp, cc, nsteps, *refs):
    srcs, refs = list(refs[:nsrc]), refs[nsrc:]
    if has_y:
        y_hbm, gab_ref, gac_ref = refs[:3]
        refs = refs[3:]
    shb_ref, shc_ref, scb_ref, scc_ref, g_ref, wu_ref, wrq_ref = refs[:7]
    refs = refs[7:]
    if has_y:
        xo_hbm, ut_ref, r_hbm, q_hbm, xbuf, rbuf, qbuf, isem, osem, ybuf, xobuf = refs
        ins = [(srcs, xbuf), ([y_hbm], ybuf)]
        outs = [(xo_hbm, xobuf, 0), (r_hbm, rbuf, 0), (q_hbm, qbuf, 0)]
    else:
        ut_ref, r_hbm, q_hbm, xbuf, rbuf, qbuf, isem, osem = refs
        ins = [(srcs, xbuf)]
        outs = [(r_hbm, rbuf, 0), (q_hbm, qbuf, 0)]
    st = _TileStreams(ins, outs, isem, osem, cc, 0, nsteps)
    st.begin()
    ct, slot = st.ct, st.slot
    x = xbuf[slot]
    if has_y:
        x = x + _mod_pick(ct, gab_ref, gac_ref) * ybuf[slot]
        xobuf[slot] = x
    nb, _, d = x.shape
    d_lru = rbuf.shape[-1]
    h = _rms(x, g_ref[...]) * (1.0 + _mod_pick(ct, scb_ref, scc_ref)) + _mod_pick(ct, shb_ref, shc_ref)
    hb = h.reshape(nb * cc, d).astype(BF16)
    ut = lax.dot_general(wu_ref[...], hb, (((1,), (1,)), ((), ())), preferred_element_type=F32)
    ut_ref[...] = ut.reshape(ngrp, ut.shape[0] // ngrp, nb * cc)
    p = jnp.dot(hb, wrq_ref[...], preferred_element_type=F32)
    rbuf[slot] = p[:, :d_lru].reshape(nb, cc, d_lru)
    qbuf[slot] = p[:, d_lru:].reshape(nb, cc, d_lru)
    st.end()


def _inproj(xs, y, mods, layer, g1, w_in, ngrp, d_s5, d_lru, cc):
    q = S5_CHUNK
    nb, d = xs[0].shape[0], xs[0].shape[-1]
    n = sum(a.shape[1] for a in xs)
    nchunk = n // q
    has_y = y is not None
    nh = d_s5 // ngrp
    nsteps = (nchunk // cc) * q
    anyspec = pl.BlockSpec(memory_space=pl.ANY)
    in_specs = [anyspec] * len(xs)
    args = [_view4(a) for a in xs]
    if has_y:
        in_specs += [anyspec] + _mod_specs(layer - 1, 5, d, nb)
        args += [_view4(y), mods, mods]
    wu = w_in[:, :d_s5].T.astype(BF16)
    wrq = w_in[:, d_s5:].astype(BF16)
    full = lambda a: pl.BlockSpec(a.shape, lambda ct, t: (0,) * a.ndim)
    in_specs += _mod_specs(layer, 0, d, nb) + _mod_specs(layer, 1, d, nb) + [
        pl.BlockSpec((1, d), lambda ct, t: (0, 0)), full(wu), full(wrq)]
    args += [mods, mods, mods, mods, g1.reshape(1, d), wu, wrq]
    out_specs = [pl.BlockSpec((ngrp, nh, nb * cc), lambda ct, t: (0, t, ct)), anyspec, anyspec]
    out_shape = [jax.ShapeDtypeStruct((ngrp, q * nh, nb * nchunk), F32),
                 jax.ShapeDtypeStruct((nb, nchunk, q, d_lru), F32),
                 jax.ShapeDtypeStruct((nb, nchunk, q, d_lru), F32)]
    buf = lambda w: pltpu.VMEM((2, nb, cc, w), F32)
    scratch = [buf(d), buf(d_lru), buf(d_lru), pltpu.SemaphoreType.DMA((2,)), pltpu.SemaphoreType.DMA((2,))]
    if has_y:
        out_specs = [anyspec] + out_specs
        out_shape = [jax.ShapeDtypeStruct((nb, nchunk, q, d), F32)] + out_shape
        scratch += [buf(d), buf(d)]
    res = pl.pallas_call(
        functools.partial(_inproj_kernel, has_y, len(xs), ngrp, cc, nsteps),
        grid=(nchunk // cc, q), in_specs=in_specs, out_specs=out_specs, out_shape=out_shape,
        scratch_shapes=scratch,
        compiler_params=_cparams("arbitrary", "arbitrary"),
        name="inproj",
    )(*args)
    if has_y:
        xs = [res[0].reshape(nb, n, d)]
        res = res[1:]
    return [xs, res[0], res[1].reshape(nb, n, d_lru), res[2].reshape(nb, n, d_lru)]


def _s5_weights(a_re, a_im, log_dt, b_re, b_im, c_re, c_im):
    q = S5_CHUNK
    lam = lax.complex(a_re.astype(F32), a_im.astype(F32))
    dt = jnp.exp(log_dt.astype(F32))[..., None]
    a_bar = jnp.exp(lam * dt)
    bmat = lax.complex(b_re.astype(F32), b_im.astype(F32))
    b_bar = ((a_bar - 1) / lam)[..., None] * bmat
    cmat = lax.complex(c_re.astype(F32), c_im.astype(F32))
    k = jnp.arange(q + 1, dtype=F32)
    apow = jnp.exp((lam * dt)[..., None] * k)
    ngrp, nst = a_re.shape[1], a_re.shape[2]
    nh = b_re.shape[-1]

    kern = jnp.einsum('dgop,dgpk,dgpi->dgkoi', cmat, apow[..., :q], b_bar, precision=HIGHEST).real
    sig = jnp.arange(q)[:, None]
    tau = jnp.arange(q)[None, :]
    lag_f = jnp.clip(tau - sig, 0, q - 1)
    lag_b = jnp.clip(sig - tau, 0, q - 1)
    tf = jnp.where((sig <= tau)[None, :, :, None, None], kern[0][:, lag_f], 0.0)
    tb = jnp.where((sig >= tau)[None, :, :, None, None], kern[1][:, lag_b], 0.0)
    toep = (tf + tb).transpose(0, 1, 4, 2, 3).reshape(ngrp, q * nh, q * nh)

    win_f = jnp.einsum('gps,gpi->gsip', apow[0][..., :q][..., ::-1], b_bar[0])
    win_b = jnp.einsum('gps,gpi->gsip', apow[1][..., :q], b_bar[1])
    win = jnp.concatenate([win_f.real, win_b.real, win_f.imag, win_b.imag], axis=-1)
    win = win.reshape(ngrp, q * nh, 4 * nst)

    cf = jnp.einsum('gop,gpt->gpto', cmat[0], apow[0][..., 1:])
    cb = jnp.einsum('gop,gpt->gpto', cmat[1], apow[1][..., 1:][..., ::-1])
    z = jnp.zeros_like(cf.real)
    wof = jnp.concatenate([cf.real, z, -cf.imag, z], axis=1).reshape(ngrp, 4 * nst, q * nh)
    wob = jnp.concatenate([z, cb.real, z, -cb.imag], axis=1).reshape(ngrp, 4 * nst, q * nh)

    aq = apow[..., q]
    a16 = jnp.stack([jnp.concatenate([aq[0].real, aq[1].real], -1),
                     jnp.concatenate([aq[0].imag, aq[1].imag], -1)], axis=1)
    tr = lambda a: a.transpose(0, 2, 1).astype(BF16)
    return tr(toep), tr(win), tr(wof), tr(wob), a16.astype(F32)


def _s5_kernel(nb, cc, nc_ctx, nc_lat, u_ref, toep_ref, win_ref, wof_ref, wob_ref, a_ref, y_ref,
               sre_ref, sim_ref, fre_ref, fim_ref, bre_ref, bim_ref):
    qh, m = u_ref.shape
    ns2 = a_ref.shape[-1]
    mt = nb * cc
    toep = toep_ref[...]
    win = win_ref[...]

    def intra(i, c):
        r0 = pl.multiple_of(i * mt, mt)
        ub = u_ref[:, pl.ds(r0, mt)].astype(BF16)
        y_ref[:, pl.ds(r0, mt)] = jnp.dot(toep, ub, preferred_element_type=F32)
        s = jnp.dot(win, ub, preferred_element_type=F32).T
        sre_ref[pl.ds(r0, mt), :] = s[:, :ns2]
        sim_ref[pl.ds(r0, mt), :] = s[:, ns2:]
        return c

    lax.fori_loop(0, m // mt, intra, 0)

    are = jnp.broadcast_to(a_ref[0:1, :], (nb, ns2))
    aim = jnp.broadcast_to(a_ref[1:2, :], (nb, ns2))
    fwd_lane = lax.broadcasted_iota(jnp.int32, (nb, ns2), 1) < ns2 // 2

    def rows(c):
        tile = lax.div(c, cc)
        return pl.ds(tile * mt + (c - tile * cc), nb, stride=cc)

    def scan(base, n, carry):
        def step(k, hc):
            hre, him = hc
            rf = rows(base + k)
            rb = rows(base + n - 1 - k)
            fre_ref[rf, :] = hre
            fim_ref[rf, :] = him
            bre_ref[rb, :] = hre
            bim_ref[rb, :] = him
            sre = jnp.where(fwd_lane, sre_ref[rf, :], sre_ref[rb, :])
            sim = jnp.where(fwd_lane, sim_ref[rf, :], sim_ref[rb, :])
            return (are * hre - aim * him + sre, are * him + aim * hre + sim)
        return lax.fori_loop(0, n, step, carry)

    zero = jnp.zeros((nb, ns2), F32)
    carry = scan(0, nc_ctx, (zero, zero))
    scan(nc_ctx, nc_lat, carry)

    wof = wof_ref[...]
    wob = wob_ref[...]
    nt_dims = (((1,), (1,)), ((), ()))

    def inter(i, c):
        r0 = pl.multiple_of(i * mt, mt)
        hf = jnp.concatenate([fre_ref[pl.ds(r0, mt), :], fim_ref[pl.ds(r0, mt), :]], axis=1).astype(BF16)
        hb = jnp.concatenate([bre_ref[pl.ds(r0, mt), :], bim_ref[pl.ds(r0, mt), :]], axis=1).astype(BF16)
        y_ref[:, pl.ds(r0, mt)] += (lax.dot_general(wof, hf, nt_dims, preferred_element_type=F32)
                                    + lax.dot_general(wob, hb, nt_dims, preferred_element_type=F32))
        return c

    lax.fori_loop(0, m // mt, inter, 0)


def _s5_scan(u_t, toep, win, wof, wob, a16, nb, cc, nc_ctx, nc_lat):
    ngrp, qh, m = u_t.shape
    ns2 = a16.shape[-1]
    grp = lambda a: pl.BlockSpec((None,) + a.shape[1:], lambda g: (g, 0, 0))
    return pl.pallas_call(
        functools.partial(_s5_kernel, nb, cc, nc_ctx, nc_lat),
        grid=(ngrp,),
        in_specs=[grp(u_t), grp(toep), grp(win), grp(wof), grp(wob), grp(a16)],
        out_specs=grp(u_t),
        out_shape=jax.ShapeDtypeStruct(u_t.shape, F32),
        scratch_shapes=[pltpu.VMEM((m, ns2), F32)] * 6,
        compiler_params=_cparams("parallel"),
        name="s5_chunked",
    )(u_t, toep, win, wof, wob, a16)


def _lru_chunk(s, reverse, nc_ctx, nc):
    if not reverse:
        return s
    return jnp.where(s < nc_ctx, nc_ctx - 1 - s, nc - 1 - (s - nc_ctx))


def _lru_kernel(reverse, combine, nc_ctx, nc, rows, *refs):
    if combine:
        (r_hbm, q_hbm, ho_hbm, cw_ref, cb_ref, wa_ref, ba_ref, wx_ref, bx_ref, cn_ref, o_hbm,
         rbuf, obuf, h_ref, a_s, b_s, isem, osem, qbuf, hbuf) = refs
    else:
        (r_hbm, cw_ref, cb_ref, wa_ref, ba_ref, wx_ref, bx_ref, cn_ref, o_hbm,
         rbuf, obuf, h_ref, a_s, b_s, isem, osem) = refs
    _, jt, nb, ch = obuf.shape
    gw = r_hbm.shape[2]
    cblk = (nc_ctx * jt) // gw
    ncol = jt // rows
    s = pl.program_id(0)
    slot = lax.rem(s, 2)

    def chunk_of(step):
        return _lru_chunk(jnp.clip(step, 0, nc - 1), reverse, nc_ctx, nc)

    def main_copies(hbm, buf, row0, c, sl, sem, fn):
        @pl.when(c < nc_ctx)
        def _():
            j0 = c * jt
            for b in range(nb):
                fn(hbm.at[b, j0 // gw, pl.ds(j0 % gw, jt), :], buf.at[sl, pl.ds(row0, jt), b, :], sem.at[sl])

        @pl.when(c >= nc_ctx)
        def _():
            w0 = (c - nc_ctx) * ncol
            for b in range(nb):
                for k in range(ncol):
                    fn(hbm.at[b, pl.ds(cblk, rows), w0 + k, :],
                       buf.at[sl, pl.ds(row0 + k * rows, rows), b, :], sem.at[sl])

    def halo_copies(c, sl, fn):
        first = jnp.logical_or(c == 0, c == nc_ctx)
        last = jnp.logical_or(c == nc_ctx - 1, c == nc - 1)
        in_ctx = c < nc_ctx

        @pl.when(jnp.logical_and(in_ctx, jnp.logical_not(first)))
        def _():
            j = c * jt - CONV_LEFT
            for b in range(nb):
                fn(r_hbm.at[b, j // gw, pl.ds(j % gw, CONV_LEFT), :], rbuf.at[sl, pl.ds(0, CONV_LEFT), b, :],
                   isem.at[sl])

        @pl.when(jnp.logical_and(in_ctx, jnp.logical_not(last)))
        def _():
            j = c * jt + jt
            for b in range(nb):
                fn(r_hbm.at[b, j // gw, pl.ds(j % gw, 1), :], rbuf.at[sl, pl.ds(jt + CONV_LEFT, 1), b, :],
                   isem.at[sl])

        @pl.when(jnp.logical_and(jnp.logical_not(in_ctx), jnp.logical_not(first)))
        def _():
            w = (c - nc_ctx) * ncol - 1
            for b in range(nb):
                fn(r_hbm.at[b, pl.ds(cblk + rows - CONV_LEFT, CONV_LEFT), w, :],
                   rbuf.at[sl, pl.ds(0, CONV_LEFT), b, :], isem.at[sl])

        @pl.when(jnp.logical_and(jnp.logical_not(in_ctx), jnp.logical_not(last)))
        def _():
            w = (c - nc_ctx) * ncol + ncol
            for b in range(nb):
                fn(r_hbm.at[b, pl.ds(cblk, 1), w, :], rbuf.at[sl, pl.ds(jt + CONV_LEFT, 1), b, :], isem.at[sl])

    def loads(c, sl, fn):
        main_copies(r_hbm, rbuf, CONV_LEFT, c, sl, isem, fn)
        halo_copies(c, sl, fn)
        if combine:
            main_copies(q_hbm, qbuf, 0, c, sl, isem, fn)
            main_copies(ho_hbm, hbuf, 0, c, sl, isem, fn)

    def stores(c, sl, fn):
        main_copies(o_hbm, obuf, 0, c, sl, osem, lambda hbm, buf, sem: fn(buf, hbm, sem))

    start = lambda src, dst, sem: pltpu.make_async_copy(src, dst, sem).start()
    wait = lambda src, dst, sem: pltpu.make_async_copy(src, dst, sem).wait()

    c = chunk_of(s)
    first = jnp.logical_or(c == 0, c == nc_ctx)
    last = jnp.logical_or(c == nc_ctx - 1, c == nc - 1)

    @pl.when(s == 0)
    def _():
        h_ref[...] = jnp.zeros_like(h_ref)
        rbuf[...] = jnp.zeros_like(rbuf)
        loads(c, 0, start)

    @pl.when(s + 1 < nc)
    def _():
        loads(chunk_of(s + 1), 1 - slot, start)

    loads(c, slot, wait)

    @pl.when(s >= 2)
    def _():
        stores(chunk_of(s - 2), slot, wait)

    rb = rbuf[slot]
    prev = jnp.where(first, 0.0, rb[0:CONV_LEFT])
    nxt = jnp.where(last, 0.0, rb[jt + CONV_LEFT:jt + CONV_LEFT + 1])
    xe = jnp.concatenate([prev, rb[CONV_LEFT:jt + CONV_LEFT], nxt], axis=0)
    cw = cw_ref[...]
    xc = cb_ref[...].reshape(1, 1, ch)
    for k in range(cw.shape[0]):
        xc = xc + cw[k:k + 1, :].reshape(1, 1, ch) * xe[k:k + jt]
    xc2 = xc.reshape(jt * nb, ch)

    xb = xc2.astype(BF16)
    nblk, wb = wa_ref.shape[0], wa_ref.shape[1]
    za = jnp.concatenate([jnp.dot(xb[:, i * wb:(i + 1) * wb], wa_ref[i], preferred_element_type=F32)
                          for i in range(nblk)], axis=-1)
    zx = jnp.concatenate([jnp.dot(xb[:, i * wb:(i + 1) * wb], wx_ref[i], preferred_element_type=F32)
                          for i in range(nblk)], axis=-1)
    rg = jax.nn.sigmoid(za + ba_ref[...])
    ig = jax.nn.sigmoid(zx + bx_ref[...])
    log_a = cn_ref[...] * rg
    a = jnp.exp(log_a)
    bb = jnp.sqrt(1.0 - a * a) * (ig * xc2)
    a_s[...] = a.reshape(jt, nb, ch)
    b_s[...] = bb.reshape(jt, nb, ch)

    def step(t, h):
        tt = jt - 1 - t if reverse else t
        h = a_s[tt] * h + b_s[tt]
        if combine:
            obuf[slot, tt] = (h + hbuf[slot, tt]) * jax.nn.gelu(qbuf[slot, tt])
        else:
            obuf[slot, tt] = h
        return h

    h_ref[...] = lax.fori_loop(0, jt, step, h_ref[...])

    stores(c, slot, start)

    @pl.when(s == nc - 1)
    def _():
        if nc >= 2:
            stores(chunk_of(s - 1), 1 - slot, wait)
        stores(c, slot, wait)


def _lru_pass(reverse, r4, q4, h_other4, conv_w, conv_b, wa, ba, wx, bx, cneg, n_ctx, rows):
    nb, nblk, gw, ch = r4.shape
    jt = LRU_STEPS
    nc, nc_ctx = (nblk * gw) // jt, n_ctx // jt
    combine = h_other4 is not None
    assert gw % jt == 0 and n_ctx % gw == 0 and jt % rows == 0 and rows >= CONV_LEFT
    anyspec = pl.BlockSpec(memory_space=pl.ANY)
    full = lambda a: pl.BlockSpec(a.shape, lambda s: (0,) * a.ndim)
    consts = [conv_w, conv_b, wa, ba, wx, bx, cneg]
    big = [r4, q4, h_other4] if combine else [r4]
    buf = lambda n: pltpu.VMEM((2, n, nb, ch), F32)
    scratch = [buf(jt + CONV_LEFT + 1), buf(jt), pltpu.VMEM((nb, ch), F32),
               pltpu.VMEM((jt, nb, ch), F32), pltpu.VMEM((jt, nb, ch), F32),
               pltpu.SemaphoreType.DMA((2,)), pltpu.SemaphoreType.DMA((2,))]
    if combine:
        scratch += [buf(jt), buf(jt)]
    return pl.pallas_call(
        functools.partial(_lru_kernel, reverse, combine, nc_ctx, nc, rows),
        grid=(nc,), in_specs=[anyspec] * len(big) + [full(a) for a in consts], out_specs=anyspec,
        out_shape=jax.ShapeDtypeStruct(r4.shape, F32),
        scratch_shapes=scratch,
        compiler_params=_cparams("arbitrary"),
        name="lru_bwd" if reverse else "lru_fwd",
    )(*big, *consts)


def _block_diag_halves(w):
    nh, hd, _ = w.shape
    half = nh // 2
    eye = jnp.eye(half, dtype=w.dtype)
    wh = w.reshape(2, half, hd, hd)
    bd = jnp.einsum('bhij,hk->bhikj', wh, eye).reshape(2, half * hd, half * hd)
    return bd.astype(BF16)


def _post_kernel(n_groups, n_per, nsrc, off, cc, nsteps, *refs):
    srcs, refs = list(refs[:nsrc]), refs[nsrc:]
    (lr_hbm, yt_ref, ut_ref, gab_ref, gac_ref, shb_ref, shc_ref, scb_ref, scc_ref, d_ref, wg_ref, bg_ref,
     wo_ref, g2_ref, wr_ref, br_ref, x1_hbm, hm_hbm, meta_ref, xbuf, lbuf, x1buf, hmbuf, isem, osem) = refs
    st = _TileStreams([(srcs, xbuf), ([lr_hbm], lbuf)], [(x1_hbm, x1buf, off), (hm_hbm, hmbuf, off)],
                      isem, osem, cc, off, nsteps)
    st.begin()
    ct, slot = st.ct, st.slot
    x_in = xbuf[slot]
    nb, _, d = x_in.shape
    tm = nb * cc
    d_s5 = d_ref.shape[0]
    ys = yt_ref[...].reshape(d_s5, tm) + d_ref[...] * ut_ref[...].reshape(d_s5, tm)
    z = jax.nn.gelu(ys)
    s5t = z * jax.nn.sigmoid(jnp.dot(wg_ref[...], z.astype(BF16), preferred_element_type=F32) + bg_ref[...])
    mix = jnp.concatenate([s5t.T.astype(BF16), lbuf[slot].reshape(tm, -1).astype(BF16)], axis=-1)
    proj = jnp.dot(mix, wo_ref[...], preferred_element_type=F32).reshape(nb, cc, d)
    x1 = x_in + _mod_pick(ct, gab_ref, gac_ref) * proj
    x1buf[slot] = x1
    hm3 = _rms(x1, g2_ref[...]) * (1.0 + _mod_pick(ct, scb_ref, scc_ref)) + _mod_pick(ct, shb_ref, shc_ref)
    hmbuf[slot] = hm3
    hm = hm3.reshape(tm, d)

    hi = hm.astype(BF16)
    lo = (hm - hi.astype(F32)).astype(BF16)
    two = jnp.dot(jnp.concatenate([hi, lo], axis=-1), wr_ref[...], preferred_element_type=F32)
    logits = two[:, :LANES] + two[:, LANES:] + br_ref[...]

    lane = lax.broadcasted_iota(jnp.int32, (tm, LANES), 1).astype(F32)
    neg = jnp.float32(-jnp.inf)
    big = jnp.float32(LANES)
    lg = jnp.where(lane < n_groups, logits, neg)
    mg = jnp.max(lg, axis=-1, keepdims=True)
    g_p = 1.0 / jnp.sum(jnp.exp(lg - mg), axis=-1, keepdims=True)
    gidx = jnp.min(jnp.where(lg == mg, lane, big), axis=-1, keepdims=True)
    e0 = n_groups + n_per * gidx
    le = jnp.where(jnp.logical_and(lane >= e0, lane < e0 + n_per), logits, neg)
    m1 = jnp.max(le, axis=-1, keepdims=True)
    i1 = jnp.min(jnp.where(le == m1, lane, big), axis=-1, keepdims=True)
    le2 = jnp.where(lane == i1, neg, le)
    m2 = jnp.max(le2, axis=-1, keepdims=True)
    i2 = jnp.min(jnp.where(le2 == m2, lane, big), axis=-1, keepdims=True)
    r2 = jnp.exp(m2 - m1)
    w1 = g_p / (1.0 + r2)
    w2 = g_p * r2 / (1.0 + r2)
    j1 = i1 - e0
    j2 = i2 - e0
    jlo = jnp.minimum(j1, j2)
    jhi = jnp.maximum(j1, j2)
    wlo = jnp.where(j1 < j2, w1, w2)
    whi = jnp.where(j1 < j2, w2, w1)
    pair = jlo * (2 * n_per - 1 - jlo) * 0.5 + (jhi - jlo - 1.0)
    bucket = gidx * (n_per * (n_per - 1) // 2) + pair
    meta = jnp.where(lane == 0, bucket, jnp.where(lane == 1, wlo, jnp.where(lane == 2, whi, 0.0)))
    meta_ref[...] = meta.T[0:8, :]
    st.end()


def _post(lat_only, yt, ut, lr, xs, mods, layer, s5_d, w_glu, b_glu, w_out_bf, g2, wr2, br, n_groups, n_per, cc):
    q = S5_CHUNK
    nb, d = xs[0].shape[0], xs[0].shape[-1]
    n = sum(a.shape[1] for a in xs)
    nchunk = n // q
    ngrp, qh, _ = yt.shape
    nh = qh // q
    d_s5 = ngrp * nh
    d_lru = lr.shape[-1]
    off = 1 if lat_only else 0
    assert not (lat_only and len(xs) == 2)
    nct = nchunk // cc - off
    tm = nb * cc
    anyspec = pl.BlockSpec(memory_space=pl.ANY)
    tspec = pl.BlockSpec((ngrp, nh, tm), lambda ct, t: (0, t, ct + off))
    full = lambda a: pl.BlockSpec(a.shape, lambda ct, t: (0,) * a.ndim)
    consts = [s5_d.reshape(d_s5, 1), w_glu.T.astype(BF16), b_glu.reshape(d_s5, 1), w_out_bf, g2.reshape(1, d),
              wr2, br]
    buf = lambda w: pltpu.VMEM((2, nb, cc, w), F32)
    x1, hm, meta = pl.pallas_call(
        functools.partial(_post_kernel, n_groups, n_per, len(xs), off, cc, nct * q),
        grid=(nct, q),
        in_specs=([anyspec] * (len(xs) + 1) + [tspec, tspec]
                  + _mod_specs(layer, 2, d, nb) + _mod_specs(layer, 3, d, nb) + _mod_specs(layer, 4, d, nb)
                  + [full(a) for a in consts]),
        out_specs=[anyspec, anyspec, pl.BlockSpec((None, None, 8, tm), lambda ct, t: (t, ct, 0, 0))],
        out_shape=[jax.ShapeDtypeStruct((nb, nct * cc, q, d), F32), jax.ShapeDtypeStruct((nb, nct * cc, q, d), F32),
                   jax.ShapeDtypeStruct((q, nct, 8, tm), F32)],
        scratch_shapes=[buf(d), buf(d_lru), buf(d), buf(d),
                        pltpu.SemaphoreType.DMA((2,)), pltpu.SemaphoreType.DMA((2,))],
        compiler_params=_cparams("arbitrary", "arbitrary"),
        name="post_mixer",
    )(*[_view4(a) for a in xs], _view4(lr), yt, ut, mods, mods, mods, mods, mods, mods, *consts)
    n_out = nct * cc * q
    return x1.reshape(nb, n_out, d), hm.reshape(nb, n_out, d), meta


def _moe_kernel(n_rows, nt, tb_ref, src0_ref, srcn_ref, dst_ref, hm_hbm, wt_ref, w1a, w3a, w2a, w1b, w3b, w2b,
                y_hbm, xbuf, ybuf, gsem, ssem):
    i = pl.program_id(0)
    slot = lax.rem(i, 2)
    tmr = xbuf.shape[1]

    def valid(j):
        return tb_ref[2, jnp.clip(j, 0, nt - 1)] > 0

    def start_gathers(ids_ref, s):
        for r in range(tmr):
            pltpu.make_async_copy(hm_hbm.at[pl.ds(ids_ref[0, r], 1), :], xbuf.at[s, pl.ds(r, 1), :],
                                  gsem.at[s]).start(priority=r % 2)

    def start_scatters(s):
        for r in range(tmr):
            pltpu.make_async_copy(ybuf.at[s, pl.ds(r, 1), :], y_hbm.at[pl.ds(dst_ref[0, r], 1), :],
                                  ssem.at[s]).start(priority=r % 2)

    def wait_gathers(s):
        pltpu.make_async_copy(hm_hbm.at[pl.ds(0, tmr), :], xbuf.at[s], gsem.at[s]).wait()

    def wait_scatters(s):
        pltpu.make_async_copy(ybuf.at[s], y_hbm.at[pl.ds(0, tmr), :], ssem.at[s]).wait()

    @pl.when(i == 0)
    def _():
        xbuf[...] = jnp.zeros_like(xbuf)
        ybuf[...] = jnp.zeros_like(ybuf)
        for s in range((y_hbm.shape[0] - n_rows) // tmr):
            cp = pltpu.make_async_copy(ybuf.at[0], y_hbm.at[pl.ds(n_rows + s * tmr, tmr), :], ssem.at[0])
            cp.start()
            cp.wait()
        start_gathers(src0_ref, 0)

    @pl.when(jnp.logical_and(i + 1 < nt, valid(i + 1)))
    def _():
        start_gathers(srcn_ref, 1 - slot)

    @pl.when(jnp.logical_and(i >= 2, valid(i - 2)))
    def _():
        wait_scatters(slot)

    @pl.when(valid(i))
    def _():
        wait_gathers(slot)
        xb = xbuf[slot].astype(BF16)
        wt = wt_ref[...]

        def expert(w1, w3, w2, gate):
            h1 = jnp.dot(xb, w1[...], preferred_element_type=F32)
            h3 = jnp.dot(xb, w3[...], preferred_element_type=F32)
            hid = (h1 * jax.nn.sigmoid(h1)) * h3
            return gate * jnp.dot(hid.astype(BF16), w2[...], preferred_element_type=F32)

        half = LANES // 2
        ybuf[slot] = expert(w1a, w3a, w2a, wt[:, 0:1]) + expert(w1b, w3b, w2b, wt[:, half:half + 1])
        start_scatters(slot)

    @pl.when(i == nt - 1)
    def _():
        @pl.when(jnp.logical_and(nt >= 2, valid(i - 1)))
        def _():
            wait_scatters(1 - slot)

        @pl.when(valid(i))
        def _():
            wait_scatters(slot)


def _moe(hm2, bucket, wlo, whi, tok_rows, w1_bf, w3_bf, w2_bf, n_per, tmr, spare_rows):
    n_rows, d = hm2.shape
    assert spare_rows >= 2 * tmr and spare_rows % tmr == 0
    t = bucket.shape[0]
    npairs = n_per * (n_per - 1) // 2
    nbuck = (w1_bf.shape[0] // n_per) * npairs
    ntiles = t // tmr + nbuck

    _, s_tok, s_lo, s_hi = lax.sort((bucket, tok_rows, wlo, whi), num_keys=1, is_stable=True)
    counts = jnp.sum((bucket[None, :] == jnp.arange(nbuck, dtype=jnp.int32)[:, None]).astype(jnp.int32), axis=1)
    padded = ((counts + tmr - 1) // tmr) * tmr
    pend = jnp.cumsum(padded)
    cend = jnp.cumsum(counts)
    shift = (pend - padded) - (cend - counts)
    tile_start = jnp.arange(ntiles, dtype=jnp.int32) * tmr
    valid = (tile_start < pend[-1]).astype(jnp.int32)
    tb = jnp.sum((tile_start[:, None] >= pend[None, :]).astype(jnp.int32), axis=1)
    tb = jnp.minimum(tb, jnp.sum((pend < pend[-1]).astype(jnp.int32)))
    tb = jnp.minimum(tb, nbuck - 1)
    pos = (tile_start[:, None] + jnp.arange(tmr, dtype=jnp.int32)[None, :]) - shift[tb][:, None]
    real = jnp.logical_and(pos < cend[tb][:, None], valid[:, None] > 0)
    pos = jnp.clip(pos, 0, t - 1)
    src = jnp.where(real, s_tok[pos], 0)
    spare = n_rows + (jnp.arange(ntiles, dtype=jnp.int32) % 2)[:, None] * tmr + jnp.arange(tmr, dtype=jnp.int32)
    dst = jnp.where(real, s_tok[pos], spare)
    g_lo = jnp.where(real, s_lo[pos], 0.0).reshape(-1, 1)
    g_hi = jnp.where(real, s_hi[pos], 0.0).reshape(-1, 1)
    half = LANES // 2
    wts = jnp.concatenate([jnp.broadcast_to(g_lo, (ntiles * tmr, half)),
                           jnp.broadcast_to(g_hi, (ntiles * tmr, half))], axis=1)
    grp, pr = tb // npairs, tb % npairs
    pairs = [(i, j) for i in range(n_per) for j in range(i + 1, n_per)]
    plo = jnp.array([p[0] for p in pairs], jnp.int32)[pr]
    phi = jnp.array([p[1] for p in pairs], jnp.int32)[pr]
    tinfo = jnp.stack([grp * n_per + plo, grp * n_per + phi, valid])

    wspec = lambda a, row: pl.BlockSpec((None,) + a.shape[1:], lambda i, tb_: (tb_[row, i], 0, 0))
    ids = lambda f: pl.BlockSpec((None, 1, tmr), lambda i, tb_: (f(i), 0, 0), memory_space=pltpu.SMEM)
    grid_spec = pltpu.PrefetchScalarGridSpec(
        num_scalar_prefetch=1,
        grid=(ntiles,),
        in_specs=[ids(lambda i: 0), ids(lambda i: jnp.minimum(i + 1, ntiles - 1)), ids(lambda i: i),
                  pl.BlockSpec(memory_space=pl.ANY),
                  pl.BlockSpec((tmr, LANES), lambda i, tb_: (i, 0)),
                  wspec(w1_bf, 0), wspec(w3_bf, 0), wspec(w2_bf, 0),
                  wspec(w1_bf, 1), wspec(w3_bf, 1), wspec(w2_bf, 1)],
        out_specs=pl.BlockSpec(memory_space=pl.ANY),
        scratch_shapes=[pltpu.VMEM((2, tmr, d), F32), pltpu.VMEM((2, tmr, d), F32),
                        pltpu.SemaphoreType.DMA((2,)), pltpu.SemaphoreType.DMA((2,))],
    )
    src3 = src.reshape(ntiles, 1, tmr)
    return pl.pallas_call(
        functools.partial(_moe_kernel, n_rows, ntiles),
        grid_spec=grid_spec,
        out_shape=jax.ShapeDtypeStruct((n_rows + spare_rows, d), F32),
        compiler_params=_cparams("arbitrary"),
        name="moe_pairs",
    )(tinfo, src3, src3, dst.reshape(ntiles, 1, tmr), hm2, wts, w1_bf, w3_bf, w2_bf, w1_bf, w3_bf, w2_bf)


def _final_kernel(x_ref, y_ref, ga_ref, g_ref, o_ref):
    o_ref[...] = _rms(x_ref[...] + ga_ref[...] * y_ref[...], g_ref[...])


def _final(x1, y, mods, layer, final_g, tm):
    nb, seq, d = x1.shape
    tok = pl.BlockSpec((None, tm, d), lambda b, i: (b, i, 0))
    return pl.pallas_call(
        _final_kernel,
        grid=(nb, seq // tm),
        in_specs=[tok, pl.BlockSpec((tm, d), lambda b, i: (b * (seq // tm) + i, 0)),
                  _mod_spec_lat(layer, 5, d), pl.BlockSpec((1, d), lambda b, i: (0, 0))],
        out_specs=tok,
        out_shape=jax.ShapeDtypeStruct((nb, seq, d), F32),
        compiler_params=_cparams("parallel", "parallel"),
        name="final_norm",
    )(x1, y, mods, final_g.reshape(1, d))


def kernel(x, c, ctx, c_ctx, w_mod, b_mod, norm1_g, norm2_g, w_in, w_out, s5_a_re, s5_a_im, s5_log_dt, s5_b_re, s5_b_im, s5_c_re, s5_c_im, s5_d, s5_w_glu, s5_b_glu, lru_conv_w, lru_conv_b, lru_w_a, lru_b_a, lru_w_x, lru_b_x, lru_lam, moe_w_group, moe_b_group, moe_w_router, moe_b_router, moe_w1, moe_w3, moe_w2, final_g):
    nb, seq, d = x.shape
    n_ctx = ctx.shape[1]
    depth = w_mod.shape[0]
    n = n_ctx + seq
    d_s5 = s5_d.shape[-1]
    d_lru = lru_conv_b.shape[-1]
    ngrp, nh = s5_b_re.shape[2], s5_b_re.shape[4]
    n_groups, n_per = moe_w_router.shape[1], moe_w_router.shape[3]
    rows = seq // GRID_W
    tm = n_ctx
    q = S5_CHUNK
    nc_ctx, nc_lat = n_ctx // q, seq // q
    moe_rows = 256 if nb * seq >= 16384 else 32
    assert seq % GRID_W == 0 and seq % tm == 0 and n_ctx % GRID_W == 0 and seq % LRU_STEPS == 0
    assert d_s5 == ngrp * nh and q * nh == 2 * LANES and n_groups + n_groups * n_per <= LANES

    pad = (-(nb + 1)) % 8
    c_rows = jnp.concatenate([c, c_ctx[None, :], jnp.zeros((pad, d), F32)], axis=0)
    mods = _modulation(c_rows, w_mod, b_mod).reshape(depth, nb + 1 + pad, 1, 6 * d)

    cc = nc_ctx
    xs = [ctx, x]
    y = None
    for l in range(depth):
        need_ctx = l < depth - 1
        xs, ut, r, qg = _inproj(xs, y, mods, l, norm1_g[l], w_in[l], ngrp, d_s5, d_lru, cc)

        toep, win, wof, wob, a16 = _s5_weights(s5_a_re[l], s5_a_im[l], s5_log_dt[l], s5_b_re[l], s5_b_im[l],
                                               s5_c_re[l], s5_c_im[l])
        yt = _s5_scan(ut, toep, win, wof, wob, a16, nb, cc, nc_ctx, nc_lat)

        view = lambda a: a.reshape(nb, n // GRID_W, GRID_W, d_lru)
        cneg = (-LRU_C * jax.nn.softplus(-lru_lam[l].astype(F32))).reshape(2, 1, d_lru)
        cw, cb = lru_conv_w[l].astype(F32), lru_conv_b[l].reshape(1, d_lru).astype(F32)
        gates = [(_block_diag_halves(lru_w_a[l, dd]), lru_b_a[l, dd].reshape(1, d_lru),
                  _block_diag_halves(lru_w_x[l, dd]), lru_b_x[l, dd].reshape(1, d_lru), cneg[dd]) for dd in range(2)]
        h_b = _lru_pass(True, view(r), None, None, cw, cb, *gates[1], n_ctx, rows)
        lr = _lru_pass(False, view(r), view(qg), h_b, cw, cb, *gates[0], n_ctx, rows).reshape(nb, n, d_lru)

        wr = jnp.concatenate([moe_w_group[l], moe_w_router[l].transpose(1, 0, 2).reshape(d, n_groups * n_per)], -1)
        wr = jnp.pad(wr.astype(F32), ((0, 0), (0, LANES - wr.shape[-1])))
        wr_hi = wr.astype(BF16)
        wr_lo = (wr - wr_hi.astype(F32)).astype(BF16)
        wr2 = jnp.concatenate([jnp.concatenate([wr_hi, wr_lo], -1),
                               jnp.concatenate([wr_hi, jnp.zeros_like(wr_lo)], -1)], axis=0)
        br = jnp.concatenate([moe_b_group[l], moe_b_router[l].reshape(-1)])
        br = jnp.pad(br.astype(F32), (0, LANES - br.shape[0])).reshape(1, LANES)

        x1, hm, meta = _post(not need_ctx, yt, ut, lr, xs, mods, l, s5_d[l], s5_w_glu[l], s5_b_glu[l],
                             w_out[l].astype(BF16), norm2_g[l], wr2, br, n_groups, n_per, cc)

        n_out = x1.shape[1]
        nct = n_out // (cc * q)
        meta_t = meta.reshape(q, nct, 8, nb, cc).transpose(2, 3, 1, 4, 0).reshape(8, nb * n_out)
        bucket = meta_t[0].astype(jnp.int32)
        y = _moe(hm.reshape(nb * n_out, d), bucket, meta_t[1], meta_t[2], jnp.arange(nb * n_out, dtype=jnp.int32),
                 moe_w1[l].astype(BF16), moe_w3[l].astype(BF16), moe_w2[l].astype(BF16), n_per, moe_rows, n_out)
        xs = [x1]
        if need_ctx:
            y = y.reshape(nb + 1, n_out, d)
    return _final(xs[0], y, mods, depth - 1, final_g, tm)
```

```python
import functools

import jax
import jax.numpy as jnp
from jax import lax
from jax.experimental import pallas as pl
from jax.experimental.pallas import tpu as pltpu

F32 = jnp.float32
BF16 = jnp.bfloat16
HIGHEST = lax.Precision.HIGHEST

EPS = 1e-6
GRID_W = 64
LRU_C = 8.0
CONV_LEFT = 2
S5_CHUNK = 16
LRU_STEPS = 32
LANES = 128
VMEM_LIMIT = 56 * 1024 * 1024


def _cparams(*sem):
    return pltpu.CompilerParams(dimension_semantics=sem, vmem_limit_bytes=VMEM_LIMIT)


def _rms(x, g):
    return x * lax.rsqrt(jnp.mean(x * x, axis=-1, keepdims=True) + EPS) * g


def _mod_kernel(c_ref, w_ref, b_ref, o_ref):
    c = c_ref[...]
    s = c * jax.nn.sigmoid(c)
    o_ref[...] = jnp.dot(s, w_ref[...], preferred_element_type=F32, precision=HIGHEST) + b_ref[...]


def _modulation(c_rows, w_mod, b_mod):
    depth, d, n6 = w_mod.shape
    rows = c_rows.shape[0]
    tn = n6 // 4
    return pl.pallas_call(
        _mod_kernel,
        grid=(depth, n6 // tn),
        in_specs=[pl.BlockSpec((rows, d), lambda l, j: (0, 0)),
                  pl.BlockSpec((None, d, tn), lambda l, j: (l, 0, j)),
                  pl.BlockSpec((None, 1, tn), lambda l, j: (l, 0, j))],
        out_specs=pl.BlockSpec((None, rows, tn), lambda l, j: (l, 0, j)),
        out_shape=jax.ShapeDtypeStruct((depth, rows, n6), F32),
        compiler_params=_cparams("parallel", "parallel"),
        name="modulation",
    )(c_rows, w_mod, b_mod.reshape(depth, 1, n6))


def _mod_specs(layer, part, d, nb):
    return [pl.BlockSpec((None, nb, 1, d), lambda ct, t: (layer, 0, 0, part)),
            pl.BlockSpec((None, None, 1, d), lambda ct, t: (layer, nb, 0, part))]


def _mod_pick(ct, batch_ref, ctx_ref):
    return jnp.where(ct == 0, ctx_ref[...][None], batch_ref[...])


def _mod_spec_lat(layer, part, d):
    return pl.BlockSpec((None, None, 1, d), lambda b, i: (layer, b, 0, part))


def _view4(a):
    nb, n, w = a.shape
    return a.reshape(nb, n // S5_CHUNK, S5_CHUNK, w)


def _tile_slice(a4, nb, ct, t, cc, off=0):
    return a4.at[pl.ds(0, nb), pl.ds((ct - off) * cc, cc), t, :]


class _TileStreams:
    def __init__(self, ins, outs, isem, osem, cc, in_off, nsteps):
        self.ins, self.outs, self.isem, self.osem = ins, outs, isem, osem
        self.cc, self.in_off, self.nsteps = cc, in_off, nsteps
        self.nb = ins[0][1].shape[1]
        self.ct = pl.program_id(0) + in_off
        self.t = pl.program_id(1)
        self.q = pl.num_programs(1)
        self.s = pl.program_id(0) * self.q + self.t
        self.slot = lax.rem(self.s, 2)

    def _start_inputs(self, ct, t, sl):
        for srcs, buf in self.ins:
            if len(srcs) == 2:
                @pl.when(ct == 0)
                def _():
                    pltpu.make_async_copy(_tile_slice(srcs[0], self.nb, 0, t, self.cc), buf.at[sl], self.isem.at[sl]).start()

                @pl.when(ct > 0)
                def _():
                    pltpu.make_async_copy(_tile_slice(srcs[1], self.nb, ct, t, self.cc, 1), buf.at[sl],
                                          self.isem.at[sl]).start()
            else:
                pltpu.make_async_copy(_tile_slice(srcs[0], self.nb, ct, t, self.cc), buf.at[sl], self.isem.at[sl]).start()

    def _wait_outputs(self, sl):
        for arr, buf, _ in self.outs:
            pltpu.make_async_copy(buf.at[sl], _tile_slice(arr, self.nb, 0, 0, self.cc), self.osem.at[sl]).wait()

    def begin(self):
        ct, t, s, slot = self.ct, self.t, self.s, self.slot

        @pl.when(s == 0)
        def _():
            self._start_inputs(ct, t, 0)

        @pl.when(s + 1 < self.nsteps)
        def _():
            wrap = t == self.q - 1
            self._start_inputs(jnp.where(wrap, ct + 1, ct), jnp.where(wrap, 0, t + 1), 1 - slot)

        for srcs, buf in self.ins:
            pltpu.make_async_copy(_tile_slice(srcs[0], self.nb, 0, 0, self.cc), buf.at[slot], self.isem.at[slot]).wait()

        @pl.when(s >= 2)
        def _():
            self._wait_outputs(slot)

    def end(self):
        for arr, buf, off in self.outs:
            pltpu.make_async_copy(buf.at[self.slot], _tile_slice(arr, self.nb, self.ct, self.t, self.cc, off),
                                  self.osem.at[self.slot]).start()

        @pl.when(self.s == self.nsteps - 1)
        def _():
            if self.nsteps >= 2:
                self._wait_outputs(1 - self.slot)
            self._wait_outputs(self.slot)


def _inproj_kernel(has_y, nsrc, ngrp, cc, nsteps, *refs):
    srcs, refs = list(refs[:nsrc]), refs[nsrc:]
    if has_y:
        y_hbm, gab_ref, gac_ref = refs[:3]
        refs = refs[3:]
    shb_ref, shc_ref, scb_ref, scc_ref, g_ref, wu_ref, wrq_ref = refs[:7]
    refs = refs[7:]
    if has_y:
        xo_hbm, ut_ref, r_hbm, q_hbm, xbuf, rbuf, qbuf, isem, osem, ybuf, xobuf = refs
        ins = [(srcs, xbuf), ([y_hbm], ybuf)]
        outs = [(xo_hbm, xobuf, 0), (r_hbm, rbuf, 0), (q_hbm, qbuf, 0)]
    else:
        ut_ref, r_hbm, q_hbm, xbuf, rbuf, qbuf, isem, osem = refs
        ins = [(srcs, xbuf)]
        outs = [(r_hbm, rbuf, 0), (q_hbm, qbuf, 0)]
    st = _TileStreams(ins, outs, isem, osem, cc, 0, nsteps)
    st.begin()
    ct, slot = st.ct, st.slot
    x = xbuf[slot]
    if has_y:
        x = x + _mod_pick(ct, gab_ref, gac_ref) * ybuf[slot]
        xobuf[slot] = x
    nb, _, d = x.shape
    d_lru = rbuf.shape[-1]
    h = _rms(x, g_ref[...]) * (1.0 + _mod_pick(ct, scb_ref, scc_ref)) + _mod_pick(ct, shb_ref, shc_ref)
    hb = h.reshape(nb * cc, d).astype(BF16)
    ut = lax.dot_general(wu_ref[...], hb, (((1,), (1,)), ((), ())), preferred_element_type=F32)
    ut_ref[...] = ut.reshape(ngrp, ut.shape[0] // ngrp, nb * cc)
    p = jnp.dot(hb, wrq_ref[...], preferred_element_type=F32)
    rbuf[slot] = p[:, :d_lru].reshape(nb, cc, d_lru)
    qbuf[slot] = p[:, d_lru:].reshape(nb, cc, d_lru)
    st.end()


def _inproj(xs, y, mods, layer, g1, w_in, ngrp, d_s5, d_lru, cc):
    q = S5_CHUNK
    nb, d = xs[0].shape[0], xs[0].shape[-1]
    n = sum(a.shape[1] for a in xs)
    nchunk = n // q
    has_y = y is not None
    nh = d_s5 // ngrp
    nsteps = (nchunk // cc) * q
    anyspec = pl.BlockSpec(memory_space=pl.ANY)
    in_specs = [anyspec] * len(xs)
    args = [_view4(a) for a in xs]
    if has_y:
        in_specs += [anyspec] + _mod_specs(layer - 1, 5, d, nb)
        args += [_view4(y), mods, mods]
    wu = w_in[:, :d_s5].T.astype(BF16)
    wrq = w_in[:, d_s5:].astype(BF16)
    full = lambda a: pl.BlockSpec(a.shape, lambda ct, t: (0,) * a.ndim)
    in_specs += _mod_specs(layer, 0, d, nb) + _mod_specs(layer, 1, d, nb) + [
        pl.BlockSpec((1, d), lambda ct, t: (0, 0)), full(wu), full(wrq)]
    args += [mods, mods, mods, mods, g1.reshape(1, d), wu, wrq]
    out_specs = [pl.BlockSpec((ngrp, nh, nb * cc), lambda ct, t: (0, t, ct)), anyspec, anyspec]
    out_shape = [jax.ShapeDtypeStruct((ngrp, q * nh, nb * nchunk), F32),
                 jax.ShapeDtypeStruct((nb, nchunk, q, d_lru), F32),
                 jax.ShapeDtypeStruct((nb, nchunk, q, d_lru), F32)]
    buf = lambda w: pltpu.VMEM((2, nb, cc, w), F32)
    scratch = [buf(d), buf(d_lru), buf(d_lru), pltpu.SemaphoreType.DMA((2,)), pltpu.SemaphoreType.DMA((2,))]
    if has_y:
        out_specs = [anyspec] + out_specs
        out_shape = [jax.ShapeDtypeStruct((nb, nchunk, q, d), F32)] + out_shape
        scratch += [buf(d), buf(d)]
    res = pl.pallas_call(
        functools.partial(_inproj_kernel, has_y, len(xs), ngrp, cc, nsteps),
        grid=(nchunk // cc, q), in_specs=in_specs, out_specs=out_specs, out_shape=out_shape,
        scratch_shapes=scratch,
        compiler_params=_cparams("arbitrary", "arbitrary"),
        name="inproj",
    )(*args)
    if has_y:
        xs = [res[0].reshape(nb, n, d)]
        res = res[1:]
    return [xs, res[0], res[1].reshape(nb, n, d_lru), res[2].reshape(nb, n, d_lru)]


def _s5_weights(a_re, a_im, log_dt, b_re, b_im, c_re, c_im):
    q = S5_CHUNK
    lam = lax.complex(a_re.astype(F32), a_im.astype(F32))
    dt = jnp.exp(log_dt.astype(F32))[..., None]
    a_bar = jnp.exp(lam * dt)
    bmat = lax.complex(b_re.astype(F32), b_im.astype(F32))
    b_bar = ((a_bar - 1) / lam)[..., None] * bmat
    cmat = lax.complex(c_re.astype(F32), c_im.astype(F32))
    k = jnp.arange(q + 1, dtype=F32)
    apow = jnp.exp((lam * dt)[..., None] * k)
    ngrp, nst = a_re.shape[1], a_re.shape[2]
    nh = b_re.shape[-1]

    kern = jnp.einsum('dgop,dgpk,dgpi->dgkoi', cmat, apow[..., :q], b_bar, precision=HIGHEST).real
    sig = jnp.arange(q)[:, None]
    tau = jnp.arange(q)[None, :]
    lag_f = jnp.clip(tau - sig, 0, q - 1)
    lag_b = jnp.clip(sig - tau, 0, q - 1)
    tf = jnp.where((sig <= tau)[None, :, :, None, None], kern[0][:, lag_f], 0.0)
    tb = jnp.where((sig >= tau)[None, :, :, None, None], kern[1][:, lag_b], 0.0)
    toep = (tf + tb).transpose(0, 1, 4, 2, 3).reshape(ngrp, q * nh, q * nh)

    win_f = jnp.einsum('gps,gpi->gsip', apow[0][..., :q][..., ::-1], b_bar[0])
    win_b = jnp.einsum('gps,gpi->gsip', apow[1][..., :q], b_bar[1])
    win = jnp.concatenate([win_f.real, win_b.real, win_f.imag, win_b.imag], axis=-1)
    win = win.reshape(ngrp, q * nh, 4 * nst)

    cf = jnp.einsum('gop,gpt->gpto', cmat[0], apow[0][..., 1:])
    cb = jnp.einsum('gop,gpt->gpto', cmat[1], apow[1][..., 1:][..., ::-1])
    z = jnp.zeros_like(cf.real)
    wof = jnp.concatenate([cf.real, z, -cf.imag, z], axis=1).reshape(ngrp, 4 * nst, q * nh)
    wob = jnp.concatenate([z, cb.real, z, -cb.imag], axis=1).reshape(ngrp, 4 * nst, q * nh)

    aq = apow[..., q]
    a16 = jnp.stack([jnp.concatenate([aq[0].real, aq[1].real], -1),
                     jnp.concatenate([aq[0].imag, aq[1].imag], -1)], axis=1)
    w1 = jnp.concatenate([toep, win], axis=-1)
    return w1.astype(BF16), wof.astype(BF16), wob.astype(BF16), a16.astype(F32)


def _s5_kernel(nb, cc, nc_ctx, nc_lat, u_ref, w1_ref, wof_ref, wob_ref, a_ref, y_ref,
               yrow_ref, sre_ref, sim_ref, fre_ref, fim_ref, bre_ref, bim_ref):
    qh, m = u_ref.shape
    ns2 = a_ref.shape[-1]
    mt = nb * cc
    w1 = w1_ref[...]

    def intra(i, c):
        r0 = pl.multiple_of(i * mt, mt)
        u = u_ref[:, pl.ds(r0, mt)].T.astype(BF16)
        res = jnp.dot(u, w1, preferred_element_type=F32)
        yrow_ref[pl.ds(r0, mt), :] = res[:, :qh]
        sre_ref[pl.ds(r0, mt), :] = res[:, qh:qh + ns2]
        sim_ref[pl.ds(r0, mt), :] = res[:, qh + ns2:]
        return c

    lax.fori_loop(0, m // mt, intra, 0)

    are = jnp.broadcast_to(a_ref[0:1, :], (nb, ns2))
    aim = jnp.broadcast_to(a_ref[1:2, :], (nb, ns2))
    fwd_lane = lax.broadcasted_iota(jnp.int32, (nb, ns2), 1) < ns2 // 2

    def rows(c):
        tile = lax.div(c, cc)
        return pl.ds(tile * mt + (c - tile * cc), nb, stride=cc)

    def scan(base, n, carry):
        def step(k, hc):
            hre, him = hc
            rf = rows(base + k)
            rb = rows(base + n - 1 - k)
            fre_ref[rf, :] = hre
            fim_ref[rf, :] = him
            bre_ref[rb, :] = hre
            bim_ref[rb, :] = him
            sre = jnp.where(fwd_lane, sre_ref[rf, :], sre_ref[rb, :])
            sim = jnp.where(fwd_lane, sim_ref[rf, :], sim_ref[rb, :])
            return (are * hre - aim * him + sre, are * him + aim * hre + sim)
        return lax.fori_loop(0, n, step, carry)

    zero = jnp.zeros((nb, ns2), F32)
    carry = scan(0, nc_ctx, (zero, zero))
    scan(nc_ctx, nc_lat, carry)

    wof = wof_ref[...]
    wob = wob_ref[...]

    def inter(i, c):
        r0 = pl.multiple_of(i * mt, mt)
        hf = jnp.concatenate([fre_ref[pl.ds(r0, mt), :], fim_ref[pl.ds(r0, mt), :]], axis=1).astype(BF16)
        hb = jnp.concatenate([bre_ref[pl.ds(r0, mt), :], bim_ref[pl.ds(r0, mt), :]], axis=1).astype(BF16)
        y = (yrow_ref[pl.ds(r0, mt), :] + jnp.dot(hf, wof, preferred_element_type=F32)
             + jnp.dot(hb, wob, preferred_element_type=F32))
        y_ref[:, pl.ds(r0, mt)] = y.T
        return c

    lax.fori_loop(0, m // mt, inter, 0)


def _s5_scan(u_t, w1, wof, wob, a16, nb, cc, nc_ctx, nc_lat):
    ngrp, qh, m = u_t.shape
    ns2 = a16.shape[-1]
    grp = lambda a: pl.BlockSpec((None,) + a.shape[1:], lambda g: (g, 0, 0))
    return pl.pallas_call(
        functools.partial(_s5_kernel, nb, cc, nc_ctx, nc_lat),
        grid=(ngrp,),
        in_specs=[grp(u_t), grp(w1), grp(wof), grp(wob), grp(a16)],
        out_specs=grp(u_t),
        out_shape=jax.ShapeDtypeStruct(u_t.shape, F32),
        scratch_shapes=[pltpu.VMEM((m, qh), F32)] + [pltpu.VMEM((m, ns2), F32)] * 6,
        compiler_params=_cparams("parallel"),
        name="s5_chunked",
    )(u_t, w1, wof, wob, a16)


def _lru_chunk(s, reverse, nc_ctx, nc):
    if not reverse:
        return s
    return jnp.where(s < nc_ctx, nc_ctx - 1 - s, nc - 1 - (s - nc_ctx))


def _lru_kernel(reverse, combine, nc_ctx, nc, rows, *refs):
    if combine:
        (r_hbm, q_hbm, ho_hbm, cw_ref, cb_ref, wa_ref, ba_ref, wx_ref, bx_ref, cn_ref, o_hbm,
         rbuf, obuf, h_ref, a_s, b_s, isem, osem, qbuf, hbuf) = refs
    else:
        (r_hbm, cw_ref, cb_ref, wa_ref, ba_ref, wx_ref, bx_ref, cn_ref, o_hbm,
         rbuf, obuf, h_ref, a_s, b_s, isem, osem) = refs
    _, jt, nb, ch = obuf.shape
    gw = r_hbm.shape[2]
    cblk = (nc_ctx * jt) // gw
    ncol = jt // rows
    s = pl.program_id(0)
    slot = lax.rem(s, 2)

    def chunk_of(step):
        return _lru_chunk(jnp.clip(step, 0, nc - 1), reverse, nc_ctx, nc)

    def main_copies(hbm, buf, row0, c, sl, sem, fn):
        @pl.when(c < nc_ctx)
        def _():
            j0 = c * jt
            for b in range(nb):
                fn(hbm.at[b, j0 // gw, pl.ds(j0 % gw, jt), :], buf.at[sl, pl.ds(row0, jt), b, :], sem.at[sl])

        @pl.when(c >= nc_ctx)
        def _():
            w0 = (c - nc_ctx) * ncol
            for b in range(nb):
                for k in range(ncol):
                    fn(hbm.at[b, pl.ds(cblk, rows), w0 + k, :],
                       buf.at[sl, pl.ds(row0 + k * rows, rows), b, :], sem.at[sl])

    def halo_copies(c, sl, fn):
        first = jnp.logical_or(c == 0, c == nc_ctx)
        last = jnp.logical_or(c == nc_ctx - 1, c == nc - 1)
        in_ctx = c < nc_ctx

        @pl.when(jnp.logical_and(in_ctx, jnp.logical_not(first)))
        def _():
            j = c * jt - CONV_LEFT
            for b in range(nb):
                fn(r_hbm.at[b, j // gw, pl.ds(j % gw, CONV_LEFT), :], rbuf.at[sl, pl.ds(0, CONV_LEFT), b, :],
                   isem.at[sl])

        @pl.when(jnp.logical_and(in_ctx, jnp.logical_not(last)))
        def _():
            j = c * jt + jt
            for b in range(nb):
                fn(r_hbm.at[b, j // gw, pl.ds(j % gw, 1), :], rbuf.at[sl, pl.ds(jt + CONV_LEFT, 1), b, :],
                   isem.at[sl])

        @pl.when(jnp.logical_and(jnp.logical_not(in_ctx), jnp.logical_not(first)))
        def _():
            w = (c - nc_ctx) * ncol - 1
            for b in range(nb):
                fn(r_hbm.at[b, pl.ds(cblk + rows - CONV_LEFT, CONV_LEFT), w, :],
                   rbuf.at[sl, pl.ds(0, CONV_LEFT), b, :], isem.at[sl])

        @pl.when(jnp.logical_and(jnp.logical_not(in_ctx), jnp.logical_not(last)))
        def _():
            w = (c - nc_ctx) * ncol + ncol
            for b in range(nb):
                fn(r_hbm.at[b, pl.ds(cblk, 1), w, :], rbuf.at[sl, pl.ds(jt + CONV_LEFT, 1), b, :], isem.at[sl])

    def loads(c, sl, fn):
        main_copies(r_hbm, rbuf, CONV_LEFT, c, sl, isem, fn)
        halo_copies(c, sl, fn)
        if combine:
            main_copies(q_hbm, qbuf, 0, c, sl, isem, fn)
            main_copies(ho_hbm, hbuf, 0, c, sl, isem, fn)

    def stores(c, sl, fn):
        main_copies(o_hbm, obuf, 0, c, sl, osem, lambda hbm, buf, sem: fn(buf, hbm, sem))

    start = lambda src, dst, sem: pltpu.make_async_copy(src, dst, sem).start()
    wait = lambda src, dst, sem: pltpu.make_async_copy(src, dst, sem).wait()

    c = chunk_of(s)
    first = jnp.logical_or(c == 0, c == nc_ctx)
    last = jnp.logical_or(c == nc_ctx - 1, c == nc - 1)

    @pl.when(s == 0)
    def _():
        h_ref[...] = jnp.zeros_like(h_ref)
        rbuf[...] = jnp.zeros_like(rbuf)
        loads(c, 0, start)

    @pl.when(s + 1 < nc)
    def _():
        loads(chunk_of(s + 1), 1 - slot, start)

    loads(c, slot, wait)

    @pl.when(s >= 2)
    def _():
        stores(chunk_of(s - 2), slot, wait)

    rb = rbuf[slot]
    prev = jnp.where(first, 0.0, rb[0:CONV_LEFT])
    nxt = jnp.where(last, 0.0, rb[jt + CONV_LEFT:jt + CONV_LEFT + 1])
    xe = jnp.concatenate([prev, rb[CONV_LEFT:jt + CONV_LEFT], nxt], axis=0)
    cw = cw_ref[...]
    xc = cb_ref[...].reshape(1, 1, ch)
    for k in range(cw.shape[0]):
        xc = xc + cw[k:k + 1, :].reshape(1, 1, ch) * xe[k:k + jt]
    xc2 = xc.reshape(jt * nb, ch)

    xb = xc2.astype(BF16)
    nblk, wb = wa_ref.shape[0], wa_ref.shape[1]
    za = jnp.concatenate([jnp.dot(xb[:, i * wb:(i + 1) * wb], wa_ref[i], preferred_element_type=F32)
                          for i in range(nblk)], axis=-1)
    zx = jnp.concatenate([jnp.dot(xb[:, i * wb:(i + 1) * wb], wx_ref[i], preferred_element_type=F32)
                          for i in range(nblk)], axis=-1)
    rg = jax.nn.sigmoid(za + ba_ref[...])
    ig = jax.nn.sigmoid(zx + bx_ref[...])
    log_a = cn_ref[...] * rg
    a = jnp.exp(log_a)
    bb = jnp.sqrt(1.0 - a * a) * (ig * xc2)
    a_s[...] = a.reshape(jt, nb, ch)
    b_s[...] = bb.reshape(jt, nb, ch)

    def step(t, h):
        tt = jt - 1 - t if reverse else t
        h = a_s[tt] * h + b_s[tt]
        if combine:
            obuf[slot, tt] = (h + hbuf[slot, tt]) * jax.nn.gelu(qbuf[slot, tt])
        else:
            obuf[slot, tt] = h
        return h

    h_ref[...] = lax.fori_loop(0, jt, step, h_ref[...])

    stores(c, slot, start)

    @pl.when(s == nc - 1)
    def _():
        if nc >= 2:
            stores(chunk_of(s - 1), 1 - slot, wait)
        stores(c, slot, wait)


def _lru_pass(reverse, r4, q4, h_other4, conv_w, conv_b, wa, ba, wx, bx, cneg, n_ctx, rows):
    nb, nblk, gw, ch = r4.shape
    jt = LRU_STEPS
    nc, nc_ctx = (nblk * gw) // jt, n_ctx // jt
    combine = h_other4 is not None
    assert gw % jt == 0 and n_ctx % gw == 0 and jt % rows == 0 and rows >= CONV_LEFT
    anyspec = pl.BlockSpec(memory_space=pl.ANY)
    full = lambda a: pl.BlockSpec(a.shape, lambda s: (0,) * a.ndim)
    consts = [conv_w, conv_b, wa, ba, wx, bx, cneg]
    big = [r4, q4, h_other4] if combine else [r4]
    buf = lambda n: pltpu.VMEM((2, n, nb, ch), F32)
    scratch = [buf(jt + CONV_LEFT + 1), buf(jt), pltpu.VMEM((nb, ch), F32),
               pltpu.VMEM((jt, nb, ch), F32), pltpu.VMEM((jt, nb, ch), F32),
               pltpu.SemaphoreType.DMA((2,)), pltpu.SemaphoreType.DMA((2,))]
    if combine:
        scratch += [buf(jt), buf(jt)]
    return pl.pallas_call(
        functools.partial(_lru_kernel, reverse, combine, nc_ctx, nc, rows),
        grid=(nc,), in_specs=[anyspec] * len(big) + [full(a) for a in consts], out_specs=anyspec,
        out_shape=jax.ShapeDtypeStruct(r4.shape, F32),
        scratch_shapes=scratch,
        compiler_params=_cparams("arbitrary"),
        name="lru_bwd" if reverse else "lru_fwd",
    )(*big, *consts)


def _block_diag_halves(w):
    nh, hd, _ = w.shape
    half = nh // 2
    eye = jnp.eye(half, dtype=w.dtype)
    wh = w.reshape(2, half, hd, hd)
    bd = jnp.einsum('bhij,hk->bhikj', wh, eye).reshape(2, half * hd, half * hd)
    return bd.astype(BF16)


def _post_kernel(n_groups, n_per, nsrc, off, cc, nsteps, *refs):
    srcs, refs = list(refs[:nsrc]), refs[nsrc:]
    (lr_hbm, yt_ref, ut_ref, gab_ref, gac_ref, shb_ref, shc_ref, scb_ref, scc_ref, d_ref, wg_ref, bg_ref,
     wo_ref, g2_ref, wr_ref, br_ref, x1_hbm, hm_hbm, meta_ref, xbuf, lbuf, x1buf, hmbuf, isem, osem) = refs
    st = _TileStreams([(srcs, xbuf), ([lr_hbm], lbuf)], [(x1_hbm, x1buf, off), (hm_hbm, hmbuf, off)],
                      isem, osem, cc, off, nsteps)
    st.begin()
    ct, slot = st.ct, st.slot
    x_in = xbuf[slot]
    nb, _, d = x_in.shape
    tm = nb * cc
    d_s5 = d_ref.shape[0]
    ys = yt_ref[...].reshape(d_s5, tm) + d_ref[...] * ut_ref[...].reshape(d_s5, tm)
    z = jax.nn.gelu(ys)
    s5t = z * jax.nn.sigmoid(jnp.dot(wg_ref[...], z.astype(BF16), preferred_element_type=F32) + bg_ref[...])
    mix = jnp.concatenate([s5t.T.astype(BF16), lbuf[slot].reshape(tm, -1).astype(BF16)], axis=-1)
    proj = jnp.dot(mix, wo_ref[...], preferred_element_type=F32).reshape(nb, cc, d)
    x1 = x_in + _mod_pick(ct, gab_ref, gac_ref) * proj
    x1buf[slot] = x1
    hm3 = _rms(x1, g2_ref[...]) * (1.0 + _mod_pick(ct, scb_ref, scc_ref)) + _mod_pick(ct, shb_ref, shc_ref)
    hmbuf[slot] = hm3
    hm = hm3.reshape(tm, d)

    hi = hm.astype(BF16)
    lo = (hm - hi.astype(F32)).astype(BF16)
    two = jnp.dot(jnp.concatenate([hi, lo], axis=-1), wr_ref[...], preferred_element_type=F32)
    logits = two[:, :LANES] + two[:, LANES:] + br_ref[...]

    lane = lax.broadcasted_iota(jnp.int32, (tm, LANES), 1).astype(F32)
    neg = jnp.float32(-jnp.inf)
    big = jnp.float32(LANES)
    lg = jnp.where(lane < n_groups, logits, neg)
    mg = jnp.max(lg, axis=-1, keepdims=True)
    g_p = 1.0 / jnp.sum(jnp.exp(lg - mg), axis=-1, keepdims=True)
    gidx = jnp.min(jnp.where(lg == mg, lane, big), axis=-1, keepdims=True)
    e0 = n_groups + n_per * gidx
    le = jnp.where(jnp.logical_and(lane >= e0, lane < e0 + n_per), logits, neg)
    m1 = jnp.max(le, axis=-1, keepdims=True)
    i1 = jnp.min(jnp.where(le == m1, lane, big), axis=-1, keepdims=True)
    le2 = jnp.where(lane == i1, neg, le)
    m2 = jnp.max(le2, axis=-1, keepdims=True)
    i2 = jnp.min(jnp.where(le2 == m2, lane, big), axis=-1, keepdims=True)
    r2 = jnp.exp(m2 - m1)
    w1 = g_p / (1.0 + r2)
    w2 = g_p * r2 / (1.0 + r2)
    j1 = i1 - e0
    j2 = i2 - e0
    jlo = jnp.minimum(j1, j2)
    jhi = jnp.maximum(j1, j2)
    wlo = jnp.where(j1 < j2, w1, w2)
    whi = jnp.where(j1 < j2, w2, w1)
    pair = jlo * (2 * n_per - 1 - jlo) * 0.5 + (jhi - jlo - 1.0)
    bucket = gidx * (n_per * (n_per - 1) // 2) + pair
    meta = jnp.where(lane == 0, bucket, jnp.where(lane == 1, wlo, jnp.where(lane == 2, whi, 0.0)))
    meta_ref[...] = meta.T[0:8, :]
    st.end()


def _post(lat_only, yt, ut, lr, xs, mods, layer, s5_d, w_glu, b_glu, w_out_bf, g2, wr2, br, n_groups, n_per, cc):
    q = S5_CHUNK
    nb, d = xs[0].shape[0], xs[0].shape[-1]
    n = sum(a.shape[1] for a in xs)
    nchunk = n // q
    ngrp, qh, _ = yt.shape
    nh = qh // q
    d_s5 = ngrp * nh
    d_lru = lr.shape[-1]
    off = 1 if lat_only else 0
    assert not (lat_only and len(xs) == 2)
    nct = nchunk // cc - off
    tm = nb * cc
    anyspec = pl.BlockSpec(memory_space=pl.ANY)
    tspec = pl.BlockSpec((ngrp, nh, tm), lambda ct, t: (0, t, ct + off))
    full = lambda a: pl.BlockSpec(a.shape, lambda ct, t: (0,) * a.ndim)
    consts = [s5_d.reshape(d_s5, 1), w_glu.T.astype(BF16), b_glu.reshape(d_s5, 1), w_out_bf, g2.reshape(1, d),
              wr2, br]
    buf = lambda w: pltpu.VMEM((2, nb, cc, w), F32)
    x1, hm, meta = pl.pallas_call(
        functools.partial(_post_kernel, n_groups, n_per, len(xs), off, cc, nct * q),
        grid=(nct, q),
        in_specs=([anyspec] * (len(xs) + 1) + [tspec, tspec]
                  + _mod_specs(layer, 2, d, nb) + _mod_specs(layer, 3, d, nb) + _mod_specs(layer, 4, d, nb)
                  + [full(a) for a in consts]),
        out_specs=[anyspec, anyspec, pl.BlockSpec((None, None, 8, tm), lambda ct, t: (t, ct, 0, 0))],
        out_shape=[jax.ShapeDtypeStruct((nb, nct * cc, q, d), F32), jax.ShapeDtypeStruct((nb, nct * cc, q, d), F32),
                   jax.ShapeDtypeStruct((q, nct, 8, tm), F32)],
        scratch_shapes=[buf(d), buf(d_lru), buf(d), buf(d),
                        pltpu.SemaphoreType.DMA((2,)), pltpu.SemaphoreType.DMA((2,))],
        compiler_params=_cparams("arbitrary", "arbitrary"),
        name="post_mixer",
    )(*[_view4(a) for a in xs], _view4(lr), yt, ut, mods, mods, mods, mods, mods, mods, *consts)
    n_out = nct * cc * q
    return x1.reshape(nb, n_out, d), hm.reshape(nb, n_out, d), meta


def _moe_kernel(n_rows, nt, tb_ref, src0_ref, srcn_ref, dst_ref, hm_hbm, wt_ref, w1a, w3a, w2a, w1b, w3b, w2b,
                y_hbm, xbuf, ybuf, gsem, ssem):
    i = pl.program_id(0)
    slot = lax.rem(i, 2)
    tmr = xbuf.shape[1]

    def valid(j):
        return tb_ref[2, jnp.clip(j, 0, nt - 1)] > 0

    def start_gathers(ids_ref, s):
        for r in range(tmr):
            pltpu.make_async_copy(hm_hbm.at[pl.ds(ids_ref[0, r], 1), :], xbuf.at[s, pl.ds(r, 1), :],
                                  gsem.at[s]).start(priority=r % 2)

    def start_scatters(s):
        for r in range(tmr):
            pltpu.make_async_copy(ybuf.at[s, pl.ds(r, 1), :], y_hbm.at[pl.ds(dst_ref[0, r], 1), :],
                                  ssem.at[s]).start(priority=r % 2)

    def wait_gathers(s):
        pltpu.make_async_copy(hm_hbm.at[pl.ds(0, tmr), :], xbuf.at[s], gsem.at[s]).wait()

    def wait_scatters(s):
        pltpu.make_async_copy(ybuf.at[s], y_hbm.at[pl.ds(0, tmr), :], ssem.at[s]).wait()

    @pl.when(i == 0)
    def _():
        xbuf[...] = jnp.zeros_like(xbuf)
        ybuf[...] = jnp.zeros_like(ybuf)
        for s in range((y_hbm.shape[0] - n_rows) // tmr):
            cp = pltpu.make_async_copy(ybuf.at[0], y_hbm.at[pl.ds(n_rows + s * tmr, tmr), :], ssem.at[0])
            cp.start()
            cp.wait()
        start_gathers(src0_ref, 0)

    @pl.when(jnp.logical_and(i + 1 < nt, valid(i + 1)))
    def _():
        start_gathers(srcn_ref, 1 - slot)

    @pl.when(jnp.logical_and(i >= 2, valid(i - 2)))
    def _():
        wait_scatters(slot)

    @pl.when(valid(i))
    def _():
        wait_gathers(slot)
        xb = xbuf[slot].astype(BF16)
        wt = wt_ref[...]

        def expert(w1, w3, w2, gate):
            h1 = jnp.dot(xb, w1[...], preferred_element_type=F32)
            h3 = jnp.dot(xb, w3[...], preferred_element_type=F32)
            hid = (h1 * jax.nn.sigmoid(h1)) * h3
            return gate * jnp.dot(hid.astype(BF16), w2[...], preferred_element_type=F32)

        half = LANES // 2
        ybuf[slot] = expert(w1a, w3a, w2a, wt[:, 0:1]) + expert(w1b, w3b, w2b, wt[:, half:half + 1])
        start_scatters(slot)

    @pl.when(i == nt - 1)
    def _():
        @pl.when(jnp.logical_and(nt >= 2, valid(i - 1)))
        def _():
            wait_scatters(1 - slot)

        @pl.when(valid(i))
        def _():
            wait_scatters(slot)


def _moe(hm2, bucket, wlo, whi, tok_rows, w1_bf, w3_bf, w2_bf, n_per, tmr, spare_rows):
    n_rows, d = hm2.shape
    assert spare_rows >= 2 * tmr and spare_rows % tmr == 0
    t = bucket.shape[0]
    npairs = n_per * (n_per - 1) // 2
    nbuck = (w1_bf.shape[0] // n_per) * npairs
    ntiles = t // tmr + nbuck

    _, s_tok, s_lo, s_hi = lax.sort((bucket, tok_rows, wlo, whi), num_keys=1, is_stable=True)
    counts = jnp.sum((bucket[None, :] == jnp.arange(nbuck, dtype=jnp.int32)[:, None]).astype(jnp.int32), axis=1)
    padded = ((counts + tmr - 1) // tmr) * tmr
    pend = jnp.cumsum(padded)
    cend = jnp.cumsum(counts)
    shift = (pend - padded) - (cend - counts)
    tile_start = jnp.arange(ntiles, dtype=jnp.int32) * tmr
    valid = (tile_start < pend[-1]).astype(jnp.int32)
    tb = jnp.sum((tile_start[:, None] >= pend[None, :]).astype(jnp.int32), axis=1)
    tb = jnp.minimum(tb, jnp.sum((pend < pend[-1]).astype(jnp.int32)))
    tb = jnp.minimum(tb, nbuck - 1)
    pos = (tile_start[:, None] + jnp.arange(tmr, dtype=jnp.int32)[None, :]) - shift[tb][:, None]
    real = jnp.logical_and(pos < cend[tb][:, None], valid[:, None] > 0)
    pos = jnp.clip(pos, 0, t - 1)
    src = jnp.where(real, s_tok[pos], 0)
    spare = n_rows + (jnp.arange(ntiles, dtype=jnp.int32) % 2)[:, None] * tmr + jnp.arange(tmr, dtype=jnp.int32)
    dst = jnp.where(real, s_tok[pos], spare)
    g_lo = jnp.where(real, s_lo[pos], 0.0).reshape(-1, 1)
    g_hi = jnp.where(real, s_hi[pos], 0.0).reshape(-1, 1)
    half = LANES // 2
    wts = jnp.concatenate([jnp.broadcast_to(g_lo, (ntiles * tmr, half)),
                           jnp.broadcast_to(g_hi, (ntiles * tmr, half))], axis=1)
    grp, pr = tb // npairs, tb % npairs
    pairs = [(i, j) for i in range(n_per) for j in range(i + 1, n_per)]
    plo = jnp.array([p[0] for p in pairs], jnp.int32)[pr]
    phi = jnp.array([p[1] for p in pairs], jnp.int32)[pr]
    tinfo = jnp.stack([grp * n_per + plo, grp * n_per + phi, valid])

    wspec = lambda a, row: pl.BlockSpec((None,) + a.shape[1:], lambda i, tb_: (tb_[row, i], 0, 0))
    ids = lambda f: pl.BlockSpec((None, 1, tmr), lambda i, tb_: (f(i), 0, 0), memory_space=pltpu.SMEM)
    grid_spec = pltpu.PrefetchScalarGridSpec(
        num_scalar_prefetch=1,
        grid=(ntiles,),
        in_specs=[ids(lambda i: 0), ids(lambda i: jnp.minimum(i + 1, ntiles - 1)), ids(lambda i: i),
                  pl.BlockSpec(memory_space=pl.ANY),
                  pl.BlockSpec((tmr, LANES), lambda i, tb_: (i, 0)),
                  wspec(w1_bf, 0), wspec(w3_bf, 0), wspec(w2_bf, 0),
                  wspec(w1_bf, 1), wspec(w3_bf, 1), wspec(w2_bf, 1)],
        out_specs=pl.BlockSpec(memory_space=pl.ANY),
        scratch_shapes=[pltpu.VMEM((2, tmr, d), F32), pltpu.VMEM((2, tmr, d), F32),
                        pltpu.SemaphoreType.DMA((2,)), pltpu.SemaphoreType.DMA((2,))],
    )
    src3 = src.reshape(ntiles, 1, tmr)
    return pl.pallas_call(
        functools.partial(_moe_kernel, n_rows, ntiles),
        grid_spec=grid_spec,
        out_shape=jax.ShapeDtypeStruct((n_rows + spare_rows, d), F32),
        compiler_params=_cparams("arbitrary"),
        name="moe_pairs",
    )(tinfo, src3, src3, dst.reshape(ntiles, 1, tmr), hm2, wts, w1_bf, w3_bf, w2_bf, w1_bf, w3_bf, w2_bf)


def _final_kernel(x_ref, y_ref, ga_ref, g_ref, o_ref):
    o_ref[...] = _rms(x_ref[...] + ga_ref[...] * y_ref[...], g_ref[...])


def _final(x1, y, mods, layer, final_g, tm):
    nb, seq, d = x1.shape
    tok = pl.BlockSpec((None, tm, d), lambda b, i: (b, i, 0))
    return pl.pallas_call(
        _final_kernel,
        grid=(nb, seq // tm),
        in_specs=[tok, pl.BlockSpec((tm, d), lambda b, i: (b * (seq // tm) + i, 0)),
                  _mod_spec_lat(layer, 5, d), pl.BlockSpec((1, d), lambda b, i: (0, 0))],
        out_specs=tok,
        out_shape=jax.ShapeDtypeStruct((nb, seq, d), F32),
        compiler_params=_cparams("parallel", "parallel"),
        name="final_norm",
    )(x1, y, mods, final_g.reshape(1, d))


def kernel(x, c, ctx, c_ctx, w_mod, b_mod, norm1_g, norm2_g, w_in, w_out, s5_a_re, s5_a_im, s5_log_dt, s5_b_re, s5_b_im, s5_c_re, s5_c_im, s5_d, s5_w_glu, s5_b_glu, lru_conv_w, lru_conv_b, lru_w_a, lru_b_a, lru_w_x, lru_b_x, lru_lam, moe_w_group, moe_b_group, moe_w_router, moe_b_router, moe_w1, moe_w3, moe_w2, final_g):
    nb, seq, d = x.shape
    n_ctx = ctx.shape[1]
    depth = w_mod.shape[0]
    n = n_ctx + seq
    d_s5 = s5_d.shape[-1]
    d_lru = lru_conv_b.shape[-1]
    ngrp, nh = s5_b_re.shape[2], s5_b_re.shape[4]
    n_groups, n_per = moe_w_router.shape[1], moe_w_router.shape[3]
    rows = seq // GRID_W
    tm = n_ctx
    q = S5_CHUNK
    nc_ctx, nc_lat = n_ctx // q, seq // q
    moe_rows = 256 if nb * seq >= 16384 else 32
    assert seq % GRID_W == 0 and seq % tm == 0 and n_ctx % GRID_W == 0 and seq % LRU_STEPS == 0
    assert d_s5 == ngrp * nh and q * nh == 2 * LANES and n_groups + n_groups * n_per <= LANES

    pad = (-(nb + 1)) % 8
    c_rows = jnp.concatenate([c, c_ctx[None, :], jnp.zeros((pad, d), F32)], axis=0)
    mods = _modulation(c_rows, w_mod, b_mod).reshape(depth, nb + 1 + pad, 1, 6 * d)

    cc = nc_ctx
    xs = [ctx, x]
    y = None
    for l in range(depth):
        need_ctx = l < depth - 1
        xs, ut, r, qg = _inproj(xs, y, mods, l, norm1_g[l], w_in[l], ngrp, d_s5, d_lru, cc)

        w1, wof, wob, a16 = _s5_weights(s5_a_re[l], s5_a_im[l], s5_log_dt[l], s5_b_re[l], s5_b_im[l],
                                         s5_c_re[l], s5_c_im[l])
        yt = _s5_scan(ut, w1, wof, wob, a16, nb, cc, nc_ctx, nc_lat)

        view = lambda a: a.reshape(nb, n // GRID_W, GRID_W, d_lru)
        cneg = (-LRU_C * jax.nn.softplus(-lru_lam[l].astype(F32))).reshape(2, 1, d_lru)
        cw, cb = lru_conv_w[l].astype(F32), lru_conv_b[l].reshape(1, d_lru).astype(F32)
        gates = [(_block_diag_halves(lru_w_a[l, dd]), lru_b_a[l, dd].reshape(1, d_lru),
                  _block_diag_halves(lru_w_x[l, dd]), lru_b_x[l, dd].reshape(1, d_lru), cneg[dd]) for dd in range(2)]
        h_b = _lru_pass(True, view(r), None, None, cw, cb, *gates[1], n_ctx, rows)
        lr = _lru_pass(False, view(r), view(qg), h_b, cw, cb, *gates[0], n_ctx, rows).reshape(nb, n, d_lru)

        wr = jnp.concatenate([moe_w_group[l], moe_w_router[l].transpose(1, 0, 2).reshape(d, n_groups * n_per)], -1)
        wr = jnp.pad(wr.astype(F32), ((0, 0), (0, LANES - wr.shape[-1])))
        wr_hi = wr.astype(BF16)
        wr_lo = (wr - wr_hi.astype(F32)).astype(BF16)
        wr2 = jnp.concatenate([jnp.concatenate([wr_hi, wr_lo], -1),
                               jnp.concatenate([wr_hi, jnp.zeros_like(wr_lo)], -1)], axis=0)
        br = jnp.concatenate([moe_b_group[l], moe_b_router[l].reshape(-1)])
        br = jnp.pad(br.astype(F32), (0, LANES - br.shape[0])).reshape(1, LANES)

        x1, hm, meta = _post(not need_ctx, yt, ut, lr, xs, mods, l, s5_d[l], s5_w_glu[l], s5_b_glu[l],
                             w_out[l].astype(BF16), norm2_g[l], wr2, br, n_groups, n_per, cc)

        n_out = x1.shape[1]
        nct = n_out // (cc * q)
        meta_t = meta.reshape(q, nct, 8, nb, cc).transpose(2, 3, 1, 4, 0).reshape(8, nb * n_out)
        bucket = meta_t[0].astype(jnp.int32)
        y = _moe(hm.reshape(nb * n_out, d), bucket, meta_t[1], meta_t[2], jnp.arange(nb * n_out, dtype=jnp.int32),
                 moe_w1[l].astype(BF16), moe_w3[l].astype(BF16), moe_w2[l].astype(BF16), n_per, moe_rows, n_out)
        xs = [x1]
        if need_ctx:
            y = y.reshape(nb + 1, n_out, d)
    return _final(xs[0], y, mods, depth - 1, final_g, 4 * tm if seq % (4 * tm) == 0 else tm)
```

```python
import functools

import jax
import jax.numpy as jnp
from jax import lax
from jax.experimental import pallas as pl
from jax.experimental.pallas import tpu as pltpu

F32 = jnp.float32
BF16 = jnp.bfloat16
HIGHEST = lax.Precision.HIGHEST

EPS = 1e-6
GRID_W = 64
LRU_C = 8.0
CONV_LEFT = 2
S5_CHUNK = 16
S5_ROW_PAD = 8
LRU_STEPS = 32
LANES = 128
VMEM_LIMIT = 56 * 1024 * 1024


def _cparams(*sem):
    return pltpu.CompilerParams(dimension_semantics=sem, vmem_limit_bytes=VMEM_LIMIT)


def _rms(x, g):
    return x * lax.rsqrt(jnp.mean(x * x, axis=-1, keepdims=True) + EPS) * g


def _mod_kernel(c_ref, w_ref, b_ref, o_ref):
    c = c_ref[...]
    s = c * jax.nn.sigmoid(c)
    o_ref[...] = jnp.dot(s, w_ref[...], preferred_element_type=F32, precision=HIGHEST) + b_ref[...]


def _modulation(c_rows, w_mod, b_mod):
    depth, d, n6 = w_mod.shape
    rows = c_rows.shape[0]
    tn = n6 // 4
    return pl.pallas_call(
        _mod_kernel,
        grid=(depth, n6 // tn),
        in_specs=[pl.BlockSpec((rows, d), lambda l, j: (0, 0)),
                  pl.BlockSpec((None, d, tn), lambda l, j: (l, 0, j)),
                  pl.BlockSpec((None, 1, tn), lambda l, j: (l, 0, j))],
        out_specs=pl.BlockSpec((None, rows, tn), lambda l, j: (l, 0, j)),
        out_shape=jax.ShapeDtypeStruct((depth, rows, n6), F32),
        compiler_params=_cparams("parallel", "parallel"),
        name="modulation",
    )(c_rows, w_mod, b_mod.reshape(depth, 1, n6))


def _mod_specs(layer, part, d, nb):
    return [pl.BlockSpec((None, nb, 1, d), lambda ct, t: (layer, 0, 0, part)),
            pl.BlockSpec((None, None, 1, d), lambda ct, t: (layer, nb, 0, part))]


def _mod_pick(ct, batch_ref, ctx_ref):
    return jnp.where(ct == 0, ctx_ref[...][None], batch_ref[...])


def _mod_spec_lat(layer, part, d):
    return pl.BlockSpec((None, None, 1, d), lambda b, i: (layer, b, 0, part))


def _view4(a):
    nb, n, w = a.shape
    return a.reshape(nb, n // S5_CHUNK, S5_CHUNK, w)


def _tile_slice(a4, nb, ct, t, cc, off=0):
    return a4.at[pl.ds(0, nb), pl.ds((ct - off) * cc, cc), t, :]


class _TileStreams:
    def __init__(self, ins, outs, isem, osem, cc, in_off, nsteps):
        self.ins, self.outs, self.isem, self.osem = ins, outs, isem, osem
        self.cc, self.in_off, self.nsteps = cc, in_off, nsteps
        self.nb = ins[0][1].shape[1]
        self.ct = pl.program_id(0) + in_off
        self.t = pl.program_id(1)
        self.q = pl.num_programs(1)
        self.s = pl.program_id(0) * self.q + self.t
        self.slot = lax.rem(self.s, 2)

    def _start_inputs(self, ct, t, sl):
        for srcs, buf in self.ins:
            if len(srcs) == 2:
                @pl.when(ct == 0)
                def _():
                    pltpu.make_async_copy(_tile_slice(srcs[0], self.nb, 0, t, self.cc), buf.at[sl], self.isem.at[sl]).start()

                @pl.when(ct > 0)
                def _():
                    pltpu.make_async_copy(_tile_slice(srcs[1], self.nb, ct, t, self.cc, 1), buf.at[sl],
                                          self.isem.at[sl]).start()
            else:
                pltpu.make_async_copy(_tile_slice(srcs[0], self.nb, ct, t, self.cc), buf.at[sl], self.isem.at[sl]).start()

    def _wait_outputs(self, sl):
        for arr, buf, _ in self.outs:
            pltpu.make_async_copy(buf.at[sl], _tile_slice(arr, self.nb, 0, 0, self.cc), self.osem.at[sl]).wait()

    def begin(self):
        ct, t, s, slot = self.ct, self.t, self.s, self.slot

        @pl.when(s == 0)
        def _():
            self._start_inputs(ct, t, 0)

        @pl.when(s + 1 < self.nsteps)
        def _():
            wrap = t == self.q - 1
            self._start_inputs(jnp.where(wrap, ct + 1, ct), jnp.where(wrap, 0, t + 1), 1 - slot)

        for srcs, buf in self.ins:
            pltpu.make_async_copy(_tile_slice(srcs[0], self.nb, 0, 0, self.cc), buf.at[slot], self.isem.at[slot]).wait()

        @pl.when(s >= 2)
        def _():
            self._wait_outputs(slot)

    def end(self):
        for arr, buf, off in self.outs:
            pltpu.make_async_copy(buf.at[self.slot], _tile_slice(arr, self.nb, self.ct, self.t, self.cc, off),
                                  self.osem.at[self.slot]).start()

        @pl.when(self.s == self.nsteps - 1)
        def _():
            if self.nsteps >= 2:
                self._wait_outputs(1 - self.slot)
            self._wait_outputs(self.slot)


def _inproj_kernel(has_y, nsrc, ngrp, cc, nsteps, *refs):
    srcs, refs = list(refs[:nsrc]), refs[nsrc:]
    if has_y:
        y_hbm, gab_ref, gac_ref = refs[:3]
        refs = refs[3:]
    shb_ref, shc_ref, scb_ref, scc_ref, g_ref, wu_ref, wrq_ref = refs[:7]
    refs = refs[7:]
    if has_y:
        xo_hbm, ut_ref, r_hbm, q_hbm, xbuf, rbuf, qbuf, isem, osem, ybuf, xobuf = refs
        ins = [(srcs, xbuf), ([y_hbm], ybuf)]
        outs = [(xo_hbm, xobuf, 0), (r_hbm, rbuf, 0), (q_hbm, qbuf, 0)]
    else:
        ut_ref, r_hbm, q_hbm, xbuf, rbuf, qbuf, isem, osem = refs
        ins = [(srcs, xbuf)]
        outs = [(r_hbm, rbuf, 0), (q_hbm, qbuf, 0)]
    st = _TileStreams(ins, outs, isem, osem, cc, 0, nsteps)
    st.begin()
    ct, slot = st.ct, st.slot
    x = xbuf[slot]
    if has_y:
        x = x + _mod_pick(ct, gab_ref, gac_ref) * ybuf[slot]
        xobuf[slot] = x
    nb, _, d = x.shape
    d_lru = rbuf.shape[-1]
    h = _rms(x, g_ref[...]) * (1.0 + _mod_pick(ct, scb_ref, scc_ref)) + _mod_pick(ct, shb_ref, shc_ref)
    hb = h.reshape(nb * cc, d).astype(BF16)
    ut = lax.dot_general(wu_ref[...], hb, (((1,), (1,)), ((), ())), preferred_element_type=F32)
    ut_ref[...] = ut.reshape(ngrp, ut.shape[0] // ngrp, nb * cc)
    p = jnp.dot(hb, wrq_ref[...], preferred_element_type=F32)
    rbuf[slot] = p[:, :d_lru].reshape(nb, cc, d_lru)
    qbuf[slot] = p[:, d_lru:].reshape(nb, cc, d_lru)
    st.end()


def _inproj(xs, y, mods, layer, g1, w_in, ngrp, d_s5, d_lru, cc):
    q = S5_CHUNK
    nb, d = xs[0].shape[0], xs[0].shape[-1]
    n = sum(a.shape[1] for a in xs)
    nchunk = n // q
    has_y = y is not None
    nh = d_s5 // ngrp
    nsteps = (nchunk // cc) * q
    anyspec = pl.BlockSpec(memory_space=pl.ANY)
    in_specs = [anyspec] * len(xs)
    args = [_view4(a) for a in xs]
    if has_y:
        in_specs += [anyspec] + _mod_specs(layer - 1, 5, d, nb)
        args += [_view4(y), mods, mods]
    wu = w_in[:, :d_s5].T.astype(BF16)
    wrq = w_in[:, d_s5:].astype(BF16)
    full = lambda a: pl.BlockSpec(a.shape, lambda ct, t: (0,) * a.ndim)
    in_specs += _mod_specs(layer, 0, d, nb) + _mod_specs(layer, 1, d, nb) + [
        pl.BlockSpec((1, d), lambda ct, t: (0, 0)), full(wu), full(wrq)]
    args += [mods, mods, mods, mods, g1.reshape(1, d), wu, wrq]
    out_specs = [pl.BlockSpec((ngrp, nh, nb * cc), lambda ct, t: (0, t, ct)), anyspec, anyspec]
    out_shape = [jax.ShapeDtypeStruct((ngrp, q * nh, nb * nchunk), F32),
                 jax.ShapeDtypeStruct((nb, nchunk, q, d_lru), F32),
                 jax.ShapeDtypeStruct((nb, nchunk, q, d_lru), F32)]
    buf = lambda w: pltpu.VMEM((2, nb, cc, w), F32)
    scratch = [buf(d), buf(d_lru), buf(d_lru), pltpu.SemaphoreType.DMA((2,)), pltpu.SemaphoreType.DMA((2,))]
    if has_y:
        out_specs = [anyspec] + out_specs
        out_shape = [jax.ShapeDtypeStruct((nb, nchunk, q, d), F32)] + out_shape
        scratch += [buf(d), buf(d)]
    res = pl.pallas_call(
        functools.partial(_inproj_kernel, has_y, len(xs), ngrp, cc, nsteps),
        grid=(nchunk // cc, q), in_specs=in_specs, out_specs=out_specs, out_shape=out_shape,
        scratch_shapes=scratch,
        compiler_params=_cparams("arbitrary", "arbitrary"),
        name="inproj",
    )(*args)
    if has_y:
        xs = [res[0].reshape(nb, n, d)]
        res = res[1:]
    return [xs, res[0], res[1].reshape(nb, n, d_lru), res[2].reshape(nb, n, d_lru)]


def _s5_weights(a_re, a_im, log_dt, b_re, b_im, c_re, c_im):
    q = S5_CHUNK
    lam = lax.complex(a_re.astype(F32), a_im.astype(F32))
    dt = jnp.exp(log_dt.astype(F32))[..., None]
    a_bar = jnp.exp(lam * dt)
    bmat = lax.complex(b_re.astype(F32), b_im.astype(F32))
    b_bar = ((a_bar - 1) / lam)[..., None] * bmat
    cmat = lax.complex(c_re.astype(F32), c_im.astype(F32))
    k = jnp.arange(q + 1, dtype=F32)
    apow = jnp.exp((lam * dt)[..., None] * k)
    ngrp, nst = a_re.shape[1], a_re.shape[2]
    nh = b_re.shape[-1]

    kern = jnp.einsum('dgop,dgpk,dgpi->dgkoi', cmat, apow[..., :q], b_bar, precision=HIGHEST).real
    sig = jnp.arange(q)[:, None]
    tau = jnp.arange(q)[None, :]
    lag_f = jnp.clip(tau - sig, 0, q - 1)
    lag_b = jnp.clip(sig - tau, 0, q - 1)
    tf = jnp.where((sig <= tau)[None, :, :, None, None], kern[0][:, lag_f], 0.0)
    tb = jnp.where((sig >= tau)[None, :, :, None, None], kern[1][:, lag_b], 0.0)
    toep = (tf + tb).transpose(0, 1, 4, 2, 3).reshape(ngrp, q * nh, q * nh)

    win_f = jnp.einsum('gps,gpi->gsip', apow[0][..., :q][..., ::-1], b_bar[0])
    win_b = jnp.einsum('gps,gpi->gsip', apow[1][..., :q], b_bar[1])
    win = jnp.concatenate([win_f.real, win_b.real, win_f.imag, win_b.imag], axis=-1)
    win = win.reshape(ngrp, q * nh, 4 * nst)

    cf = jnp.einsum('gop,gpt->gpto', cmat[0], apow[0][..., 1:])
    cb = jnp.einsum('gop,gpt->gpto', cmat[1], apow[1][..., 1:][..., ::-1])
    z = jnp.zeros_like(cf.real)
    wof = jnp.concatenate([cf.real, z, -cf.imag, z], axis=1).reshape(ngrp, 4 * nst, q * nh)
    wob = jnp.concatenate([z, cb.real, z, -cb.imag], axis=1).reshape(ngrp, 4 * nst, q * nh)

    aq = apow[..., q]
    a16 = jnp.stack([jnp.concatenate([aq[0].real, aq[1].real], -1),
                     jnp.concatenate([aq[0].imag, aq[1].imag], -1)], axis=1)
    w1 = jnp.concatenate([toep, win], axis=-1)
    return w1.astype(BF16), wof.astype(BF16), wob.astype(BF16), a16.astype(F32)


def _s5_kernel(nb, cc, nc_ctx, nc_lat, u_ref, w1_ref, wof_ref, wob_ref, a_ref, y_ref,
               yrow_ref, sre_ref, sim_ref, fre_ref, fim_ref, bre_ref, bim_ref):
    qh, m = u_ref.shape
    ns2 = a_ref.shape[-1]
    mt = nb * cc
    pitch = cc + S5_ROW_PAD
    w1 = w1_ref[...]

    def intra(i, c):
        r0 = pl.multiple_of(i * mt, mt)
        u = u_ref[:, pl.ds(r0, mt)].T.astype(BF16)
        res = jnp.dot(u, w1, preferred_element_type=F32)
        yrow_ref[pl.ds(r0, mt), :] = res[:, :qh]
        p0 = pl.multiple_of(i * (nb * pitch), 8)
        for b in range(nb):
            sre_ref[pl.ds(p0 + b * pitch, cc), :] = res[b * cc:(b + 1) * cc, qh:qh + ns2]
            sim_ref[pl.ds(p0 + b * pitch, cc), :] = res[b * cc:(b + 1) * cc, qh + ns2:]
        return c

    lax.fori_loop(0, m // mt, intra, 0)

    are = jnp.broadcast_to(a_ref[0:1, :], (nb, ns2))
    aim = jnp.broadcast_to(a_ref[1:2, :], (nb, ns2))
    fwd_lane = lax.broadcasted_iota(jnp.int32, (nb, ns2), 1) < ns2 // 2

    def rows(c):
        tile = lax.div(c, cc)
        return pl.ds(tile * (nb * pitch) + (c - tile * cc), nb, stride=pitch)

    def scan(base, n, carry):
        def step(k, hc):
            hre, him = hc
            rf = rows(base + k)
            rb = rows(base + n - 1 - k)
            fre_ref[rf, :] = hre
            fim_ref[rf, :] = him
            bre_ref[rb, :] = hre
            bim_ref[rb, :] = him
            sre = jnp.where(fwd_lane, sre_ref[rf, :], sre_ref[rb, :])
            sim = jnp.where(fwd_lane, sim_ref[rf, :], sim_ref[rb, :])
            return (are * hre - aim * him + sre, are * him + aim * hre + sim)
        return lax.fori_loop(0, n, step, carry)

    zero = jnp.zeros((nb, ns2), F32)
    carry = scan(0, nc_ctx, (zero, zero))
    scan(nc_ctx, nc_lat, carry)

    wof = wof_ref[...]
    wob = wob_ref[...]

    def inter(i, c):
        r0 = pl.multiple_of(i * mt, mt)
        p0 = pl.multiple_of(i * (nb * pitch), 8)
        tile_rows = lambda ref: jnp.concatenate([ref[pl.ds(p0 + b * pitch, cc), :] for b in range(nb)], axis=0)
        hf = jnp.concatenate([tile_rows(fre_ref), tile_rows(fim_ref)], axis=1).astype(BF16)
        hb = jnp.concatenate([tile_rows(bre_ref), tile_rows(bim_ref)], axis=1).astype(BF16)
        y = (yrow_ref[pl.ds(r0, mt), :] + jnp.dot(hf, wof, preferred_element_type=F32)
             + jnp.dot(hb, wob, preferred_element_type=F32))
        y_ref[:, pl.ds(r0, mt)] = y.T
        return c

    lax.fori_loop(0, m // mt, inter, 0)


def _s5_scan(u_t, w1, wof, wob, a16, nb, cc, nc_ctx, nc_lat):
    ngrp, qh, m = u_t.shape
    ns2 = a16.shape[-1]
    grp = lambda a: pl.BlockSpec((None,) + a.shape[1:], lambda g: (g, 0, 0))
    return pl.pallas_call(
        functools.partial(_s5_kernel, nb, cc, nc_ctx, nc_lat),
        grid=(ngrp,),
        in_specs=[grp(u_t), grp(w1), grp(wof), grp(wob), grp(a16)],
        out_specs=grp(u_t),
        out_shape=jax.ShapeDtypeStruct(u_t.shape, F32),
        scratch_shapes=[pltpu.VMEM((m, qh), F32)] + [pltpu.VMEM((m // cc * (cc + S5_ROW_PAD), ns2), F32)] * 6,
        compiler_params=_cparams("parallel"),
        name="s5_chunked",
    )(u_t, w1, wof, wob, a16)


def _lru_chunk(s, reverse, nc_ctx, nc):
    if not reverse:
        return s
    return jnp.where(s < nc_ctx, nc_ctx - 1 - s, nc - 1 - (s - nc_ctx))


def _lru_kernel(reverse, combine, nc_ctx, nc, rows, *refs):
    if combine:
        (r_hbm, q_hbm, ho_hbm, cw_ref, cb_ref, wa_ref, ba_ref, wx_ref, bx_ref, cn_ref, o_hbm,
         rbuf, obuf, h_ref, a_s, b_s, isem, osem, qbuf, hbuf) = refs
    else:
        (r_hbm, cw_ref, cb_ref, wa_ref, ba_ref, wx_ref, bx_ref, cn_ref, o_hbm,
         rbuf, obuf, h_ref, a_s, b_s, isem, osem) = refs
    _, jt, nb, ch = obuf.shape
    gw = r_hbm.shape[2]
    cblk = (nc_ctx * jt) // gw
    ncol = jt // rows
    s = pl.program_id(0)
    slot = lax.rem(s, 2)

    def chunk_of(step):
        return _lru_chunk(jnp.clip(step, 0, nc - 1), reverse, nc_ctx, nc)

    def main_copies(hbm, buf, row0, c, sl, sem, fn):
        @pl.when(c < nc_ctx)
        def _():
            j0 = c * jt
            for b in range(nb):
                fn(hbm.at[b, j0 // gw, pl.ds(j0 % gw, jt), :], buf.at[sl, pl.ds(row0, jt), b, :], sem.at[sl])

        @pl.when(c >= nc_ctx)
        def _():
            w0 = (c - nc_ctx) * ncol
            for b in range(nb):
                for k in range(ncol):
                    fn(hbm.at[b, pl.ds(cblk, rows), w0 + k, :],
                       buf.at[sl, pl.ds(row0 + k * rows, rows), b, :], sem.at[sl])

    def halo_copies(c, sl, fn):
        first = jnp.logical_or(c == 0, c == nc_ctx)
        last = jnp.logical_or(c == nc_ctx - 1, c == nc - 1)
        in_ctx = c < nc_ctx

        @pl.when(jnp.logical_and(in_ctx, jnp.logical_not(first)))
        def _():
            j = c * jt - CONV_LEFT
            for b in range(nb):
                fn(r_hbm.at[b, j // gw, pl.ds(j % gw, CONV_LEFT), :], rbuf.at[sl, pl.ds(0, CONV_LEFT), b, :],
                   isem.at[sl])

        @pl.when(jnp.logical_and(in_ctx, jnp.logical_not(last)))
        def _():
            j = c * jt + jt
            for b in range(nb):
                fn(r_hbm.at[b, j // gw, pl.ds(j % gw, 1), :], rbuf.at[sl, pl.ds(jt + CONV_LEFT, 1), b, :],
                   isem.at[sl])

        @pl.when(jnp.logical_and(jnp.logical_not(in_ctx), jnp.logical_not(first)))
        def _():
            w = (c - nc_ctx) * ncol - 1
            for b in range(nb):
                fn(r_hbm.at[b, pl.ds(cblk + rows - CONV_LEFT, CONV_LEFT), w, :],
                   rbuf.at[sl, pl.ds(0, CONV_LEFT), b, :], isem.at[sl])

        @pl.when(jnp.logical_and(jnp.logical_not(in_ctx), jnp.logical_not(last)))
        def _():
            w = (c - nc_ctx) * ncol + ncol
            for b in range(nb):
                fn(r_hbm.at[b, pl.ds(cblk, 1), w, :], rbuf.at[sl, pl.ds(jt + CONV_LEFT, 1), b, :], isem.at[sl])

    def loads(c, sl, fn):
        main_copies(r_hbm, rbuf, CONV_LEFT, c, sl, isem, fn)
        halo_copies(c, sl, fn)
        if combine:
            main_copies(q_hbm, qbuf, 0, c, sl, isem, fn)
            main_copies(ho_hbm, hbuf, 0, c, sl, isem, fn)

    def stores(c, sl, fn):
        main_copies(o_hbm, obuf, 0, c, sl, osem, lambda hbm, buf, sem: fn(buf, hbm, sem))

    start = lambda src, dst, sem: pltpu.make_async_copy(src, dst, sem).start()
    wait = lambda src, dst, sem: pltpu.make_async_copy(src, dst, sem).wait()

    c = chunk_of(s)
    first = jnp.logical_or(c == 0, c == nc_ctx)
    last = jnp.logical_or(c == nc_ctx - 1, c == nc - 1)

    @pl.when(s == 0)
    def _():
        h_ref[...] = jnp.zeros_like(h_ref)
        rbuf[...] = jnp.zeros_like(rbuf)
        loads(c, 0, start)

    @pl.when(s + 1 < nc)
    def _():
        loads(chunk_of(s + 1), 1 - slot, start)

    loads(c, slot, wait)

    @pl.when(s >= 2)
    def _():
        stores(chunk_of(s - 2), slot, wait)

    rb = rbuf[slot]
    prev = jnp.where(first, 0.0, rb[0:CONV_LEFT])
    nxt = jnp.where(last, 0.0, rb[jt + CONV_LEFT:jt + CONV_LEFT + 1])
    xe = jnp.concatenate([prev, rb[CONV_LEFT:jt + CONV_LEFT], nxt], axis=0)
    cw = cw_ref[...]
    xc = cb_ref[...].reshape(1, 1, ch)
    for k in range(cw.shape[0]):
        xc = xc + cw[k:k + 1, :].reshape(1, 1, ch) * xe[k:k + jt]
    xc2 = xc.reshape(jt * nb, ch)

    xb = xc2.astype(BF16)
    nblk, wb = wa_ref.shape[0], wa_ref.shape[1]
    za = jnp.concatenate([jnp.dot(xb[:, i * wb:(i + 1) * wb], wa_ref[i], preferred_element_type=F32)
                          for i in range(nblk)], axis=-1)
    zx = jnp.concatenate([jnp.dot(xb[:, i * wb:(i + 1) * wb], wx_ref[i], preferred_element_type=F32)
                          for i in range(nblk)], axis=-1)
    rg = jax.nn.sigmoid(za + ba_ref[...])
    ig = jax.nn.sigmoid(zx + bx_ref[...])
    log_a = cn_ref[...] * rg
    a = jnp.exp(log_a)
    bb = jnp.sqrt(1.0 - a * a) * (ig * xc2)
    a_s[...] = a.reshape(jt, nb, ch)
    b_s[...] = bb.reshape(jt, nb, ch)

    def step(t, h):
        tt = jt - 1 - t if reverse else t
        h = a_s[tt] * h + b_s[tt]
        if combine:
            obuf[slot, tt] = (h + hbuf[slot, tt]) * jax.nn.gelu(qbuf[slot, tt])
        else:
            obuf[slot, tt] = h
        return h

    h_ref[...] = lax.fori_loop(0, jt, step, h_ref[...])

    stores(c, slot, start)

    @pl.when(s == nc - 1)
    def _():
        if nc >= 2:
            stores(chunk_of(s - 1), 1 - slot, wait)
        stores(c, slot, wait)


def _lru_pass(reverse, r4, q4, h_other4, conv_w, conv_b, wa, ba, wx, bx, cneg, n_ctx, rows):
    nb, nblk, gw, ch = r4.shape
    jt = LRU_STEPS
    nc, nc_ctx = (nblk * gw) // jt, n_ctx // jt
    combine = h_other4 is not None
    assert gw % jt == 0 and n_ctx % gw == 0 and jt % rows == 0 and rows >= CONV_LEFT
    anyspec = pl.BlockSpec(memory_space=pl.ANY)
    full = lambda a: pl.BlockSpec(a.shape, lambda s: (0,) * a.ndim)
    consts = [conv_w, conv_b, wa, ba, wx, bx, cneg]
    big = [r4, q4, h_other4] if combine else [r4]
    buf = lambda n: pltpu.VMEM((2, n, nb, ch), F32)
    scratch = [buf(jt + CONV_LEFT + 1), buf(jt), pltpu.VMEM((nb, ch), F32),
               pltpu.VMEM((jt, nb, ch), F32), pltpu.VMEM((jt, nb, ch), F32),
               pltpu.SemaphoreType.DMA((2,)), pltpu.SemaphoreType.DMA((2,))]
    if combine:
        scratch += [buf(jt), buf(jt)]
    return pl.pallas_call(
        functools.partial(_lru_kernel, reverse, combine, nc_ctx, nc, rows),
        grid=(nc,), in_specs=[anyspec] * len(big) + [full(a) for a in consts], out_specs=anyspec,
        out_shape=jax.ShapeDtypeStruct(r4.shape, F32),
        scratch_shapes=scratch,
        compiler_params=_cparams("arbitrary"),
        name="lru_bwd" if reverse else "lru_fwd",
    )(*big, *consts)


def _block_diag_halves(w):
    nh, hd, _ = w.shape
    half = nh // 2
    eye = jnp.eye(half, dtype=w.dtype)
    wh = w.reshape(2, half, hd, hd)
    bd = jnp.einsum('bhij,hk->bhikj', wh, eye).reshape(2, half * hd, half * hd)
    return bd.astype(BF16)


def _post_kernel(n_groups, n_per, nsrc, off, cc, nsteps, *refs):
    srcs, refs = list(refs[:nsrc]), refs[nsrc:]
    (lr_hbm, yt_ref, ut_ref, gab_ref, gac_ref, shb_ref, shc_ref, scb_ref, scc_ref, d_ref, wg_ref, bg_ref,
     wo_ref, g2_ref, wr_ref, br_ref, x1_hbm, hm_hbm, meta_ref, xbuf, lbuf, x1buf, hmbuf, isem, osem) = refs
    st = _TileStreams([(srcs, xbuf), ([lr_hbm], lbuf)], [(x1_hbm, x1buf, off), (hm_hbm, hmbuf, off)],
                      isem, osem, cc, off, nsteps)
    st.begin()
    ct, slot = st.ct, st.slot
    x_in = xbuf[slot]
    nb, _, d = x_in.shape
    tm = nb * cc
    d_s5 = d_ref.shape[0]
    ys = yt_ref[...].reshape(d_s5, tm) + d_ref[...] * ut_ref[...].reshape(d_s5, tm)
    z = jax.nn.gelu(ys)
    s5t = z * jax.nn.sigmoid(jnp.dot(wg_ref[...], z.astype(BF16), preferred_element_type=F32) + bg_ref[...])
    mix = jnp.concatenate([s5t.T.astype(BF16), lbuf[slot].reshape(tm, -1).astype(BF16)], axis=-1)
    proj = jnp.dot(mix, wo_ref[...], preferred_element_type=F32).reshape(nb, cc, d)
    x1 = x_in + _mod_pick(ct, gab_ref, gac_ref) * proj
    x1buf[slot] = x1
    hm3 = _rms(x1, g2_ref[...]) * (1.0 + _mod_pick(ct, scb_ref, scc_ref)) + _mod_pick(ct, shb_ref, shc_ref)
    hmbuf[slot] = hm3
    hm = hm3.reshape(tm, d)

    hi = hm.astype(BF16)
    lo = (hm - hi.astype(F32)).astype(BF16)
    two = jnp.dot(jnp.concatenate([hi, lo], axis=-1), wr_ref[...], preferred_element_type=F32)
    logits = two[:, :LANES] + two[:, LANES:] + br_ref[...]

    lane = lax.broadcasted_iota(jnp.int32, (tm, LANES), 1).astype(F32)
    neg = jnp.float32(-jnp.inf)
    big = jnp.float32(LANES)
    lg = jnp.where(lane < n_groups, logits, neg)
    mg = jnp.max(lg, axis=-1, keepdims=True)
    g_p = 1.0 / jnp.sum(jnp.exp(lg - mg), axis=-1, keepdims=True)
    gidx = jnp.min(jnp.where(lg == mg, lane, big), axis=-1, keepdims=True)
    e0 = n_groups + n_per * gidx
    le = jnp.where(jnp.logical_and(lane >= e0, lane < e0 + n_per), logits, neg)
    m1 = jnp.max(le, axis=-1, keepdims=True)
    i1 = jnp.min(jnp.where(le == m1, lane, big), axis=-1, keepdims=True)
    le2 = jnp.where(lane == i1, neg, le)
    m2 = jnp.max(le2, axis=-1, keepdims=True)
    i2 = jnp.min(jnp.where(le2 == m2, lane, big), axis=-1, keepdims=True)
    r2 = jnp.exp(m2 - m1)
    w1 = g_p / (1.0 + r2)
    w2 = g_p * r2 / (1.0 + r2)
    j1 = i1 - e0
    j2 = i2 - e0
    jlo = jnp.minimum(j1, j2)
    jhi = jnp.maximum(j1, j2)
    wlo = jnp.where(j1 < j2, w1, w2)
    whi = jnp.where(j1 < j2, w2, w1)
    pair = jlo * (2 * n_per - 1 - jlo) * 0.5 + (jhi - jlo - 1.0)
    bucket = gidx * (n_per * (n_per - 1) // 2) + pair
    meta = jnp.where(lane == 0, bucket, jnp.where(lane == 1, wlo, jnp.where(lane == 2, whi, 0.0)))
    meta_ref[...] = meta.T[0:8, :]
    st.end()


def _post(lat_only, yt, ut, lr, xs, mods, layer, s5_d, w_glu, b_glu, w_out_bf, g2, wr2, br, n_groups, n_per, cc):
    q = S5_CHUNK
    nb, d = xs[0].shape[0], xs[0].shape[-1]
    n = sum(a.shape[1] for a in xs)
    nchunk = n // q
    ngrp, qh, _ = yt.shape
    nh = qh // q
    d_s5 = ngrp * nh
    d_lru = lr.shape[-1]
    off = 1 if lat_only else 0
    assert not (lat_only and len(xs) == 2)
    nct = nchunk // cc - off
    tm = nb * cc
    anyspec = pl.BlockSpec(memory_space=pl.ANY)
    tspec = pl.BlockSpec((ngrp, nh, tm), lambda ct, t: (0, t, ct + off))
    full = lambda a: pl.BlockSpec(a.shape, lambda ct, t: (0,) * a.ndim)
    consts = [s5_d.reshape(d_s5, 1), w_glu.T.astype(BF16), b_glu.reshape(d_s5, 1), w_out_bf, g2.reshape(1, d),
              wr2, br]
    buf = lambda w: pltpu.VMEM((2, nb, cc, w), F32)
    x1, hm, meta = pl.pallas_call(
        functools.partial(_post_kernel, n_groups, n_per, len(xs), off, cc, nct * q),
        grid=(nct, q),
        in_specs=([anyspec] * (len(xs) + 1) + [tspec, tspec]
                  + _mod_specs(layer, 2, d, nb) + _mod_specs(layer, 3, d, nb) + _mod_specs(layer, 4, d, nb)
                  + [full(a) for a in consts]),
        out_specs=[anyspec, anyspec, pl.BlockSpec((None, None, 8, tm), lambda ct, t: (t, ct, 0, 0))],
        out_shape=[jax.ShapeDtypeStruct((nb, nct * cc, q, d), F32), jax.ShapeDtypeStruct((nb, nct * cc, q, d), F32),
                   jax.ShapeDtypeStruct((q, nct, 8, tm), F32)],
        scratch_shapes=[buf(d), buf(d_lru), buf(d), buf(d),
                        pltpu.SemaphoreType.DMA((2,)), pltpu.SemaphoreType.DMA((2,))],
        compiler_params=_cparams("arbitrary", "arbitrary"),
        name="post_mixer",
    )(*[_view4(a) for a in xs], _view4(lr), yt, ut, mods, mods, mods, mods, mods, mods, *consts)
    n_out = nct * cc * q
    return x1.reshape(nb, n_out, d), hm.reshape(nb, n_out, d), meta


def _moe_kernel(n_rows, nt, tb_ref, src0_ref, srcn_ref, dst_ref, hm_hbm, wt_ref, w1a, w3a, w2a, w1b, w3b, w2b,
                y_hbm, xbuf, ybuf, gsem, ssem):
    i = pl.program_id(0)
    slot = lax.rem(i, 2)
    tmr = xbuf.shape[1]

    def valid(j):
        return tb_ref[2, jnp.clip(j, 0, nt - 1)] > 0

    def start_gathers(ids_ref, s):
        for r in range(tmr):
            pltpu.make_async_copy(hm_hbm.at[pl.ds(ids_ref[0, r], 1), :], xbuf.at[s, pl.ds(r, 1), :],
                                  gsem.at[s]).start(priority=r % 2)

    def start_scatters(s):
        for r in range(tmr):
            pltpu.make_async_copy(ybuf.at[s, pl.ds(r, 1), :], y_hbm.at[pl.ds(dst_ref[0, r], 1), :],
                                  ssem.at[s]).start(priority=r % 2)

    def wait_gathers(s):
        pltpu.make_async_copy(hm_hbm.at[pl.ds(0, tmr), :], xbuf.at[s], gsem.at[s]).wait()

    def wait_scatters(s):
        pltpu.make_async_copy(ybuf.at[s], y_hbm.at[pl.ds(0, tmr), :], ssem.at[s]).wait()

    @pl.when(i == 0)
    def _():
        xbuf[...] = jnp.zeros_like(xbuf)
        ybuf[...] = jnp.zeros_like(ybuf)
        for s in range((y_hbm.shape[0] - n_rows) // tmr):
            cp = pltpu.make_async_copy(ybuf.at[0], y_hbm.at[pl.ds(n_rows + s * tmr, tmr), :], ssem.at[0])
            cp.start()
            cp.wait()
        start_gathers(src0_ref, 0)

    def step(s):
        @pl.when(jnp.logical_and(i + 1 < nt, valid(i + 1)))
        def _():
            start_gathers(srcn_ref, 1 - s)

        @pl.when(jnp.logical_and(i >= 2, valid(i - 2)))
        def _():
            wait_scatters(s)

        @pl.when(valid(i))
        def _():
            wait_gathers(s)
            xb = xbuf[s].astype(BF16)
            wt = wt_ref[...]

            def expert(w1, w3, w2, gate):
                h1 = jnp.dot(xb, w1[...], preferred_element_type=F32)
                h3 = jnp.dot(xb, w3[...], preferred_element_type=F32)
                hid = (h1 * jax.nn.sigmoid(h1)) * h3
                return gate * jnp.dot(hid.astype(BF16), w2[...], preferred_element_type=F32)

            half = LANES // 2
            ybuf[s] = expert(w1a, w3a, w2a, wt[:, 0:1]) + expert(w1b, w3b, w2b, wt[:, half:half + 1])
            start_scatters(s)

        @pl.when(i == nt - 1)
        def _():
            @pl.when(jnp.logical_and(nt >= 2, valid(i - 1)))
            def _():
                wait_scatters(1 - s)

            @pl.when(valid(i))
            def _():
                wait_scatters(s)

    for s in range(2):
        pl.when(slot == s)(functools.partial(step, s))


def _moe(hm2, bucket, wlo, whi, tok_rows, w1_bf, w3_bf, w2_bf, n_per, tmr, spare_rows):
    n_rows, d = hm2.shape
    assert spare_rows >= 2 * tmr and spare_rows % tmr == 0
    t = bucket.shape[0]
    npairs = n_per * (n_per - 1) // 2
    nbuck = (w1_bf.shape[0] // n_per) * npairs
    ntiles = t // tmr + nbuck

    _, s_tok, s_lo, s_hi = lax.sort((bucket, tok_rows, wlo, whi), num_keys=1, is_stable=True)
    counts = jnp.sum((bucket[None, :] == jnp.arange(nbuck, dtype=jnp.int32)[:, None]).astype(jnp.int32), axis=1)
    padded = ((counts + tmr - 1) // tmr) * tmr
    pend = jnp.cumsum(padded)
    cend = jnp.cumsum(counts)
    shift = (pend - padded) - (cend - counts)
    tile_start = jnp.arange(ntiles, dtype=jnp.int32) * tmr
    valid = (tile_start < pend[-1]).astype(jnp.int32)
    tb = jnp.sum((tile_start[:, None] >= pend[None, :]).astype(jnp.int32), axis=1)
    tb = jnp.minimum(tb, jnp.sum((pend < pend[-1]).astype(jnp.int32)))
    tb = jnp.minimum(tb, nbuck - 1)
    pos = (tile_start[:, None] + jnp.arange(tmr, dtype=jnp.int32)[None, :]) - shift[tb][:, None]
    real = jnp.logical_and(pos < cend[tb][:, None], valid[:, None] > 0)
    pos = jnp.clip(pos, 0, t - 1)
    src = jnp.where(real, s_tok[pos], 0)
    spare = n_rows + (jnp.arange(ntiles, dtype=jnp.int32) % 2)[:, None] * tmr + jnp.arange(tmr, dtype=jnp.int32)
    dst = jnp.where(real, s_tok[pos], spare)
    g_lo = jnp.where(real, s_lo[pos], 0.0).reshape(-1, 1)
    g_hi = jnp.where(real, s_hi[pos], 0.0).reshape(-1, 1)
    half = LANES // 2
    wts = jnp.concatenate([jnp.broadcast_to(g_lo, (ntiles * tmr, half)),
                           jnp.broadcast_to(g_hi, (ntiles * tmr, half))], axis=1)
    grp, pr = tb // npairs, tb % npairs
    pairs = [(i, j) for i in range(n_per) for j in range(i + 1, n_per)]
    plo = jnp.array([p[0] for p in pairs], jnp.int32)[pr]
    phi = jnp.array([p[1] for p in pairs], jnp.int32)[pr]
    tinfo = jnp.stack([grp * n_per + plo, grp * n_per + phi, valid])

    wspec = lambda a, row: pl.BlockSpec((None,) + a.shape[1:], lambda i, tb_: (tb_[row, i], 0, 0))
    ids = lambda f: pl.BlockSpec((None, 1, tmr), lambda i, tb_: (f(i), 0, 0), memory_space=pltpu.SMEM)
    grid_spec = pltpu.PrefetchScalarGridSpec(
        num_scalar_prefetch=1,
        grid=(ntiles,),
        in_specs=[ids(lambda i: 0), ids(lambda i: jnp.minimum(i + 1, ntiles - 1)), ids(lambda i: i),
                  pl.BlockSpec(memory_space=pl.ANY),
                  pl.BlockSpec((tmr, LANES), lambda i, tb_: (i, 0)),
                  wspec(w1_bf, 0), wspec(w3_bf, 0), wspec(w2_bf, 0),
                  wspec(w1_bf, 1), wspec(w3_bf, 1), wspec(w2_bf, 1)],
        out_specs=pl.BlockSpec(memory_space=pl.ANY),
        scratch_shapes=[pltpu.VMEM((2, tmr, d), F32), pltpu.VMEM((2, tmr, d), F32),
                        pltpu.SemaphoreType.DMA((2,)), pltpu.SemaphoreType.DMA((2,))],
    )
    src3 = src.reshape(ntiles, 1, tmr)
    return pl.pallas_call(
        functools.partial(_moe_kernel, n_rows, ntiles),
        grid_spec=grid_spec,
        out_shape=jax.ShapeDtypeStruct((n_rows + spare_rows, d), F32),
        compiler_params=_cparams("arbitrary"),
        name="moe_pairs",
    )(tinfo, src3, src3, dst.reshape(ntiles, 1, tmr), hm2, wts, w1_bf, w3_bf, w2_bf, w1_bf, w3_bf, w2_bf)


def _final_kernel(x_ref, y_ref, ga_ref, g_ref, o_ref):
    o_ref[...] = _rms(x_ref[...] + ga_ref[...] * y_ref[...], g_ref[...])


def _final(x1, y, mods, layer, final_g, tm):
    nb, seq, d = x1.shape
    tok = pl.BlockSpec((None, tm, d), lambda b, i: (b, i, 0))
    return pl.pallas_call(
        _final_kernel,
        grid=(nb, seq // tm),
        in_specs=[tok, pl.BlockSpec((tm, d), lambda b, i: (b * (seq // tm) + i, 0)),
                  _mod_spec_lat(layer, 5, d), pl.BlockSpec((1, d), lambda b, i: (0, 0))],
        out_specs=tok,
        out_shape=jax.ShapeDtypeStruct((nb, seq, d), F32),
        compiler_params=_cparams("parallel", "parallel"),
        name="final_norm",
    )(x1, y, mods, final_g.reshape(1, d))


def kernel(x, c, ctx, c_ctx, w_mod, b_mod, norm1_g, norm2_g, w_in, w_out, s5_a_re, s5_a_im, s5_log_dt, s5_b_re, s5_b_im, s5_c_re, s5_c_im, s5_d, s5_w_glu, s5_b_glu, lru_conv_w, lru_conv_b, lru_w_a, lru_b_a, lru_w_x, lru_b_x, lru_lam, moe_w_group, moe_b_group, moe_w_router, moe_b_router, moe_w1, moe_w3, moe_w2, final_g):
    nb, seq, d = x.shape
    n_ctx = ctx.shape[1]
    depth = w_mod.shape[0]
    n = n_ctx + seq
    d_s5 = s5_d.shape[-1]
    d_lru = lru_conv_b.shape[-1]
    ngrp, nh = s5_b_re.shape[2], s5_b_re.shape[4]
    n_groups, n_per = moe_w_router.shape[1], moe_w_router.shape[3]
    rows = seq // GRID_W
    tm = n_ctx
    q = S5_CHUNK
    nc_ctx, nc_lat = n_ctx // q, seq // q
    moe_rows = 256 if nb * seq >= 16384 else 32
    assert seq % GRID_W == 0 and seq % tm == 0 and n_ctx % GRID_W == 0 and seq % LRU_STEPS == 0
    assert d_s5 == ngrp * nh and q * nh == 2 * LANES and n_groups + n_groups * n_per <= LANES

    pad = (-(nb + 1)) % 8
    c_rows = jnp.concatenate([c, c_ctx[None, :], jnp.zeros((pad, d), F32)], axis=0)
    mods = _modulation(c_rows, w_mod, b_mod).reshape(depth, nb + 1 + pad, 1, 6 * d)

    cc = nc_ctx
    xs = [ctx, x]
    y = None
    for l in range(depth):
        need_ctx = l < depth - 1
        xs, ut, r, qg = _inproj(xs, y, mods, l, norm1_g[l], w_in[l], ngrp, d_s5, d_lru, cc)

        w1, wof, wob, a16 = _s5_weights(s5_a_re[l], s5_a_im[l], s5_log_dt[l], s5_b_re[l], s5_b_im[l],
                                         s5_c_re[l], s5_c_im[l])
        yt = _s5_scan(ut, w1, wof, wob, a16, nb, cc, nc_ctx, nc_lat)

        view = lambda a: a.reshape(nb, n // GRID_W, GRID_W, d_lru)
        cneg = (-LRU_C * jax.nn.softplus(-lru_lam[l].astype(F32))).reshape(2, 1, d_lru)
        cw, cb = lru_conv_w[l].astype(F32), lru_conv_b[l].reshape(1, d_lru).astype(F32)
        gates = [(_block_diag_halves(lru_w_a[l, dd]), lru_b_a[l, dd].reshape(1, d_lru),
                  _block_diag_halves(lru_w_x[l, dd]), lru_b_x[l, dd].reshape(1, d_lru), cneg[dd]) for dd in range(2)]
        h_b = _lru_pass(True, view(r), None, None, cw, cb, *gates[1], n_ctx, rows)
        lr = _lru_pass(False, view(r), view(qg), h_b, cw, cb, *gates[0], n_ctx, rows).reshape(nb, n, d_lru)

        wr = jnp.concatenate([moe_w_group[l], moe_w_router[l].transpose(1, 0, 2).reshape(d, n_groups * n_per)], -1)
        wr = jnp.pad(wr.astype(F32), ((0, 0), (0, LANES - wr.shape[-1])))
        wr_hi = wr.astype(BF16)
        wr_lo = (wr - wr_hi.astype(F32)).astype(BF16)
        wr2 = jnp.concatenate([jnp.concatenate([wr_hi, wr_lo], -1),
                               jnp.concatenate([wr_hi, jnp.zeros_like(wr_lo)], -1)], axis=0)
        br = jnp.concatenate([moe_b_group[l], moe_b_router[l].reshape(-1)])
        br = jnp.pad(br.astype(F32), (0, LANES - br.shape[0])).reshape(1, LANES)

        x1, hm, meta = _post(not need_ctx, yt, ut, lr, xs, mods, l, s5_d[l], s5_w_glu[l], s5_b_glu[l],
                             w_out[l].astype(BF16), norm2_g[l], wr2, br, n_groups, n_per, cc)

        n_out = x1.shape[1]
        nct = n_out // (cc * q)
        meta_t = meta.reshape(q, nct, 8, nb, cc).transpose(2, 3, 1, 4, 0).reshape(8, nb * n_out)
        bucket = meta_t[0].astype(jnp.int32)
        y = _moe(hm.reshape(nb * n_out, d), bucket, meta_t[1], meta_t[2], jnp.arange(nb * n_out, dtype=jnp.int32),
                 moe_w1[l].astype(BF16), moe_w3[l].astype(BF16), moe_w2[l].astype(BF16), n_per, moe_rows, n_out)
        xs = [x1]
        if need_ctx:
            y = y.reshape(nb + 1, n_out, d)
    return _final(xs[0], y, mods, depth - 1, final_g, 4 * tm if seq % (4 * tm) == 0 else tm)
```

```python
import functools

import jax
import jax.numpy as jnp
from jax import lax
from jax.experimental import pallas as pl
from jax.experimental.pallas import tpu as pltpu

F32 = jnp.float32
BF16 = jnp.bfloat16
HIGHEST = lax.Precision.HIGHEST

EPS = 1e-6
GRID_W = 64
LRU_C = 8.0
CONV_LEFT = 2
S5_CHUNK = 16
S5_ROW_PAD = 8
LRU_STEPS = 32
LANES = 128
VMEM_LIMIT = 56 * 1024 * 1024


def _cparams(*sem):
    return pltpu.CompilerParams(dimension_semantics=sem, vmem_limit_bytes=VMEM_LIMIT)


def _rms(x, g):
    return x * lax.rsqrt(jnp.mean(x * x, axis=-1, keepdims=True) + EPS) * g


def _mod_kernel(c_ref, w_ref, b_ref, o_ref):
    c = c_ref[...]
    s = c * jax.nn.sigmoid(c)
    o_ref[...] = jnp.dot(s, w_ref[...], preferred_element_type=F32, precision=HIGHEST) + b_ref[...]


def _modulation(c_rows, w_mod, b_mod):
    depth, d, n6 = w_mod.shape
    rows = c_rows.shape[0]
    tn = n6 // 4
    return pl.pallas_call(
        _mod_kernel,
        grid=(depth, n6 // tn),
        in_specs=[pl.BlockSpec((rows, d), lambda l, j: (0, 0)),
                  pl.BlockSpec((None, d, tn), lambda l, j: (l, 0, j)),
                  pl.BlockSpec((None, 1, tn), lambda l, j: (l, 0, j))],
        out_specs=pl.BlockSpec((None, rows, tn), lambda l, j: (l, 0, j)),
        out_shape=jax.ShapeDtypeStruct((depth, rows, n6), F32),
        compiler_params=_cparams("parallel", "parallel"),
        name="modulation",
    )(c_rows, w_mod, b_mod.reshape(depth, 1, n6))


def _mod_specs(layer, part, d, nb):
    return [pl.BlockSpec((None, nb, 1, d), lambda ct, t: (layer, 0, 0, part)),
            pl.BlockSpec((None, None, 1, d), lambda ct, t: (layer, nb, 0, part))]


def _mod_pick(ct, batch_ref, ctx_ref):
    return jnp.where(ct == 0, ctx_ref[...][None], batch_ref[...])


def _mod_spec_lat(layer, part, d):
    return pl.BlockSpec((None, None, 1, d), lambda b, i: (layer, b, 0, part))


def _view4(a):
    nb, n, w = a.shape
    return a.reshape(nb, n // S5_CHUNK, S5_CHUNK, w)


def _tile_slice(a4, nb, ct, t, cc, off=0):
    return a4.at[pl.ds(0, nb), pl.ds((ct - off) * cc, cc), t, :]


class _TileStreams:
    def __init__(self, ins, outs, isem, osem, cc, in_off, nsteps):
        self.ins, self.outs, self.isem, self.osem = ins, outs, isem, osem
        self.cc, self.in_off, self.nsteps = cc, in_off, nsteps
        self.nb = ins[0][1].shape[1]
        self.ct = pl.program_id(0) + in_off
        self.t = pl.program_id(1)
        self.q = pl.num_programs(1)
        self.s = pl.program_id(0) * self.q + self.t
        self.slot = lax.rem(self.s, 2)

    def _start_inputs(self, ct, t, sl):
        for srcs, buf in self.ins:
            if len(srcs) == 2:
                @pl.when(ct == 0)
                def _():
                    pltpu.make_async_copy(_tile_slice(srcs[0], self.nb, 0, t, self.cc), buf.at[sl], self.isem.at[sl]).start()

                @pl.when(ct > 0)
                def _():
                    pltpu.make_async_copy(_tile_slice(srcs[1], self.nb, ct, t, self.cc, 1), buf.at[sl],
                                          self.isem.at[sl]).start()
            else:
                pltpu.make_async_copy(_tile_slice(srcs[0], self.nb, ct, t, self.cc), buf.at[sl], self.isem.at[sl]).start()

    def _wait_outputs(self, sl):
        for arr, buf, _ in self.outs:
            pltpu.make_async_copy(buf.at[sl], _tile_slice(arr, self.nb, 0, 0, self.cc), self.osem.at[sl]).wait()

    def begin(self):
        ct, t, s, slot = self.ct, self.t, self.s, self.slot

        @pl.when(s == 0)
        def _():
            self._start_inputs(ct, t, 0)

        @pl.when(s + 1 < self.nsteps)
        def _():
            wrap = t == self.q - 1
            self._start_inputs(jnp.where(wrap, ct + 1, ct), jnp.where(wrap, 0, t + 1), 1 - slot)

        for srcs, buf in self.ins:
            pltpu.make_async_copy(_tile_slice(srcs[0], self.nb, 0, 0, self.cc), buf.at[slot], self.isem.at[slot]).wait()

        @pl.when(s >= 2)
        def _():
            self._wait_outputs(slot)

    def end(self):
        for arr, buf, off in self.outs:
            pltpu.make_async_copy(buf.at[self.slot], _tile_slice(arr, self.nb, self.ct, self.t, self.cc, off),
                                  self.osem.at[self.slot]).start()

        @pl.when(self.s == self.nsteps - 1)
        def _():
            if self.nsteps >= 2:
                self._wait_outputs(1 - self.slot)
            self._wait_outputs(self.slot)


def _inproj_kernel(has_y, nsrc, ngrp, cc, nsteps, *refs):
    srcs, refs = list(refs[:nsrc]), refs[nsrc:]
    if has_y:
        y_hbm, gab_ref, gac_ref = refs[:3]
        refs = refs[3:]
    shb_ref, shc_ref, scb_ref, scc_ref, g_ref, wu_ref, wrq_ref = refs[:7]
    refs = refs[7:]
    if has_y:
        xo_hbm, ut_ref, r_hbm, q_hbm, xbuf, rbuf, qbuf, isem, osem, ybuf, xobuf = refs
        ins = [(srcs, xbuf), ([y_hbm], ybuf)]
        outs = [(xo_hbm, xobuf, 0), (r_hbm, rbuf, 0), (q_hbm, qbuf, 0)]
    else:
        ut_ref, r_hbm, q_hbm, xbuf, rbuf, qbuf, isem, osem = refs
        ins = [(srcs, xbuf)]
        outs = [(r_hbm, rbuf, 0), (q_hbm, qbuf, 0)]
    st = _TileStreams(ins, outs, isem, osem, cc, 0, nsteps)
    st.begin()
    ct, slot = st.ct, st.slot
    x = xbuf[slot]
    if has_y:
        x = x + _mod_pick(ct, gab_ref, gac_ref) * ybuf[slot]
        xobuf[slot] = x
    nb, _, d = x.shape
    d_lru = rbuf.shape[-1]
    h = _rms(x, g_ref[...]) * (1.0 + _mod_pick(ct, scb_ref, scc_ref)) + _mod_pick(ct, shb_ref, shc_ref)
    hb = h.reshape(nb * cc, d).astype(BF16)
    ut = lax.dot_general(wu_ref[...], hb, (((1,), (1,)), ((), ())), preferred_element_type=F32)
    ut_ref[...] = ut.reshape(ngrp, ut.shape[0] // ngrp, nb * cc)
    p = jnp.dot(hb, wrq_ref[...], preferred_element_type=F32)
    rbuf[slot] = p[:, :d_lru].reshape(nb, cc, d_lru)
    qbuf[slot] = p[:, d_lru:].reshape(nb, cc, d_lru)
    st.end()


def _inproj(xs, y, mods, layer, g1, w_in, ngrp, d_s5, d_lru, cc):
    q = S5_CHUNK
    nb, d = xs[0].shape[0], xs[0].shape[-1]
    n = sum(a.shape[1] for a in xs)
    nchunk = n // q
    has_y = y is not None
    nh = d_s5 // ngrp
    nsteps = (nchunk // cc) * q
    anyspec = pl.BlockSpec(memory_space=pl.ANY)
    in_specs = [anyspec] * len(xs)
    args = [_view4(a) for a in xs]
    if has_y:
        in_specs += [anyspec] + _mod_specs(layer - 1, 5, d, nb)
        args += [_view4(y), mods, mods]
    wu = w_in[:, :d_s5].T.astype(BF16)
    wrq = w_in[:, d_s5:].astype(BF16)
    full = lambda a: pl.BlockSpec(a.shape, lambda ct, t: (0,) * a.ndim)
    in_specs += _mod_specs(layer, 0, d, nb) + _mod_specs(layer, 1, d, nb) + [
        pl.BlockSpec((1, d), lambda ct, t: (0, 0)), full(wu), full(wrq)]
    args += [mods, mods, mods, mods, g1.reshape(1, d), wu, wrq]
    out_specs = [pl.BlockSpec((ngrp, nh, nb * cc), lambda ct, t: (0, t, ct)), anyspec, anyspec]
    out_shape = [jax.ShapeDtypeStruct((ngrp, q * nh, nb * nchunk), F32),
                 jax.ShapeDtypeStruct((nb, nchunk, q, d_lru), F32),
                 jax.ShapeDtypeStruct((nb, nchunk, q, d_lru), F32)]
    buf = lambda w: pltpu.VMEM((2, nb, cc, w), F32)
    scratch = [buf(d), buf(d_lru), buf(d_lru), pltpu.SemaphoreType.DMA((2,)), pltpu.SemaphoreType.DMA((2,))]
    if has_y:
        out_specs = [anyspec] + out_specs
        out_shape = [jax.ShapeDtypeStruct((nb, nchunk, q, d), F32)] + out_shape
        scratch += [buf(d), buf(d)]
    res = pl.pallas_call(
        functools.partial(_inproj_kernel, has_y, len(xs), ngrp, cc, nsteps),
        grid=(nchunk // cc, q), in_specs=in_specs, out_specs=out_specs, out_shape=out_shape,
        scratch_shapes=scratch,
        compiler_params=_cparams("arbitrary", "arbitrary"),
        name="inproj",
    )(*args)
    if has_y:
        xs = [res[0].reshape(nb, n, d)]
        res = res[1:]
    return [xs, res[0], res[1].reshape(nb, n, d_lru), res[2].reshape(nb, n, d_lru)]


def _s5_weights(a_re, a_im, log_dt, b_re, b_im, c_re, c_im):
    q = S5_CHUNK
    lam = lax.complex(a_re.astype(F32), a_im.astype(F32))
    dt = jnp.exp(log_dt.astype(F32))[..., None]
    a_bar = jnp.exp(lam * dt)
    bmat = lax.complex(b_re.astype(F32), b_im.astype(F32))
    b_bar = ((a_bar - 1) / lam)[..., None] * bmat
    cmat = lax.complex(c_re.astype(F32), c_im.astype(F32))
    k = jnp.arange(q + 1, dtype=F32)
    apow = jnp.exp((lam * dt)[..., None] * k)
    ngrp, nst = a_re.shape[1], a_re.shape[2]
    nh = b_re.shape[-1]

    kern = jnp.einsum('dgop,dgpk,dgpi->dgkoi', cmat, apow[..., :q], b_bar, precision=HIGHEST).real
    sig = jnp.arange(q)[:, None]
    tau = jnp.arange(q)[None, :]
    lag_f = jnp.clip(tau - sig, 0, q - 1)
    lag_b = jnp.clip(sig - tau, 0, q - 1)
    tf = jnp.where((sig <= tau)[None, :, :, None, None], kern[0][:, lag_f], 0.0)
    tb = jnp.where((sig >= tau)[None, :, :, None, None], kern[1][:, lag_b], 0.0)
    toep = (tf + tb).transpose(0, 1, 4, 2, 3).reshape(ngrp, q * nh, q * nh)

    win_f = jnp.einsum('gps,gpi->gsip', apow[0][..., :q][..., ::-1], b_bar[0])
    win_b = jnp.einsum('gps,gpi->gsip', apow[1][..., :q], b_bar[1])
    win = jnp.concatenate([win_f.real, win_b.real, win_f.imag, win_b.imag], axis=-1)
    win = win.reshape(ngrp, q * nh, 4 * nst)

    cf = jnp.einsum('gop,gpt->gpto', cmat[0], apow[0][..., 1:])
    cb = jnp.einsum('gop,gpt->gpto', cmat[1], apow[1][..., 1:][..., ::-1])
    z = jnp.zeros_like(cf.real)
    wof = jnp.concatenate([cf.real, z, -cf.imag, z], axis=1).reshape(ngrp, 4 * nst, q * nh)
    wob = jnp.concatenate([z, cb.real, z, -cb.imag], axis=1).reshape(ngrp, 4 * nst, q * nh)

    aq = apow[..., q]
    a16 = jnp.stack([jnp.concatenate([aq[0].real, aq[1].real], -1),
                     jnp.concatenate([aq[0].imag, aq[1].imag], -1)], axis=1)
    w1 = jnp.concatenate([toep, win], axis=-1)
    return w1.astype(BF16), wof.astype(BF16), wob.astype(BF16), a16.astype(F32)


def _s5_kernel(nb, cc, nc_ctx, nc_lat, u_ref, w1_ref, wof_ref, wob_ref, a_ref, y_ref,
               yrow_ref, sre_ref, sim_ref, fre_ref, fim_ref, bre_ref, bim_ref):
    qh, m = u_ref.shape
    ns2 = a_ref.shape[-1]
    mt = nb * cc
    pitch = cc + S5_ROW_PAD
    w1 = w1_ref[...]

    def intra(i, c):
        r0 = pl.multiple_of(i * mt, mt)
        u = u_ref[:, pl.ds(r0, mt)].T.astype(BF16)
        res = jnp.dot(u, w1, preferred_element_type=F32)
        yrow_ref[pl.ds(r0, mt), :] = res[:, :qh]
        p0 = pl.multiple_of(i * (nb * pitch), 8)
        for b in range(nb):
            sre_ref[pl.ds(p0 + b * pitch, cc), :] = res[b * cc:(b + 1) * cc, qh:qh + ns2]
            sim_ref[pl.ds(p0 + b * pitch, cc), :] = res[b * cc:(b + 1) * cc, qh + ns2:]
        return c

    lax.fori_loop(0, m // mt, intra, 0)

    are = jnp.broadcast_to(a_ref[0:1, :], (nb, ns2))
    aim = jnp.broadcast_to(a_ref[1:2, :], (nb, ns2))
    fwd_lane = lax.broadcasted_iota(jnp.int32, (nb, ns2), 1) < ns2 // 2

    def rows(c):
        tile = lax.div(c, cc)
        return pl.ds(tile * (nb * pitch) + (c - tile * cc), nb, stride=pitch)

    def scan(base, n, carry):
        def step(k, hc):
            hre, him = hc
            rf = rows(base + k)
            rb = rows(base + n - 1 - k)
            fre_ref[rf, :] = hre
            fim_ref[rf, :] = him
            bre_ref[rb, :] = hre
            bim_ref[rb, :] = him
            sre = jnp.where(fwd_lane, sre_ref[rf, :], sre_ref[rb, :])
            sim = jnp.where(fwd_lane, sim_ref[rf, :], sim_ref[rb, :])
            return (are * hre - aim * him + sre, are * him + aim * hre + sim)
        return lax.fori_loop(0, n, step, carry)

    zero = jnp.zeros((nb, ns2), F32)
    carry = scan(0, nc_ctx, (zero, zero))
    scan(nc_ctx, nc_lat, carry)

    wof = wof_ref[...]
    wob = wob_ref[...]

    def inter(i, c):
        r0 = pl.multiple_of(i * mt, mt)
        p0 = pl.multiple_of(i * (nb * pitch), 8)
        tile_rows = lambda ref: jnp.concatenate([ref[pl.ds(p0 + b * pitch, cc), :] for b in range(nb)], axis=0)
        hf = jnp.concatenate([tile_rows(fre_ref), tile_rows(fim_ref)], axis=1).astype(BF16)
        hb = jnp.concatenate([tile_rows(bre_ref), tile_rows(bim_ref)], axis=1).astype(BF16)
        y = (yrow_ref[pl.ds(r0, mt), :] + jnp.dot(hf, wof, preferred_element_type=F32)
             + jnp.dot(hb, wob, preferred_element_type=F32))
        y_ref[:, pl.ds(r0, mt)] = y.T
        return c

    lax.fori_loop(0, m // mt, inter, 0)


def _s5_scan(u_t, w1, wof, wob, a16, layer, nb, cc, nc_ctx, nc_lat):
    ngrp, qh, m = u_t.shape
    ns2 = a16.shape[-1]
    grp = lambda a: pl.BlockSpec((None,) + a.shape[1:], lambda g: (g, 0, 0))
    lgrp = lambda a: pl.BlockSpec((None, None) + a.shape[2:], lambda g: (layer, g, 0, 0))
    return pl.pallas_call(
        functools.partial(_s5_kernel, nb, cc, nc_ctx, nc_lat),
        grid=(ngrp,),
        in_specs=[grp(u_t), lgrp(w1), lgrp(wof), lgrp(wob), lgrp(a16)],
        out_specs=grp(u_t),
        out_shape=jax.ShapeDtypeStruct(u_t.shape, F32),
        scratch_shapes=[pltpu.VMEM((m, qh), F32)] + [pltpu.VMEM((m // cc * (cc + S5_ROW_PAD), ns2), F32)] * 6,
        compiler_params=_cparams("parallel"),
        name="s5_chunked",
    )(u_t, w1, wof, wob, a16)


def _lru_chunk(s, reverse, nc_ctx, nc):
    if not reverse:
        return s
    return jnp.where(s < nc_ctx, nc_ctx - 1 - s, nc - 1 - (s - nc_ctx))


def _lru_kernel(reverse, combine, nc_ctx, nc, rows, *refs):
    if combine:
        (r_hbm, q_hbm, ho_hbm, cw_ref, cb_ref, wa_ref, ba_ref, wx_ref, bx_ref, cn_ref, o_hbm,
         rbuf, obuf, h_ref, a_s, b_s, isem, osem, qbuf, hbuf) = refs
    else:
        (r_hbm, cw_ref, cb_ref, wa_ref, ba_ref, wx_ref, bx_ref, cn_ref, o_hbm,
         rbuf, obuf, h_ref, a_s, b_s, isem, osem) = refs
    _, jt, nb, ch = obuf.shape
    gw = r_hbm.shape[2]
    cblk = (nc_ctx * jt) // gw
    ncol = jt // rows
    s = pl.program_id(0)
    slot = lax.rem(s, 2)

    def chunk_of(step):
        return _lru_chunk(jnp.clip(step, 0, nc - 1), reverse, nc_ctx, nc)

    def main_copies(hbm, buf, row0, c, sl, sem, fn):
        @pl.when(c < nc_ctx)
        def _():
            j0 = c * jt
            for b in range(nb):
                fn(hbm.at[b, j0 // gw, pl.ds(j0 % gw, jt), :], buf.at[sl, pl.ds(row0, jt), b, :], sem.at[sl])

        @pl.when(c >= nc_ctx)
        def _():
            w0 = (c - nc_ctx) * ncol
            for b in range(nb):
                for k in range(ncol):
                    fn(hbm.at[b, pl.ds(cblk, rows), w0 + k, :],
                       buf.at[sl, pl.ds(row0 + k * rows, rows), b, :], sem.at[sl])

    def halo_copies(c, sl, fn):
        first = jnp.logical_or(c == 0, c == nc_ctx)
        last = jnp.logical_or(c == nc_ctx - 1, c == nc - 1)
        in_ctx = c < nc_ctx

        @pl.when(jnp.logical_and(in_ctx, jnp.logical_not(first)))
        def _():
            j = c * jt - CONV_LEFT
            for b in range(nb):
                fn(r_hbm.at[b, j // gw, pl.ds(j % gw, CONV_LEFT), :], rbuf.at[sl, pl.ds(0, CONV_LEFT), b, :],
                   isem.at[sl])

        @pl.when(jnp.logical_and(in_ctx, jnp.logical_not(last)))
        def _():
            j = c * jt + jt
            for b in range(nb):
                fn(r_hbm.at[b, j // gw, pl.ds(j % gw, 1), :], rbuf.at[sl, pl.ds(jt + CONV_LEFT, 1), b, :],
                   isem.at[sl])

        @pl.when(jnp.logical_and(jnp.logical_not(in_ctx), jnp.logical_not(first)))
        def _():
            w = (c - nc_ctx) * ncol - 1
            for b in range(nb):
                fn(r_hbm.at[b, pl.ds(cblk + rows - CONV_LEFT, CONV_LEFT), w, :],
                   rbuf.at[sl, pl.ds(0, CONV_LEFT), b, :], isem.at[sl])

        @pl.when(jnp.logical_and(jnp.logical_not(in_ctx), jnp.logical_not(last)))
        def _():
            w = (c - nc_ctx) * ncol + ncol
            for b in range(nb):
                fn(r_hbm.at[b, pl.ds(cblk, 1), w, :], rbuf.at[sl, pl.ds(jt + CONV_LEFT, 1), b, :], isem.at[sl])

    def loads(c, sl, fn):
        main_copies(r_hbm, rbuf, CONV_LEFT, c, sl, isem, fn)
        halo_copies(c, sl, fn)
        if combine:
            main_copies(q_hbm, qbuf, 0, c, sl, isem, fn)
            main_copies(ho_hbm, hbuf, 0, c, sl, isem, fn)

    def stores(c, sl, fn):
        main_copies(o_hbm, obuf, 0, c, sl, osem, lambda hbm, buf, sem: fn(buf, hbm, sem))

    start = lambda src, dst, sem: pltpu.make_async_copy(src, dst, sem).start()
    wait = lambda src, dst, sem: pltpu.make_async_copy(src, dst, sem).wait()

    c = chunk_of(s)
    first = jnp.logical_or(c == 0, c == nc_ctx)
    last = jnp.logical_or(c == nc_ctx - 1, c == nc - 1)

    @pl.when(s == 0)
    def _():
        h_ref[...] = jnp.zeros_like(h_ref)
        rbuf[...] = jnp.zeros_like(rbuf)
        loads(c, 0, start)

    @pl.when(s + 1 < nc)
    def _():
        loads(chunk_of(s + 1), 1 - slot, start)

    loads(c, slot, wait)

    @pl.when(s >= 2)
    def _():
        stores(chunk_of(s - 2), slot, wait)

    rb = rbuf[slot]
    prev = jnp.where(first, 0.0, rb[0:CONV_LEFT])
    nxt = jnp.where(last, 0.0, rb[jt + CONV_LEFT:jt + CONV_LEFT + 1])
    xe = jnp.concatenate([prev, rb[CONV_LEFT:jt + CONV_LEFT], nxt], axis=0)
    cw = cw_ref[...]
    xc = cb_ref[...].reshape(1, 1, ch)
    for k in range(cw.shape[0]):
        xc = xc + cw[k:k + 1, :].reshape(1, 1, ch) * xe[k:k + jt]
    xc2 = xc.reshape(jt * nb, ch)

    xb = xc2.astype(BF16)
    nblk, wb = wa_ref.shape[0], wa_ref.shape[1]
    za = jnp.concatenate([jnp.dot(xb[:, i * wb:(i + 1) * wb], wa_ref[i], preferred_element_type=F32)
                          for i in range(nblk)], axis=-1)
    zx = jnp.concatenate([jnp.dot(xb[:, i * wb:(i + 1) * wb], wx_ref[i], preferred_element_type=F32)
                          for i in range(nblk)], axis=-1)
    rg = jax.nn.sigmoid(za + ba_ref[...])
    ig = jax.nn.sigmoid(zx + bx_ref[...])
    log_a = cn_ref[...] * rg
    a = jnp.exp(log_a)
    bb = jnp.sqrt(1.0 - a * a) * (ig * xc2)
    a_s[...] = a.reshape(jt, nb, ch)
    b_s[...] = bb.reshape(jt, nb, ch)

    def step(t, h):
        tt = jt - 1 - t if reverse else t
        h = a_s[tt] * h + b_s[tt]
        if combine:
            obuf[slot, tt] = (h + hbuf[slot, tt]) * jax.nn.gelu(qbuf[slot, tt])
        else:
            obuf[slot, tt] = h
        return h

    h_ref[...] = lax.fori_loop(0, jt, step, h_ref[...])

    stores(c, slot, start)

    @pl.when(s == nc - 1)
    def _():
        if nc >= 2:
            stores(chunk_of(s - 1), 1 - slot, wait)
        stores(c, slot, wait)


def _lru_pass(reverse, r4, q4, h_other4, conv_w, conv_b, wa, ba, wx, bx, cneg, n_ctx, rows):
    nb, nblk, gw, ch = r4.shape
    jt = LRU_STEPS
    nc, nc_ctx = (nblk * gw) // jt, n_ctx // jt
    combine = h_other4 is not None
    assert gw % jt == 0 and n_ctx % gw == 0 and jt % rows == 0 and rows >= CONV_LEFT
    anyspec = pl.BlockSpec(memory_space=pl.ANY)
    full = lambda a: pl.BlockSpec(a.shape, lambda s: (0,) * a.ndim)
    consts = [conv_w, conv_b, wa, ba, wx, bx, cneg]
    big = [r4, q4, h_other4] if combine else [r4]
    buf = lambda n: pltpu.VMEM((2, n, nb, ch), F32)
    scratch = [buf(jt + CONV_LEFT + 1), buf(jt), pltpu.VMEM((nb, ch), F32),
               pltpu.VMEM((jt, nb, ch), F32), pltpu.VMEM((jt, nb, ch), F32),
               pltpu.SemaphoreType.DMA((2,)), pltpu.SemaphoreType.DMA((2,))]
    if combine:
        scratch += [buf(jt), buf(jt)]
    return pl.pallas_call(
        functools.partial(_lru_kernel, reverse, combine, nc_ctx, nc, rows),
        grid=(nc,), in_specs=[anyspec] * len(big) + [full(a) for a in consts], out_specs=anyspec,
        out_shape=jax.ShapeDtypeStruct(r4.shape, F32),
        scratch_shapes=scratch,
        compiler_params=_cparams("arbitrary"),
        name="lru_bwd" if reverse else "lru_fwd",
    )(*big, *consts)


def _block_diag_halves(w):
    nh, hd, _ = w.shape
    half = nh // 2
    eye = jnp.eye(half, dtype=w.dtype)
    wh = w.reshape(2, half, hd, hd)
    bd = jnp.einsum('bhij,hk->bhikj', wh, eye).reshape(2, half * hd, half * hd)
    return bd.astype(BF16)


def _post_kernel(n_groups, n_per, nsrc, off, cc, nsteps, *refs):
    srcs, refs = list(refs[:nsrc]), refs[nsrc:]
    (lr_hbm, yt_ref, ut_ref, gab_ref, gac_ref, shb_ref, shc_ref, scb_ref, scc_ref, d_ref, wg_ref, bg_ref,
     wo_ref, g2_ref, wr_ref, br_ref, x1_hbm, hm_hbm, meta_ref, xbuf, lbuf, x1buf, hmbuf, isem, osem) = refs
    st = _TileStreams([(srcs, xbuf), ([lr_hbm], lbuf)], [(x1_hbm, x1buf, off), (hm_hbm, hmbuf, off)],
                      isem, osem, cc, off, nsteps)
    st.begin()
    ct, slot = st.ct, st.slot
    x_in = xbuf[slot]
    nb, _, d = x_in.shape
    tm = nb * cc
    d_s5 = d_ref.shape[0]
    ys = yt_ref[...].reshape(d_s5, tm) + d_ref[...] * ut_ref[...].reshape(d_s5, tm)
    z = jax.nn.gelu(ys)
    s5t = z * jax.nn.sigmoid(jnp.dot(wg_ref[...], z.astype(BF16), preferred_element_type=F32) + bg_ref[...])
    mix = jnp.concatenate([s5t.T.astype(BF16), lbuf[slot].reshape(tm, -1).astype(BF16)], axis=-1)
    proj = jnp.dot(mix, wo_ref[...], preferred_element_type=F32).reshape(nb, cc, d)
    x1 = x_in + _mod_pick(ct, gab_ref, gac_ref) * proj
    x1buf[slot] = x1
    hm3 = _rms(x1, g2_ref[...]) * (1.0 + _mod_pick(ct, scb_ref, scc_ref)) + _mod_pick(ct, shb_ref, shc_ref)
    hmbuf[slot] = hm3
    hm = hm3.reshape(tm, d)

    hi = hm.astype(BF16)
    lo = (hm - hi.astype(F32)).astype(BF16)
    two = jnp.dot(jnp.concatenate([hi, lo], axis=-1), wr_ref[...], preferred_element_type=F32)
    logits = two[:, :LANES] + two[:, LANES:] + br_ref[...]

    lane = lax.broadcasted_iota(jnp.int32, (tm, LANES), 1).astype(F32)
    neg = jnp.float32(-jnp.inf)
    big = jnp.float32(LANES)
    lg = jnp.where(lane < n_groups, logits, neg)
    mg = jnp.max(lg, axis=-1, keepdims=True)
    g_p = 1.0 / jnp.sum(jnp.exp(lg - mg), axis=-1, keepdims=True)
    gidx = jnp.min(jnp.where(lg == mg, lane, big), axis=-1, keepdims=True)
    e0 = n_groups + n_per * gidx
    le = jnp.where(jnp.logical_and(lane >= e0, lane < e0 + n_per), logits, neg)
    m1 = jnp.max(le, axis=-1, keepdims=True)
    i1 = jnp.min(jnp.where(le == m1, lane, big), axis=-1, keepdims=True)
    le2 = jnp.where(lane == i1, neg, le)
    m2 = jnp.max(le2, axis=-1, keepdims=True)
    i2 = jnp.min(jnp.where(le2 == m2, lane, big), axis=-1, keepdims=True)
    r2 = jnp.exp(m2 - m1)
    w1 = g_p / (1.0 + r2)
    w2 = g_p * r2 / (1.0 + r2)
    j1 = i1 - e0
    j2 = i2 - e0
    jlo = jnp.minimum(j1, j2)
    jhi = jnp.maximum(j1, j2)
    wlo = jnp.where(j1 < j2, w1, w2)
    whi = jnp.where(j1 < j2, w2, w1)
    pair = jlo * (2 * n_per - 1 - jlo) * 0.5 + (jhi - jlo - 1.0)
    bucket = gidx * (n_per * (n_per - 1) // 2) + pair
    meta = jnp.where(lane == 0, bucket, jnp.where(lane == 1, wlo, jnp.where(lane == 2, whi, 0.0)))
    meta_ref[...] = meta.T[0:8, :]
    st.end()


def _post(lat_only, yt, ut, lr, xs, mods, layer, s5_d, w_glu, b_glu, w_out_bf, g2, wr2, br, n_groups, n_per, cc):
    q = S5_CHUNK
    nb, d = xs[0].shape[0], xs[0].shape[-1]
    n = sum(a.shape[1] for a in xs)
    nchunk = n // q
    ngrp, qh, _ = yt.shape
    nh = qh // q
    d_s5 = ngrp * nh
    d_lru = lr.shape[-1]
    off = 1 if lat_only else 0
    assert not (lat_only and len(xs) == 2)
    nct = nchunk // cc - off
    tm = nb * cc
    anyspec = pl.BlockSpec(memory_space=pl.ANY)
    tspec = pl.BlockSpec((ngrp, nh, tm), lambda ct, t: (0, t, ct + off))
    full = lambda a: pl.BlockSpec(a.shape, lambda ct, t: (0,) * a.ndim)
    consts = [s5_d.reshape(d_s5, 1), w_glu.T.astype(BF16), b_glu.reshape(d_s5, 1), w_out_bf, g2.reshape(1, d),
              wr2, br]
    buf = lambda w: pltpu.VMEM((2, nb, cc, w), F32)
    x1, hm, meta = pl.pallas_call(
        functools.partial(_post_kernel, n_groups, n_per, len(xs), off, cc, nct * q),
        grid=(nct, q),
        in_specs=([anyspec] * (len(xs) + 1) + [tspec, tspec]
                  + _mod_specs(layer, 2, d, nb) + _mod_specs(layer, 3, d, nb) + _mod_specs(layer, 4, d, nb)
                  + [full(a) for a in consts]),
        out_specs=[anyspec, anyspec, pl.BlockSpec((None, None, 8, tm), lambda ct, t: (t, ct, 0, 0))],
        out_shape=[jax.ShapeDtypeStruct((nb, nct * cc, q, d), F32), jax.ShapeDtypeStruct((nb, nct * cc, q, d), F32),
                   jax.ShapeDtypeStruct((q, nct, 8, tm), F32)],
        scratch_shapes=[buf(d), buf(d_lru), buf(d), buf(d),
                        pltpu.SemaphoreType.DMA((2,)), pltpu.SemaphoreType.DMA((2,))],
        compiler_params=_cparams("arbitrary", "arbitrary"),
        name="post_mixer",
    )(*[_view4(a) for a in xs], _view4(lr), yt, ut, mods, mods, mods, mods, mods, mods, *consts)
    n_out = nct * cc * q
    return x1.reshape(nb, n_out, d), hm.reshape(nb, n_out, d), meta


def _moe_kernel(n_rows, nt, tb_ref, src0_ref, srcn_ref, dst_ref, hm_hbm, wt_ref, w1a, w3a, w2a, w1b, w3b, w2b,
                y_hbm, xbuf, ybuf, gsem, ssem):
    i = pl.program_id(0)
    slot = lax.rem(i, 2)
    tmr = xbuf.shape[1]

    def valid(j):
        return tb_ref[2, jnp.clip(j, 0, nt - 1)] > 0

    def start_gathers(ids_ref, s):
        for r in range(tmr):
            pltpu.make_async_copy(hm_hbm.at[pl.ds(ids_ref[0, r], 1), :], xbuf.at[s, pl.ds(r, 1), :],
                                  gsem.at[s]).start(priority=r % 2)

    def start_scatters(s):
        for r in range(tmr):
            pltpu.make_async_copy(ybuf.at[s, pl.ds(r, 1), :], y_hbm.at[pl.ds(dst_ref[0, r], 1), :],
                                  ssem.at[s]).start(priority=r % 2)

    def wait_gathers(s):
        pltpu.make_async_copy(hm_hbm.at[pl.ds(0, tmr), :], xbuf.at[s], gsem.at[s]).wait()

    def wait_scatters(s):
        pltpu.make_async_copy(ybuf.at[s], y_hbm.at[pl.ds(0, tmr), :], ssem.at[s]).wait()

    @pl.when(i == 0)
    def _():
        xbuf[...] = jnp.zeros_like(xbuf)
        ybuf[...] = jnp.zeros_like(ybuf)
        for s in range((y_hbm.shape[0] - n_rows) // tmr):
            cp = pltpu.make_async_copy(ybuf.at[0], y_hbm.at[pl.ds(n_rows + s * tmr, tmr), :], ssem.at[0])
            cp.start()
            cp.wait()
        start_gathers(src0_ref, 0)

    def step(s):
        @pl.when(jnp.logical_and(i + 1 < nt, valid(i + 1)))
        def _():
            start_gathers(srcn_ref, 1 - s)

        @pl.when(jnp.logical_and(i >= 2, valid(i - 2)))
        def _():
            wait_scatters(s)

        @pl.when(valid(i))
        def _():
            wait_gathers(s)
            xb = xbuf[s].astype(BF16)
            wt = wt_ref[...]

            def expert(w1, w3, w2, gate):
                h1 = jnp.dot(xb, w1[...], preferred_element_type=F32)
                h3 = jnp.dot(xb, w3[...], preferred_element_type=F32)
                hid = (h1 * jax.nn.sigmoid(h1)) * h3
                return gate * jnp.dot(hid.astype(BF16), w2[...], preferred_element_type=F32)

            half = LANES // 2
            ybuf[s] = expert(w1a, w3a, w2a, wt[:, 0:1]) + expert(w1b, w3b, w2b, wt[:, half:half + 1])
            start_scatters(s)

        @pl.when(i == nt - 1)
        def _():
            @pl.when(jnp.logical_and(nt >= 2, valid(i - 1)))
            def _():
                wait_scatters(1 - s)

            @pl.when(valid(i))
            def _():
                wait_scatters(s)

    for s in range(2):
        pl.when(slot == s)(functools.partial(step, s))


def _moe(hm2, bucket, wlo, whi, tok_rows, w1_bf, w3_bf, w2_bf, expert_base, n_groups, n_per, tmr, spare_rows):
    n_rows, d = hm2.shape
    assert spare_rows >= 2 * tmr and spare_rows % tmr == 0
    t = bucket.shape[0]
    npairs = n_per * (n_per - 1) // 2
    nbuck = n_groups * npairs
    ntiles = t // tmr + nbuck

    _, s_tok, s_lo, s_hi = lax.sort((bucket, tok_rows, wlo, whi), num_keys=1, is_stable=False)
    counts = jnp.sum((bucket[None, :] == jnp.arange(nbuck, dtype=jnp.int32)[:, None]).astype(jnp.int32), axis=1)
    padded = ((counts + tmr - 1) // tmr) * tmr
    pend = jnp.cumsum(padded)
    cend = jnp.cumsum(counts)
    shift = (pend - padded) - (cend - counts)
    tile_start = jnp.arange(ntiles, dtype=jnp.int32) * tmr
    valid = (tile_start < pend[-1]).astype(jnp.int32)
    tb = jnp.sum((tile_start[:, None] >= pend[None, :]).astype(jnp.int32), axis=1)
    tb = jnp.minimum(tb, jnp.sum((pend < pend[-1]).astype(jnp.int32)))
    tb = jnp.minimum(tb, nbuck - 1)
    pos = (tile_start[:, None] + jnp.arange(tmr, dtype=jnp.int32)[None, :]) - shift[tb][:, None]
    real = jnp.logical_and(pos < cend[tb][:, None], valid[:, None] > 0)
    pos = jnp.clip(pos, 0, t - 1)
    tok = s_tok[pos]
    src = jnp.where(real, tok, 0)
    spare = n_rows + (jnp.arange(ntiles, dtype=jnp.int32) % 2)[:, None] * tmr + jnp.arange(tmr, dtype=jnp.int32)
    dst = jnp.where(real, tok, spare)
    gates = jnp.where(real[..., None], jnp.stack([s_lo, s_hi], axis=-1)[pos], 0.0)
    half = LANES // 2
    wts = jnp.repeat(gates.reshape(ntiles * tmr, 2), half, axis=1)
    grp, pr = tb // npairs, tb % npairs
    pairs = [(i, j) for i in range(n_per) for j in range(i + 1, n_per)]
    plo = jnp.array([p[0] for p in pairs], jnp.int32)[pr]
    phi = jnp.array([p[1] for p in pairs], jnp.int32)[pr]
    first = expert_base + grp * n_per
    tinfo = jnp.stack([first + plo, first + phi, valid])

    wspec = lambda a, row: pl.BlockSpec((None,) + a.shape[1:], lambda i, tb_: (tb_[row, i], 0, 0))
    ids = lambda f: pl.BlockSpec((None, 1, tmr), lambda i, tb_: (f(i), 0, 0), memory_space=pltpu.SMEM)
    grid_spec = pltpu.PrefetchScalarGridSpec(
        num_scalar_prefetch=1,
        grid=(ntiles,),
        in_specs=[ids(lambda i: 0), ids(lambda i: jnp.minimum(i + 1, ntiles - 1)), ids(lambda i: i),
                  pl.BlockSpec(memory_space=pl.ANY),
                  pl.BlockSpec((tmr, LANES), lambda i, tb_: (i, 0)),
                  wspec(w1_bf, 0), wspec(w3_bf, 0), wspec(w2_bf, 0),
                  wspec(w1_bf, 1), wspec(w3_bf, 1), wspec(w2_bf, 1)],
        out_specs=pl.BlockSpec(memory_space=pl.ANY),
        scratch_shapes=[pltpu.VMEM((2, tmr, d), F32), pltpu.VMEM((2, tmr, d), F32),
                        pltpu.SemaphoreType.DMA((2,)), pltpu.SemaphoreType.DMA((2,))],
    )
    src3 = src.reshape(ntiles, 1, tmr)
    return pl.pallas_call(
        functools.partial(_moe_kernel, n_rows, ntiles),
        grid_spec=grid_spec,
        out_shape=jax.ShapeDtypeStruct((n_rows + spare_rows, d), F32),
        compiler_params=_cparams("arbitrary"),
        name="moe_pairs",
    )(tinfo, src3, src3, dst.reshape(ntiles, 1, tmr), hm2, wts, w1_bf, w3_bf, w2_bf, w1_bf, w3_bf, w2_bf)


def _final_kernel(x_ref, y_ref, ga_ref, g_ref, o_ref):
    o_ref[...] = _rms(x_ref[...] + ga_ref[...] * y_ref[...], g_ref[...])


def _final(x1, y, mods, layer, final_g, tm):
    nb, seq, d = x1.shape
    tok = pl.BlockSpec((None, tm, d), lambda b, i: (b, i, 0))
    return pl.pallas_call(
        _final_kernel,
        grid=(nb, seq // tm),
        in_specs=[tok, pl.BlockSpec((tm, d), lambda b, i: (b * (seq // tm) + i, 0)),
                  _mod_spec_lat(layer, 5, d), pl.BlockSpec((1, d), lambda b, i: (0, 0))],
        out_specs=tok,
        out_shape=jax.ShapeDtypeStruct((nb, seq, d), F32),
        compiler_params=_cparams("parallel", "parallel"),
        name="final_norm",
    )(x1, y, mods, final_g.reshape(1, d))


def kernel(x, c, ctx, c_ctx, w_mod, b_mod, norm1_g, norm2_g, w_in, w_out, s5_a_re, s5_a_im, s5_log_dt, s5_b_re, s5_b_im, s5_c_re, s5_c_im, s5_d, s5_w_glu, s5_b_glu, lru_conv_w, lru_conv_b, lru_w_a, lru_b_a, lru_w_x, lru_b_x, lru_lam, moe_w_group, moe_b_group, moe_w_router, moe_b_router, moe_w1, moe_w3, moe_w2, final_g):
    nb, seq, d = x.shape
    n_ctx = ctx.shape[1]
    depth = w_mod.shape[0]
    n = n_ctx + seq
    d_s5 = s5_d.shape[-1]
    d_lru = lru_conv_b.shape[-1]
    ngrp, nh = s5_b_re.shape[2], s5_b_re.shape[4]
    n_groups, n_per = moe_w_router.shape[1], moe_w_router.shape[3]
    rows = seq // GRID_W
    tm = n_ctx
    q = S5_CHUNK
    nc_ctx, nc_lat = n_ctx // q, seq // q
    moe_rows = 256 if nb * seq >= 16384 else 32
    assert seq % GRID_W == 0 and seq % tm == 0 and n_ctx % GRID_W == 0 and seq % LRU_STEPS == 0
    assert d_s5 == ngrp * nh and q * nh == 2 * LANES and n_groups + n_groups * n_per <= LANES

    pad = (-(nb + 1)) % 8
    c_rows = jnp.concatenate([c, c_ctx[None, :], jnp.zeros((pad, d), F32)], axis=0)
    mods = _modulation(c_rows, w_mod, b_mod).reshape(depth, nb + 1 + pad, 1, 6 * d)

    s5w = jax.vmap(_s5_weights)(s5_a_re, s5_a_im, s5_log_dt, s5_b_re, s5_b_im, s5_c_re, s5_c_im)
    bdiag = jax.vmap(jax.vmap(_block_diag_halves))
    wa_bd, wx_bd = bdiag(lru_w_a), bdiag(lru_w_x)
    cneg_all = (-LRU_C * jax.nn.softplus(-lru_lam.astype(F32))).reshape(depth, 2, 1, d_lru)
    wr = jnp.concatenate([moe_w_group, moe_w_router.transpose(0, 2, 1, 3).reshape(depth, d, n_groups * n_per)], -1)
    wr = jnp.pad(wr.astype(F32), ((0, 0), (0, 0), (0, LANES - wr.shape[-1])))
    wr_hi = wr.astype(BF16)
    wr_lo = (wr - wr_hi.astype(F32)).astype(BF16)
    wr2_all = jnp.concatenate([jnp.concatenate([wr_hi, wr_lo], -1),
                               jnp.concatenate([wr_hi, jnp.zeros_like(wr_lo)], -1)], axis=1)
    br_all = jnp.concatenate([moe_b_group, moe_b_router.reshape(depth, -1)], axis=-1)
    br_all = jnp.pad(br_all.astype(F32), ((0, 0), (0, LANES - br_all.shape[-1]))).reshape(depth, 1, LANES)
    w_out_bf = w_out.astype(BF16)
    n_exp = n_groups * n_per
    moe_bf = [w.astype(BF16).reshape((depth * n_exp,) + w.shape[2:]) for w in (moe_w1, moe_w3, moe_w2)]

    cc = nc_ctx
    xs = [ctx, x]
    y = None
    for l in range(depth):
        need_ctx = l < depth - 1
        xs, ut, r, qg = _inproj(xs, y, mods, l, norm1_g[l], w_in[l], ngrp, d_s5, d_lru, cc)

        yt = _s5_scan(ut, *s5w, l, nb, cc, nc_ctx, nc_lat)

        view = lambda a: a.reshape(nb, n // GRID_W, GRID_W, d_lru)
        cw, cb = lru_conv_w[l].astype(F32), lru_conv_b[l].reshape(1, d_lru).astype(F32)
        gates = [(wa_bd[l, dd], lru_b_a[l, dd].reshape(1, d_lru),
                  wx_bd[l, dd], lru_b_x[l, dd].reshape(1, d_lru), cneg_all[l, dd]) for dd in range(2)]
        h_b = _lru_pass(True, view(r), None, None, cw, cb, *gates[1], n_ctx, rows)
        lr = _lru_pass(False, view(r), view(qg), h_b, cw, cb, *gates[0], n_ctx, rows).reshape(nb, n, d_lru)

        x1, hm, meta = _post(not need_ctx, yt, ut, lr, xs, mods, l, s5_d[l], s5_w_glu[l], s5_b_glu[l],
                             w_out_bf[l], norm2_g[l], wr2_all[l], br_all[l], n_groups, n_per, cc)

        n_out = x1.shape[1]
        nct = n_out // (cc * q)
        meta_t = meta.reshape(q, nct, 8, nb, cc).transpose(2, 3, 1, 4, 0).reshape(8, nb * n_out)
        bucket = meta_t[0].astype(jnp.int32)
        y = _moe(hm.reshape(nb * n_out, d), bucket, meta_t[1], meta_t[2], jnp.arange(nb * n_out, dtype=jnp.int32),
                 *moe_bf, l * n_exp, n_groups, n_per, moe_rows, n_out)
        xs = [x1]
        if need_ctx:
            y = y.reshape(nb + 1, n_out, d)
    return _final(xs[0], y, mods, depth - 1, final_g, 4 * tm if seq % (4 * tm) == 0 else tm)
```

```python
import functools

import jax
import jax.numpy as jnp
from jax import lax
from jax.experimental import pallas as pl
from jax.experimental.pallas import tpu as pltpu

F32 = jnp.float32
BF16 = jnp.bfloat16
HIGHEST = lax.Precision.HIGHEST

EPS = 1e-6
GRID_W = 64
LRU_C = 8.0
CONV_LEFT = 2
S5_CHUNK = 16
S5_ROW_PAD = 8
LRU_STEPS = 32
LANES = 128
VMEM_LIMIT = 56 * 1024 * 1024


def _cparams(*sem):
    return pltpu.CompilerParams(dimension_semantics=sem, vmem_limit_bytes=VMEM_LIMIT)


def _rms(x, g):
    return x * lax.rsqrt(jnp.mean(x * x, axis=-1, keepdims=True) + EPS) * g


def _mod_kernel(c_ref, w_ref, b_ref, o_ref):
    c = c_ref[...]
    s = c * jax.nn.sigmoid(c)
    o_ref[...] = jnp.dot(s, w_ref[...], preferred_element_type=F32, precision=HIGHEST) + b_ref[...]


def _modulation(c_rows, w_mod, b_mod):
    depth, d, n6 = w_mod.shape
    rows = c_rows.shape[0]
    tn = n6 // 4
    return pl.pallas_call(
        _mod_kernel,
        grid=(depth, n6 // tn),
        in_specs=[pl.BlockSpec((rows, d), lambda l, j: (0, 0)),
                  pl.BlockSpec((None, d, tn), lambda l, j: (l, 0, j)),
                  pl.BlockSpec((None, 1, tn), lambda l, j: (l, 0, j))],
        out_specs=pl.BlockSpec((None, rows, tn), lambda l, j: (l, 0, j)),
        out_shape=jax.ShapeDtypeStruct((depth, rows, n6), F32),
        compiler_params=_cparams("parallel", "parallel"),
        name="modulation",
    )(c_rows, w_mod, b_mod.reshape(depth, 1, n6))


def _mod_specs(layer, part, d, nb):
    return [pl.BlockSpec((None, nb, 1, d), lambda ct, t: (layer, 0, 0, part)),
            pl.BlockSpec((None, None, 1, d), lambda ct, t: (layer, nb, 0, part))]


def _mod_pick(ct, batch_ref, ctx_ref):
    return jnp.where(ct == 0, ctx_ref[...][None], batch_ref[...])


def _mod_spec_lat(layer, part, d):
    return pl.BlockSpec((None, None, 1, d), lambda b, i: (layer, b, 0, part))


def _view4(a):
    nb, n, w = a.shape
    return a.reshape(nb, n // S5_CHUNK, S5_CHUNK, w)


def _tile_slice(a4, nb, ct, t, cc, off=0):
    return a4.at[pl.ds(0, nb), pl.ds((ct - off) * cc, cc), t, :]


class _TileStreams:
    def __init__(self, ins, outs, isem, osem, cc, in_off, nsteps):
        self.ins, self.outs, self.isem, self.osem = ins, outs, isem, osem
        self.cc, self.in_off, self.nsteps = cc, in_off, nsteps
        self.nb = ins[0][1].shape[1]
        self.ct = pl.program_id(0) + in_off
        self.t = pl.program_id(1)
        self.q = pl.num_programs(1)
        self.s = pl.program_id(0) * self.q + self.t
        self.slot = lax.rem(self.s, 2)

    def _start_inputs(self, ct, t, sl):
        for srcs, buf in self.ins:
            if len(srcs) == 2:
                @pl.when(ct == 0)
                def _():
                    pltpu.make_async_copy(_tile_slice(srcs[0], self.nb, 0, t, self.cc), buf.at[sl], self.isem.at[sl]).start()

                @pl.when(ct > 0)
                def _():
                    pltpu.make_async_copy(_tile_slice(srcs[1], self.nb, ct, t, self.cc, 1), buf.at[sl],
                                          self.isem.at[sl]).start()
            else:
                pltpu.make_async_copy(_tile_slice(srcs[0], self.nb, ct, t, self.cc), buf.at[sl], self.isem.at[sl]).start()

    def _wait_outputs(self, sl):
        for arr, buf, _ in self.outs:
            pltpu.make_async_copy(buf.at[sl], _tile_slice(arr, self.nb, 0, 0, self.cc), self.osem.at[sl]).wait()

    def begin(self):
        ct, t, s, slot = self.ct, self.t, self.s, self.slot

        @pl.when(s == 0)
        def _():
            self._start_inputs(ct, t, 0)

        @pl.when(s + 1 < self.nsteps)
        def _():
            wrap = t == self.q - 1
            self._start_inputs(jnp.where(wrap, ct + 1, ct), jnp.where(wrap, 0, t + 1), 1 - slot)

        for srcs, buf in self.ins:
            pltpu.make_async_copy(_tile_slice(srcs[0], self.nb, 0, 0, self.cc), buf.at[slot], self.isem.at[slot]).wait()

        @pl.when(s >= 2)
        def _():
            self._wait_outputs(slot)

    def end(self):
        for arr, buf, off in self.outs:
            pltpu.make_async_copy(buf.at[self.slot], _tile_slice(arr, self.nb, self.ct, self.t, self.cc, off),
                                  self.osem.at[self.slot]).start()

        @pl.when(self.s == self.nsteps - 1)
        def _():
            if self.nsteps >= 2:
                self._wait_outputs(1 - self.slot)
            self._wait_outputs(self.slot)


def _inproj_kernel(has_y, nsrc, ngrp, cc, nsteps, *refs):
    srcs, refs = list(refs[:nsrc]), refs[nsrc:]
    if has_y:
        y_hbm, gab_ref, gac_ref = refs[:3]
        refs = refs[3:]
    shb_ref, shc_ref, scb_ref, scc_ref, g_ref, wu_ref, wrq_ref = refs[:7]
    refs = refs[7:]
    if has_y:
        xo_hbm, ut_ref, r_hbm, q_hbm, xbuf, rbuf, qbuf, isem, osem, ybuf, xobuf = refs
        ins = [(srcs, xbuf), ([y_hbm], ybuf)]
        outs = [(xo_hbm, xobuf, 0), (r_hbm, rbuf, 0), (q_hbm, qbuf, 0)]
    else:
        ut_ref, r_hbm, q_hbm, xbuf, rbuf, qbuf, isem, osem = refs
        ins = [(srcs, xbuf)]
        outs = [(r_hbm, rbuf, 0), (q_hbm, qbuf, 0)]
    st = _TileStreams(ins, outs, isem, osem, cc, 0, nsteps)
    st.begin()
    ct, slot = st.ct, st.slot
    x = xbuf[slot]
    if has_y:
        x = x + _mod_pick(ct, gab_ref, gac_ref) * ybuf[slot]
        xobuf[slot] = x
    nb, _, d = x.shape
    d_lru = rbuf.shape[-1]
    h = _rms(x, g_ref[...]) * (1.0 + _mod_pick(ct, scb_ref, scc_ref)) + _mod_pick(ct, shb_ref, shc_ref)
    hb = h.reshape(nb * cc, d).astype(BF16)
    ut = lax.dot_general(wu_ref[...], hb, (((1,), (1,)), ((), ())), preferred_element_type=F32)
    ut_ref[...] = ut.reshape(ngrp, ut.shape[0] // ngrp, nb * cc)
    p = jnp.dot(hb, wrq_ref[...], preferred_element_type=F32)
    rbuf[slot] = p[:, :d_lru].reshape(nb, cc, d_lru)
    qbuf[slot] = p[:, d_lru:].reshape(nb, cc, d_lru)
    st.end()


def _inproj(xs, y, mods, layer, g1, w_in, ngrp, d_s5, d_lru, cc):
    q = S5_CHUNK
    nb, d = xs[0].shape[0], xs[0].shape[-1]
    n = sum(a.shape[1] for a in xs)
    nchunk = n // q
    has_y = y is not None
    nh = d_s5 // ngrp
    nsteps = (nchunk // cc) * q
    anyspec = pl.BlockSpec(memory_space=pl.ANY)
    in_specs = [anyspec] * len(xs)
    args = [_view4(a) for a in xs]
    if has_y:
        in_specs += [anyspec] + _mod_specs(layer - 1, 5, d, nb)
        args += [_view4(y), mods, mods]
    wu = w_in[:, :d_s5].T.astype(BF16)
    wrq = w_in[:, d_s5:].astype(BF16)
    full = lambda a: pl.BlockSpec(a.shape, lambda ct, t: (0,) * a.ndim)
    in_specs += _mod_specs(layer, 0, d, nb) + _mod_specs(layer, 1, d, nb) + [
        pl.BlockSpec((1, d), lambda ct, t: (0, 0)), full(wu), full(wrq)]
    args += [mods, mods, mods, mods, g1.reshape(1, d), wu, wrq]
    out_specs = [pl.BlockSpec((ngrp, nh, nb * cc), lambda ct, t: (0, t, ct)), anyspec, anyspec]
    out_shape = [jax.ShapeDtypeStruct((ngrp, q * nh, nb * nchunk), F32),
                 jax.ShapeDtypeStruct((nb, nchunk, q, d_lru), F32),
                 jax.ShapeDtypeStruct((nb, nchunk, q, d_lru), F32)]
    buf = lambda w: pltpu.VMEM((2, nb, cc, w), F32)
    scratch = [buf(d), buf(d_lru), buf(d_lru), pltpu.SemaphoreType.DMA((2,)), pltpu.SemaphoreType.DMA((2,))]
    if has_y:
        out_specs = [anyspec] + out_specs
        out_shape = [jax.ShapeDtypeStruct((nb, nchunk, q, d), F32)] + out_shape
        scratch += [buf(d), buf(d)]
    res = pl.pallas_call(
        functools.partial(_inproj_kernel, has_y, len(xs), ngrp, cc, nsteps),
        grid=(nchunk // cc, q), in_specs=in_specs, out_specs=out_specs, out_shape=out_shape,
        scratch_shapes=scratch,
        compiler_params=_cparams("arbitrary", "arbitrary"),
        name="inproj",
    )(*args)
    if has_y:
        xs = [res[0].reshape(nb, n, d)]
        res = res[1:]
    return [xs, res[0], res[1].reshape(nb, n, d_lru), res[2].reshape(nb, n, d_lru)]


def _s5_weights(a_re, a_im, log_dt, b_re, b_im, c_re, c_im):
    q = S5_CHUNK
    lam = lax.complex(a_re.astype(F32), a_im.astype(F32))
    dt = jnp.exp(log_dt.astype(F32))[..., None]
    a_bar = jnp.exp(lam * dt)
    bmat = lax.complex(b_re.astype(F32), b_im.astype(F32))
    b_bar = ((a_bar - 1) / lam)[..., None] * bmat
    cmat = lax.complex(c_re.astype(F32), c_im.astype(F32))
    k = jnp.arange(q + 1, dtype=F32)
    apow = jnp.exp((lam * dt)[..., None] * k)
    ngrp, nst = a_re.shape[1], a_re.shape[2]
    nh = b_re.shape[-1]

    kern = jnp.einsum('dgop,dgpk,dgpi->dgkoi', cmat, apow[..., :q], b_bar, precision=HIGHEST).real
    sig = jnp.arange(q)[:, None]
    tau = jnp.arange(q)[None, :]
    lag_f = jnp.clip(tau - sig, 0, q - 1)
    lag_b = jnp.clip(sig - tau, 0, q - 1)
    tf = jnp.where((sig <= tau)[None, :, :, None, None], kern[0][:, lag_f], 0.0)
    tb = jnp.where((sig >= tau)[None, :, :, None, None], kern[1][:, lag_b], 0.0)
    toep = (tf + tb).transpose(0, 1, 4, 2, 3).reshape(ngrp, q * nh, q * nh)

    win_f = jnp.einsum('gps,gpi->gsip', apow[0][..., :q][..., ::-1], b_bar[0])
    win_b = jnp.einsum('gps,gpi->gsip', apow[1][..., :q], b_bar[1])
    win = jnp.concatenate([win_f.real, win_b.real, win_f.imag, win_b.imag], axis=-1)
    win = win.reshape(ngrp, q * nh, 4 * nst)

    cf = jnp.einsum('gop,gpt->gpto', cmat[0], apow[0][..., 1:])
    cb = jnp.einsum('gop,gpt->gpto', cmat[1], apow[1][..., 1:][..., ::-1])
    z = jnp.zeros_like(cf.real)
    wof = jnp.concatenate([cf.real, z, -cf.imag, z], axis=1).reshape(ngrp, 4 * nst, q * nh)
    wob = jnp.concatenate([z, cb.real, z, -cb.imag], axis=1).reshape(ngrp, 4 * nst, q * nh)

    aq = apow[..., q]
    a16 = jnp.stack([jnp.concatenate([aq[0].real, aq[1].real], -1),
                     jnp.concatenate([aq[0].imag, aq[1].imag], -1)], axis=1)
    w1 = jnp.concatenate([toep, win], axis=-1)
    return w1.astype(BF16), wof.astype(BF16), wob.astype(BF16), a16.astype(F32)


def _s5_kernel(nb, cc, nc_ctx, nc_lat, u_ref, w1_ref, wof_ref, wob_ref, a_ref, y_ref,
               yrow_ref, sre_ref, sim_ref, fre_ref, fim_ref, bre_ref, bim_ref):
    qh, m = u_ref.shape
    ns2 = a_ref.shape[-1]
    mt = nb * cc
    pitch = cc + S5_ROW_PAD
    w1 = w1_ref[...]

    def intra(i, c):
        r0 = pl.multiple_of(i * mt, mt)
        u = u_ref[:, pl.ds(r0, mt)].T.astype(BF16)
        res = jnp.dot(u, w1, preferred_element_type=F32)
        yrow_ref[pl.ds(r0, mt), :] = res[:, :qh]
        p0 = pl.multiple_of(i * (nb * pitch), 8)
        for b in range(nb):
            sre_ref[pl.ds(p0 + b * pitch, cc), :] = res[b * cc:(b + 1) * cc, qh:qh + ns2]
            sim_ref[pl.ds(p0 + b * pitch, cc), :] = res[b * cc:(b + 1) * cc, qh + ns2:]
        return c

    lax.fori_loop(0, m // mt, intra, 0)

    are = jnp.broadcast_to(a_ref[0:1, :], (nb, ns2))
    aim = jnp.broadcast_to(a_ref[1:2, :], (nb, ns2))
    fwd_lane = lax.broadcasted_iota(jnp.int32, (nb, ns2), 1) < ns2 // 2

    def rows(c):
        tile = lax.div(c, cc)
        return pl.ds(tile * (nb * pitch) + (c - tile * cc), nb, stride=pitch)

    def scan(base, n, carry):
        def step(k, hc):
            hre, him = hc
            rf = rows(base + k)
            rb = rows(base + n - 1 - k)
            fre_ref[rf, :] = hre
            fim_ref[rf, :] = him
            bre_ref[rb, :] = hre
            bim_ref[rb, :] = him
            sre = jnp.where(fwd_lane, sre_ref[rf, :], sre_ref[rb, :])
            sim = jnp.where(fwd_lane, sim_ref[rf, :], sim_ref[rb, :])
            return (are * hre - aim * him + sre, are * him + aim * hre + sim)
        return lax.fori_loop(0, n, step, carry)

    zero = jnp.zeros((nb, ns2), F32)
    carry = scan(0, nc_ctx, (zero, zero))
    scan(nc_ctx, nc_lat, carry)

    wof = wof_ref[...]
    wob = wob_ref[...]

    def inter(i, c):
        r0 = pl.multiple_of(i * mt, mt)
        p0 = pl.multiple_of(i * (nb * pitch), 8)
        tile_rows = lambda ref: jnp.concatenate([ref[pl.ds(p0 + b * pitch, cc), :] for b in range(nb)], axis=0)
        hf = jnp.concatenate([tile_rows(fre_ref), tile_rows(fim_ref)], axis=1).astype(BF16)
        hb = jnp.concatenate([tile_rows(bre_ref), tile_rows(bim_ref)], axis=1).astype(BF16)
        y = (yrow_ref[pl.ds(r0, mt), :] + jnp.dot(hf, wof, preferred_element_type=F32)
             + jnp.dot(hb, wob, preferred_element_type=F32))
        y_ref[:, pl.ds(r0, mt)] = y.T
        return c

    lax.fori_loop(0, m // mt, inter, 0)


def _s5_scan(u_t, w1, wof, wob, a16, layer, nb, cc, nc_ctx, nc_lat):
    ngrp, qh, m = u_t.shape
    ns2 = a16.shape[-1]
    grp = lambda a: pl.BlockSpec((None,) + a.shape[1:], lambda g: (g, 0, 0))
    lgrp = lambda a: pl.BlockSpec((None, None) + a.shape[2:], lambda g: (layer, g, 0, 0))
    return pl.pallas_call(
        functools.partial(_s5_kernel, nb, cc, nc_ctx, nc_lat),
        grid=(ngrp,),
        in_specs=[grp(u_t), lgrp(w1), lgrp(wof), lgrp(wob), lgrp(a16)],
        out_specs=grp(u_t),
        out_shape=jax.ShapeDtypeStruct(u_t.shape, F32),
        scratch_shapes=[pltpu.VMEM((m, qh), F32)] + [pltpu.VMEM((m // cc * (cc + S5_ROW_PAD), ns2), F32)] * 6,
        compiler_params=_cparams("parallel"),
        name="s5_chunked",
    )(u_t, w1, wof, wob, a16)


def _lru_chunk(s, reverse, nc_ctx, nc):
    if not reverse:
        return s
    return jnp.where(s < nc_ctx, nc_ctx - 1 - s, nc - 1 - (s - nc_ctx))


def _lru_kernel(reverse, combine, nc_ctx, nc, rows, *refs):
    if combine:
        (r_hbm, q_hbm, ho_hbm, cw_ref, cb_ref, wa_ref, ba_ref, wx_ref, bx_ref, cn_ref, o_hbm,
         rbuf, obuf, h_ref, a_s, b_s, isem, osem, qbuf, hbuf) = refs
    else:
        (r_hbm, cw_ref, cb_ref, wa_ref, ba_ref, wx_ref, bx_ref, cn_ref, o_hbm,
         rbuf, obuf, h_ref, a_s, b_s, isem, osem) = refs
    _, jt, nb, ch = obuf.shape
    gw = r_hbm.shape[2]
    cblk = (nc_ctx * jt) // gw
    ncol = jt // rows
    s = pl.program_id(0)
    slot = lax.rem(s, 2)

    def chunk_of(step):
        return _lru_chunk(jnp.clip(step, 0, nc - 1), reverse, nc_ctx, nc)

    def main_copies(hbm, buf, row0, c, sl, sem, fn):
        @pl.when(c < nc_ctx)
        def _():
            j0 = c * jt
            for b in range(nb):
                fn(hbm.at[b, j0 // gw, pl.ds(j0 % gw, jt), :], buf.at[sl, pl.ds(row0, jt), b, :], sem.at[sl])

        @pl.when(c >= nc_ctx)
        def _():
            w0 = (c - nc_ctx) * ncol
            for b in range(nb):
                for k in range(ncol):
                    fn(hbm.at[b, pl.ds(cblk, rows), w0 + k, :],
                       buf.at[sl, pl.ds(row0 + k * rows, rows), b, :], sem.at[sl])

    def halo_copies(c, sl, fn):
        first = jnp.logical_or(c == 0, c == nc_ctx)
        last = jnp.logical_or(c == nc_ctx - 1, c == nc - 1)
        in_ctx = c < nc_ctx

        @pl.when(jnp.logical_and(in_ctx, jnp.logical_not(first)))
        def _():
            j = c * jt - CONV_LEFT
            for b in range(nb):
                fn(r_hbm.at[b, j // gw, pl.ds(j % gw, CONV_LEFT), :], rbuf.at[sl, pl.ds(0, CONV_LEFT), b, :],
                   isem.at[sl])

        @pl.when(jnp.logical_and(in_ctx, jnp.logical_not(last)))
        def _():
            j = c * jt + jt
            for b in range(nb):
                fn(r_hbm.at[b, j // gw, pl.ds(j % gw, 1), :], rbuf.at[sl, pl.ds(jt + CONV_LEFT, 1), b, :],
                   isem.at[sl])

        @pl.when(jnp.logical_and(jnp.logical_not(in_ctx), jnp.logical_not(first)))
        def _():
            w = (c - nc_ctx) * ncol - 1
            for b in range(nb):
                fn(r_hbm.at[b, pl.ds(cblk + rows - CONV_LEFT, CONV_LEFT), w, :],
                   rbuf.at[sl, pl.ds(0, CONV_LEFT), b, :], isem.at[sl])

        @pl.when(jnp.logical_and(jnp.logical_not(in_ctx), jnp.logical_not(last)))
        def _():
            w = (c - nc_ctx) * ncol + ncol
            for b in range(nb):
                fn(r_hbm.at[b, pl.ds(cblk, 1), w, :], rbuf.at[sl, pl.ds(jt + CONV_LEFT, 1), b, :], isem.at[sl])

    def loads(c, sl, fn):
        main_copies(r_hbm, rbuf, CONV_LEFT, c, sl, isem, fn)
        halo_copies(c, sl, fn)
        if combine:
            main_copies(q_hbm, qbuf, 0, c, sl, isem, fn)
            main_copies(ho_hbm, hbuf, 0, c, sl, isem, fn)

    def stores(c, sl, fn):
        main_copies(o_hbm, obuf, 0, c, sl, osem, lambda hbm, buf, sem: fn(buf, hbm, sem))

    start = lambda src, dst, sem: pltpu.make_async_copy(src, dst, sem).start()
    wait = lambda src, dst, sem: pltpu.make_async_copy(src, dst, sem).wait()

    c = chunk_of(s)
    first = jnp.logical_or(c == 0, c == nc_ctx)
    last = jnp.logical_or(c == nc_ctx - 1, c == nc - 1)

    @pl.when(s == 0)
    def _():
        h_ref[...] = jnp.zeros_like(h_ref)
        rbuf[...] = jnp.zeros_like(rbuf)
        loads(c, 0, start)

    @pl.when(s + 1 < nc)
    def _():
        loads(chunk_of(s + 1), 1 - slot, start)

    loads(c, slot, wait)

    @pl.when(s >= 2)
    def _():
        stores(chunk_of(s - 2), slot, wait)

    rb = rbuf[slot]
    prev = jnp.where(first, 0.0, rb[0:CONV_LEFT])
    nxt = jnp.where(last, 0.0, rb[jt + CONV_LEFT:jt + CONV_LEFT + 1])
    xe = jnp.concatenate([prev, rb[CONV_LEFT:jt + CONV_LEFT], nxt], axis=0)
    cw = cw_ref[...]
    xc = cb_ref[...].reshape(1, 1, ch)
    for k in range(cw.shape[0]):
        xc = xc + cw[k:k + 1, :].reshape(1, 1, ch) * xe[k:k + jt]
    xc2 = xc.reshape(jt * nb, ch)

    xb = xc2.astype(BF16)
    nblk, wb = wa_ref.shape[0], wa_ref.shape[1]
    za = jnp.concatenate([jnp.dot(xb[:, i * wb:(i + 1) * wb], wa_ref[i], preferred_element_type=F32)
                          for i in range(nblk)], axis=-1)
    zx = jnp.concatenate([jnp.dot(xb[:, i * wb:(i + 1) * wb], wx_ref[i], preferred_element_type=F32)
                          for i in range(nblk)], axis=-1)
    rg = jax.nn.sigmoid(za + ba_ref[...])
    ig = jax.nn.sigmoid(zx + bx_ref[...])
    log_a = cn_ref[...] * rg
    a = jnp.exp(log_a)
    bb = jnp.sqrt(1.0 - a * a) * (ig * xc2)
    a_s[...] = a.reshape(jt, nb, ch)
    b_s[...] = bb.reshape(jt, nb, ch)

    def step(t, h):
        tt = jt - 1 - t if reverse else t
        h = a_s[tt] * h + b_s[tt]
        if combine:
            obuf[slot, tt] = (h + hbuf[slot, tt]) * jax.nn.gelu(qbuf[slot, tt])
        else:
            obuf[slot, tt] = h
        return h

    h_ref[...] = lax.fori_loop(0, jt, step, h_ref[...])

    stores(c, slot, start)

    @pl.when(s == nc - 1)
    def _():
        if nc >= 2:
            stores(chunk_of(s - 1), 1 - slot, wait)
        stores(c, slot, wait)


def _lru_pass(reverse, r4, q4, h_other4, conv_w, conv_b, wa, ba, wx, bx, cneg, n_ctx, rows):
    nb, nblk, gw, ch = r4.shape
    jt = LRU_STEPS
    nc, nc_ctx = (nblk * gw) // jt, n_ctx // jt
    combine = h_other4 is not None
    assert gw % jt == 0 and n_ctx % gw == 0 and jt % rows == 0 and rows >= CONV_LEFT
    anyspec = pl.BlockSpec(memory_space=pl.ANY)
    full = lambda a: pl.BlockSpec(a.shape, lambda s: (0,) * a.ndim)
    consts = [conv_w, conv_b, wa, ba, wx, bx, cneg]
    big = [r4, q4, h_other4] if combine else [r4]
    buf = lambda n: pltpu.VMEM((2, n, nb, ch), F32)
    scratch = [buf(jt + CONV_LEFT + 1), buf(jt), pltpu.VMEM((nb, ch), F32),
               pltpu.VMEM((jt, nb, ch), F32), pltpu.VMEM((jt, nb, ch), F32),
               pltpu.SemaphoreType.DMA((2,)), pltpu.SemaphoreType.DMA((2,))]
    if combine:
        scratch += [buf(jt), buf(jt)]
    return pl.pallas_call(
        functools.partial(_lru_kernel, reverse, combine, nc_ctx, nc, rows),
        grid=(nc,), in_specs=[anyspec] * len(big) + [full(a) for a in consts], out_specs=anyspec,
        out_shape=jax.ShapeDtypeStruct(r4.shape, F32),
        scratch_shapes=scratch,
        compiler_params=_cparams("arbitrary"),
        name="lru_bwd" if reverse else "lru_fwd",
    )(*big, *consts)


def _block_diag_halves(w):
    nh, hd, _ = w.shape
    half = nh // 2
    eye = jnp.eye(half, dtype=w.dtype)
    wh = w.reshape(2, half, hd, hd)
    bd = jnp.einsum('bhij,hk->bhikj', wh, eye).reshape(2, half * hd, half * hd)
    return bd.astype(BF16)


def _post_kernel(n_groups, n_per, nsrc, off, cc, nsteps, *refs):
    srcs, refs = list(refs[:nsrc]), refs[nsrc:]
    (lr_hbm, yt_ref, ut_ref, gab_ref, gac_ref, shb_ref, shc_ref, scb_ref, scc_ref, d_ref, wg_ref, bg_ref,
     wo_ref, g2_ref, wr_ref, br_ref, x1_hbm, hm_hbm, meta_ref, xbuf, lbuf, x1buf, hmbuf, isem, osem) = refs
    st = _TileStreams([(srcs, xbuf), ([lr_hbm], lbuf)], [(x1_hbm, x1buf, off), (hm_hbm, hmbuf, off)],
                      isem, osem, cc, off, nsteps)
    st.begin()
    ct, slot = st.ct, st.slot
    x_in = xbuf[slot]
    nb, _, d = x_in.shape
    tm = nb * cc
    d_s5 = d_ref.shape[0]
    ys = yt_ref[...].reshape(d_s5, tm) + d_ref[...] * ut_ref[...].reshape(d_s5, tm)
    z = jax.nn.gelu(ys)
    s5t = z * jax.nn.sigmoid(jnp.dot(wg_ref[...], z.astype(BF16), preferred_element_type=F32) + bg_ref[...])
    mix = jnp.concatenate([s5t.T.astype(BF16), lbuf[slot].reshape(tm, -1).astype(BF16)], axis=-1)
    proj = jnp.dot(mix, wo_ref[...], preferred_element_type=F32).reshape(nb, cc, d)
    x1 = x_in + _mod_pick(ct, gab_ref, gac_ref) * proj
    x1buf[slot] = x1
    hm3 = _rms(x1, g2_ref[...]) * (1.0 + _mod_pick(ct, scb_ref, scc_ref)) + _mod_pick(ct, shb_ref, shc_ref)
    hmbuf[slot] = hm3
    hm = hm3.reshape(tm, d)

    hi = hm.astype(BF16)
    lo = (hm - hi.astype(F32)).astype(BF16)
    two = jnp.dot(jnp.concatenate([hi, lo], axis=-1), wr_ref[...], preferred_element_type=F32)
    logits = two[:, :LANES] + two[:, LANES:] + br_ref[...]

    lane = lax.broadcasted_iota(jnp.int32, (tm, LANES), 1).astype(F32)
    neg = jnp.float32(-jnp.inf)
    big = jnp.float32(LANES)
    lg = jnp.where(lane < n_groups, logits, neg)
    mg = jnp.max(lg, axis=-1, keepdims=True)
    g_p = 1.0 / jnp.sum(jnp.exp(lg - mg), axis=-1, keepdims=True)
    gidx = jnp.min(jnp.where(lg == mg, lane, big), axis=-1, keepdims=True)
    e0 = n_groups + n_per * gidx
    le = jnp.where(jnp.logical_and(lane >= e0, lane < e0 + n_per), logits, neg)
    m1 = jnp.max(le, axis=-1, keepdims=True)
    i1 = jnp.min(jnp.where(le == m1, lane, big), axis=-1, keepdims=True)
    le2 = jnp.where(lane == i1, neg, le)
    m2 = jnp.max(le2, axis=-1, keepdims=True)
    i2 = jnp.min(jnp.where(le2 == m2, lane, big), axis=-1, keepdims=True)
    r2 = jnp.exp(m2 - m1)
    w1 = g_p / (1.0 + r2)
    w2 = g_p * r2 / (1.0 + r2)
    j1 = i1 - e0
    j2 = i2 - e0
    jlo = jnp.minimum(j1, j2)
    jhi = jnp.maximum(j1, j2)
    wlo = jnp.where(j1 < j2, w1, w2)
    whi = jnp.where(j1 < j2, w2, w1)
    pair = jlo * (2 * n_per - 1 - jlo) * 0.5 + (jhi - jlo - 1.0)
    bucket = gidx * (n_per * (n_per - 1) // 2) + pair
    meta = jnp.where(lane == 0, bucket, jnp.where(lane == 1, wlo, jnp.where(lane == 2, whi, 0.0)))
    meta_ref[...] = meta.T[0:8, :]
    st.end()


def _post(lat_only, yt, ut, lr, xs, mods, layer, s5_d, w_glu, b_glu, w_out_bf, g2, wr2, br, n_groups, n_per, cc):
    q = S5_CHUNK
    nb, d = xs[0].shape[0], xs[0].shape[-1]
    n = sum(a.shape[1] for a in xs)
    nchunk = n // q
    ngrp, qh, _ = yt.shape
    nh = qh // q
    d_s5 = ngrp * nh
    d_lru = lr.shape[-1]
    off = 1 if lat_only else 0
    assert not (lat_only and len(xs) == 2)
    nct = nchunk // cc - off
    tm = nb * cc
    anyspec = pl.BlockSpec(memory_space=pl.ANY)
    tspec = pl.BlockSpec((ngrp, nh, tm), lambda ct, t: (0, t, ct + off))
    full = lambda a: pl.BlockSpec(a.shape, lambda ct, t: (0,) * a.ndim)
    consts = [s5_d.reshape(d_s5, 1), w_glu.T.astype(BF16), b_glu.reshape(d_s5, 1), w_out_bf, g2.reshape(1, d),
              wr2, br]
    buf = lambda w: pltpu.VMEM((2, nb, cc, w), F32)
    x1, hm, meta = pl.pallas_call(
        functools.partial(_post_kernel, n_groups, n_per, len(xs), off, cc, nct * q),
        grid=(nct, q),
        in_specs=([anyspec] * (len(xs) + 1) + [tspec, tspec]
                  + _mod_specs(layer, 2, d, nb) + _mod_specs(layer, 3, d, nb) + _mod_specs(layer, 4, d, nb)
                  + [full(a) for a in consts]),
        out_specs=[anyspec, anyspec, pl.BlockSpec((None, None, 8, tm), lambda ct, t: (t, ct, 0, 0))],
        out_shape=[jax.ShapeDtypeStruct((nb, nct * cc, q, d), F32), jax.ShapeDtypeStruct((nb, nct * cc, q, d), F32),
                   jax.ShapeDtypeStruct((q, nct, 8, tm), F32)],
        scratch_shapes=[buf(d), buf(d_lru), buf(d), buf(d),
                        pltpu.SemaphoreType.DMA((2,)), pltpu.SemaphoreType.DMA((2,))],
        compiler_params=_cparams("arbitrary", "arbitrary"),
        name="post_mixer",
    )(*[_view4(a) for a in xs], _view4(lr), yt, ut, mods, mods, mods, mods, mods, mods, *consts)
    n_out = nct * cc * q
    return x1.reshape(nb, n_out, d), hm.reshape(nb, n_out, d), meta


def _moe_kernel(n_rows, nt, tb_ref, src0_ref, srcn_ref, dst_ref, hm_hbm, wt_ref, w1a, w3a, w2a, w1b, w3b, w2b,
                y_hbm, xbuf, ybuf, gsem, ssem):
    i = pl.program_id(0)
    slot = lax.rem(i, 2)
    tmr = xbuf.shape[1]

    def valid(j):
        return tb_ref[2, jnp.clip(j, 0, nt - 1)] > 0

    def start_gathers(ids_ref, s):
        for r in range(tmr):
            pltpu.make_async_copy(hm_hbm.at[pl.ds(ids_ref[0, r], 1), :], xbuf.at[s, pl.ds(r, 1), :],
                                  gsem.at[s]).start(priority=r % 2)

    def start_scatters(s):
        for r in range(tmr):
            pltpu.make_async_copy(ybuf.at[s, pl.ds(r, 1), :], y_hbm.at[pl.ds(dst_ref[0, r], 1), :],
                                  ssem.at[s]).start(priority=r % 2)

    def wait_gathers(s):
        pltpu.make_async_copy(hm_hbm.at[pl.ds(0, tmr), :], xbuf.at[s], gsem.at[s]).wait()

    def wait_scatters(s):
        pltpu.make_async_copy(ybuf.at[s], y_hbm.at[pl.ds(0, tmr), :], ssem.at[s]).wait()

    @pl.when(i == 0)
    def _():
        xbuf[...] = jnp.zeros_like(xbuf)
        ybuf[...] = jnp.zeros_like(ybuf)
        for s in range((y_hbm.shape[0] - n_rows) // tmr):
            cp = pltpu.make_async_copy(ybuf.at[0], y_hbm.at[pl.ds(n_rows + s * tmr, tmr), :], ssem.at[0])
            cp.start()
            cp.wait()
        start_gathers(src0_ref, 0)

    def step(s):
        @pl.when(jnp.logical_and(i + 1 < nt, valid(i + 1)))
        def _():
            start_gathers(srcn_ref, 1 - s)

        @pl.when(jnp.logical_and(i >= 2, valid(i - 2)))
        def _():
            wait_scatters(s)

        @pl.when(valid(i))
        def _():
            wait_gathers(s)
            xb = xbuf[s].astype(BF16)
            wt = wt_ref[...]

            def expert(w1, w3, w2, gate):
                h1 = jnp.dot(xb, w1[...], preferred_element_type=F32)
                h3 = jnp.dot(xb, w3[...], preferred_element_type=F32)
                hid = (h1 * jax.nn.sigmoid(h1)) * h3
                return gate * jnp.dot(hid.astype(BF16), w2[...], preferred_element_type=F32)

            half = LANES // 2
            ybuf[s] = expert(w1a, w3a, w2a, wt[:, 0:1]) + expert(w1b, w3b, w2b, wt[:, half:half + 1])
            start_scatters(s)

        @pl.when(i == nt - 1)
        def _():
            @pl.when(jnp.logical_and(nt >= 2, valid(i - 1)))
            def _():
                wait_scatters(1 - s)

            @pl.when(valid(i))
            def _():
                wait_scatters(s)

    for s in range(2):
        pl.when(slot == s)(functools.partial(step, s))


def _moe(hm2, bucket, wlo, whi, tok_rows, w1_bf, w3_bf, w2_bf, expert_base, n_groups, n_per, tmr, spare_rows):
    n_rows, d = hm2.shape
    assert spare_rows >= 2 * tmr and spare_rows % tmr == 0
    t = bucket.shape[0]
    npairs = n_per * (n_per - 1) // 2
    nbuck = n_groups * npairs
    ntiles = t // tmr + nbuck

    _, s_tok, s_lo, s_hi = lax.sort((bucket, tok_rows, wlo, whi), num_keys=1, is_stable=False)
    counts = jnp.sum((bucket[None, :] == jnp.arange(nbuck, dtype=jnp.int32)[:, None]).astype(jnp.int32), axis=1)
    padded = ((counts + tmr - 1) // tmr) * tmr
    pend = jnp.cumsum(padded)
    cend = jnp.cumsum(counts)
    shift = (pend - padded) - (cend - counts)
    tile_start = jnp.arange(ntiles, dtype=jnp.int32) * tmr
    valid = (tile_start < pend[-1]).astype(jnp.int32)
    tb = jnp.sum((tile_start[:, None] >= pend[None, :]).astype(jnp.int32), axis=1)
    tb = jnp.minimum(tb, jnp.sum((pend < pend[-1]).astype(jnp.int32)))
    tb = jnp.minimum(tb, nbuck - 1)
    pos = (tile_start[:, None] + jnp.arange(tmr, dtype=jnp.int32)[None, :]) - shift[tb][:, None]
    real = jnp.logical_and(pos < cend[tb][:, None], valid[:, None] > 0)
    pos = jnp.clip(pos, 0, t - 1)
    tok = s_tok[pos]
    src = jnp.where(real, tok, 0)
    spare = n_rows + (jnp.arange(ntiles, dtype=jnp.int32) % 2)[:, None] * tmr + jnp.arange(tmr, dtype=jnp.int32)
    dst = jnp.where(real, tok, spare)
    g_lo = jnp.where(real, s_lo[pos], 0.0).reshape(-1, 1)
    g_hi = jnp.where(real, s_hi[pos], 0.0).reshape(-1, 1)
    half = LANES // 2
    wts = jnp.concatenate([jnp.broadcast_to(g_lo, (ntiles * tmr, half)),
                           jnp.broadcast_to(g_hi, (ntiles * tmr, half))], axis=1)
    grp, pr = tb // npairs, tb % npairs
    pairs = [(i, j) for i in range(n_per) for j in range(i + 1, n_per)]
    plo = jnp.array([p[0] for p in pairs], jnp.int32)[pr]
    phi = jnp.array([p[1] for p in pairs], jnp.int32)[pr]
    first = expert_base + grp * n_per
    tinfo = jnp.stack([first + plo, first + phi, valid])

    wspec = lambda a, row: pl.BlockSpec((None,) + a.shape[1:], lambda i, tb_: (tb_[row, i], 0, 0))
    ids = lambda f: pl.BlockSpec((None, 1, tmr), lambda i, tb_: (f(i), 0, 0), memory_space=pltpu.SMEM)
    grid_spec = pltpu.PrefetchScalarGridSpec(
        num_scalar_prefetch=1,
        grid=(ntiles,),
        in_specs=[ids(lambda i: 0), ids(lambda i: jnp.minimum(i + 1, ntiles - 1)), ids(lambda i: i),
                  pl.BlockSpec(memory_space=pl.ANY),
                  pl.BlockSpec((tmr, LANES), lambda i, tb_: (i, 0)),
                  wspec(w1_bf, 0), wspec(w3_bf, 0), wspec(w2_bf, 0),
                  wspec(w1_bf, 1), wspec(w3_bf, 1), wspec(w2_bf, 1)],
        out_specs=pl.BlockSpec(memory_space=pl.ANY),
        scratch_shapes=[pltpu.VMEM((2, tmr, d), F32), pltpu.VMEM((2, tmr, d), F32),
                        pltpu.SemaphoreType.DMA((2,)), pltpu.SemaphoreType.DMA((2,))],
    )
    src3 = src.reshape(ntiles, 1, tmr)
    return pl.pallas_call(
        functools.partial(_moe_kernel, n_rows, ntiles),
        grid_spec=grid_spec,
        out_shape=jax.ShapeDtypeStruct((n_rows + spare_rows, d), F32),
        compiler_params=_cparams("arbitrary"),
        name="moe_pairs",
    )(tinfo, src3, src3, dst.reshape(ntiles, 1, tmr), hm2, wts, w1_bf, w3_bf, w2_bf, w1_bf, w3_bf, w2_bf)


def _final_kernel(x_ref, y_ref, ga_ref, g_ref, o_ref):
    o_ref[...] = _rms(x_ref[...] + ga_ref[...] * y_ref[...], g_ref[...])


def _final(x1, y, mods, layer, final_g, tm):
    nb, seq, d = x1.shape
    tok = pl.BlockSpec((None, tm, d), lambda b, i: (b, i, 0))
    return pl.pallas_call(
        _final_kernel,
        grid=(nb, seq // tm),
        in_specs=[tok, pl.BlockSpec((tm, d), lambda b, i: (b * (seq // tm) + i, 0)),
                  _mod_spec_lat(layer, 5, d), pl.BlockSpec((1, d), lambda b, i: (0, 0))],
        out_specs=tok,
        out_shape=jax.ShapeDtypeStruct((nb, seq, d), F32),
        compiler_params=_cparams("parallel", "parallel"),
        name="final_norm",
    )(x1, y, mods, final_g.reshape(1, d))


def kernel(x, c, ctx, c_ctx, w_mod, b_mod, norm1_g, norm2_g, w_in, w_out, s5_a_re, s5_a_im, s5_log_dt, s5_b_re, s5_b_im, s5_c_re, s5_c_im, s5_d, s5_w_glu, s5_b_glu, lru_conv_w, lru_conv_b, lru_w_a, lru_b_a, lru_w_x, lru_b_x, lru_lam, moe_w_group, moe_b_group, moe_w_router, moe_b_router, moe_w1, moe_w3, moe_w2, final_g):
    nb, seq, d = x.shape
    n_ctx = ctx.shape[1]
    depth = w_mod.shape[0]
    n = n_ctx + seq
    d_s5 = s5_d.shape[-1]
    d_lru = lru_conv_b.shape[-1]
    ngrp, nh = s5_b_re.shape[2], s5_b_re.shape[4]
    n_groups, n_per = moe_w_router.shape[1], moe_w_router.shape[3]
    rows = seq // GRID_W
    tm = n_ctx
    q = S5_CHUNK
    nc_ctx, nc_lat = n_ctx // q, seq // q
    moe_rows = 256 if nb * seq >= 16384 else 32
    assert seq % GRID_W == 0 and seq % tm == 0 and n_ctx % GRID_W == 0 and seq % LRU_STEPS == 0
    assert d_s5 == ngrp * nh and q * nh == 2 * LANES and n_groups + n_groups * n_per <= LANES

    pad = (-(nb + 1)) % 8
    c_rows = jnp.concatenate([c, c_ctx[None, :], jnp.zeros((pad, d), F32)], axis=0)
    mods = _modulation(c_rows, w_mod, b_mod).reshape(depth, nb + 1 + pad, 1, 6 * d)

    s5w = jax.vmap(_s5_weights)(s5_a_re, s5_a_im, s5_log_dt, s5_b_re, s5_b_im, s5_c_re, s5_c_im)
    bdiag = jax.vmap(jax.vmap(_block_diag_halves))
    wa_bd, wx_bd = bdiag(lru_w_a), bdiag(lru_w_x)
    cneg_all = (-LRU_C * jax.nn.softplus(-lru_lam.astype(F32))).reshape(depth, 2, 1, d_lru)
    wr = jnp.concatenate([moe_w_group, moe_w_router.transpose(0, 2, 1, 3).reshape(depth, d, n_groups * n_per)], -1)
    wr = jnp.pad(wr.astype(F32), ((0, 0), (0, 0), (0, LANES - wr.shape[-1])))
    wr_hi = wr.astype(BF16)
    wr_lo = (wr - wr_hi.astype(F32)).astype(BF16)
    wr2_all = jnp.concatenate([jnp.concatenate([wr_hi, wr_lo], -1),
                               jnp.concatenate([wr_hi, jnp.zeros_like(wr_lo)], -1)], axis=1)
    br_all = jnp.concatenate([moe_b_group, moe_b_router.reshape(depth, -1)], axis=-1)
    br_all = jnp.pad(br_all.astype(F32), ((0, 0), (0, LANES - br_all.shape[-1]))).reshape(depth, 1, LANES)
    w_out_bf = w_out.astype(BF16)
    n_exp = n_groups * n_per
    moe_bf = [w.astype(BF16).reshape((depth * n_exp,) + w.shape[2:]) for w in (moe_w1, moe_w3, moe_w2)]

    cc = nc_ctx
    xs = [ctx, x]
    y = None
    for l in range(depth):
        need_ctx = l < depth - 1
        xs, ut, r, qg = _inproj(xs, y, mods, l, norm1_g[l], w_in[l], ngrp, d_s5, d_lru, cc)

        yt = _s5_scan(ut, *s5w, l, nb, cc, nc_ctx, nc_lat)

        view = lambda a: a.reshape(nb, n // GRID_W, GRID_W, d_lru)
        cw, cb = lru_conv_w[l].astype(F32), lru_conv_b[l].reshape(1, d_lru).astype(F32)
        gates = [(wa_bd[l, dd], lru_b_a[l, dd].reshape(1, d_lru),
                  wx_bd[l, dd], lru_b_x[l, dd].reshape(1, d_lru), cneg_all[l, dd]) for dd in range(2)]
        h_b = _lru_pass(True, view(r), None, None, cw, cb, *gates[1], n_ctx, rows)
        lr = _lru_pass(False, view(r), view(qg), h_b, cw, cb, *gates[0], n_ctx, rows).reshape(nb, n, d_lru)

        x1, hm, meta = _post(not need_ctx, yt, ut, lr, xs, mods, l, s5_d[l], s5_w_glu[l], s5_b_glu[l],
                             w_out_bf[l], norm2_g[l], wr2_all[l], br_all[l], n_groups, n_per, cc)

        n_out = x1.shape[1]
        nct = n_out // (cc * q)
        meta_t = meta.reshape(q, nct, 8, nb, cc).transpose(2, 3, 1, 4, 0).reshape(8, nb * n_out)
        bucket = meta_t[0].astype(jnp.int32)
        y = _moe(hm.reshape(nb * n_out, d), bucket, meta_t[1], meta_t[2], jnp.arange(nb * n_out, dtype=jnp.int32),
                 *moe_bf, l * n_exp, n_groups, n_per, moe_rows, n_out)
        xs = [x1]
        if need_ctx:
            y = y.reshape(nb + 1, n_out, d)
    return _final(xs[0], y, mods, depth - 1, final_g, 4 * tm if seq % (4 * tm) == 0 else tm)
```

```python
import functools

import jax
import jax.numpy as jnp
from jax import lax
from jax.experimental import pallas as pl
from jax.experimental.pallas import tpu as pltpu

F32 = jnp.float32
BF16 = jnp.bfloat16
HIGHEST = lax.Precision.HIGHEST

EPS = 1e-6
GRID_W = 64
LRU_C = 8.0
CONV_LEFT = 2
S5_CHUNK = 16
S5_ROW_PAD = 4
LRU_STEPS = 32
LANES = 128
VMEM_LIMIT = 56 * 1024 * 1024


def _cparams(*sem):
    return pltpu.CompilerParams(dimension_semantics=sem, vmem_limit_bytes=VMEM_LIMIT)


def _rms(x, g):
    return x * lax.rsqrt(jnp.mean(x * x, axis=-1, keepdims=True) + EPS) * g


def _mod_kernel(c_ref, w_ref, b_ref, o_ref):
    c = c_ref[...]
    s = c * jax.nn.sigmoid(c)
    o_ref[...] = jnp.dot(s, w_ref[...], preferred_element_type=F32, precision=HIGHEST) + b_ref[...]


def _modulation(c_rows, w_mod, b_mod):
    depth, d, n6 = w_mod.shape
    rows = c_rows.shape[0]
    tn = n6 // 4
    return pl.pallas_call(
        _mod_kernel,
        grid=(depth, n6 // tn),
        in_specs=[pl.BlockSpec((rows, d), lambda l, j: (0, 0)),
                  pl.BlockSpec((None, d, tn), lambda l, j: (l, 0, j)),
                  pl.BlockSpec((None, 1, tn), lambda l, j: (l, 0, j))],
        out_specs=pl.BlockSpec((None, rows, tn), lambda l, j: (l, 0, j)),
        out_shape=jax.ShapeDtypeStruct((depth, rows, n6), F32),
        compiler_params=_cparams("parallel", "parallel"),
        name="modulation",
    )(c_rows, w_mod, b_mod.reshape(depth, 1, n6))


def _mod_specs(layer, part, d, nb):
    return [pl.BlockSpec((None, nb, 1, d), lambda ct, t: (layer, 0, 0, part)),
            pl.BlockSpec((None, None, 1, d), lambda ct, t: (layer, nb, 0, part))]


def _mod_pick(ct, batch_ref, ctx_ref):
    return jnp.where(ct == 0, ctx_ref[...][None], batch_ref[...])


def _mod_spec_lat(layer, part, d):
    return pl.BlockSpec((None, None, 1, d), lambda b, i: (layer, b, 0, part))


def _view4(a):
    nb, n, w = a.shape
    return a.reshape(nb, n // S5_CHUNK, S5_CHUNK, w)


def _tile_slice(a4, nb, ct, t, cc, off=0):
    return a4.at[pl.ds(0, nb), pl.ds((ct - off) * cc, cc), t, :]


class _TileStreams:
    def __init__(self, ins, outs, isem, osem, cc, in_off, nsteps):
        self.ins, self.outs, self.isem, self.osem = ins, outs, isem, osem
        self.cc, self.in_off, self.nsteps = cc, in_off, nsteps
        self.nb = ins[0][1].shape[1]
        self.ct = pl.program_id(0) + in_off
        self.t = pl.program_id(1)
        self.q = pl.num_programs(1)
        self.s = pl.program_id(0) * self.q + self.t
        self.slot = lax.rem(self.s, 2)

    def _start_inputs(self, ct, t, sl):
        for srcs, buf in self.ins:
            if len(srcs) == 2:
                @pl.when(ct == 0)
                def _():
                    pltpu.make_async_copy(_tile_slice(srcs[0], self.nb, 0, t, self.cc), buf.at[sl], self.isem.at[sl]).start()

                @pl.when(ct > 0)
                def _():
                    pltpu.make_async_copy(_tile_slice(srcs[1], self.nb, ct, t, self.cc, 1), buf.at[sl],
                                          self.isem.at[sl]).start()
            else:
                pltpu.make_async_copy(_tile_slice(srcs[0], self.nb, ct, t, self.cc), buf.at[sl], self.isem.at[sl]).start()

    def _wait_outputs(self, sl):
        for arr, buf, _ in self.outs:
            pltpu.make_async_copy(buf.at[sl], _tile_slice(arr, self.nb, 0, 0, self.cc), self.osem.at[sl]).wait()

    def begin(self):
        ct, t, s, slot = self.ct, self.t, self.s, self.slot

        @pl.when(s == 0)
        def _():
            self._start_inputs(ct, t, 0)

        @pl.when(s + 1 < self.nsteps)
        def _():
            wrap = t == self.q - 1
            self._start_inputs(jnp.where(wrap, ct + 1, ct), jnp.where(wrap, 0, t + 1), 1 - slot)

        for srcs, buf in self.ins:
            pltpu.make_async_copy(_tile_slice(srcs[0], self.nb, 0, 0, self.cc), buf.at[slot], self.isem.at[slot]).wait()

        @pl.when(s >= 2)
        def _():
            self._wait_outputs(slot)

    def end(self):
        for arr, buf, off in self.outs:
            pltpu.make_async_copy(buf.at[self.slot], _tile_slice(arr, self.nb, self.ct, self.t, self.cc, off),
                                  self.osem.at[self.slot]).start()

        @pl.when(self.s == self.nsteps - 1)
        def _():
            if self.nsteps >= 2:
                self._wait_outputs(1 - self.slot)
            self._wait_outputs(self.slot)


def _inproj_kernel(has_y, nsrc, ngrp, cc, nsteps, *refs):
    srcs, refs = list(refs[:nsrc]), refs[nsrc:]
    if has_y:
        y_hbm, gab_ref, gac_ref = refs[:3]
        refs = refs[3:]
    shb_ref, shc_ref, scb_ref, scc_ref, g_ref, wu_ref, wrq_ref = refs[:7]
    refs = refs[7:]
    if has_y:
        xo_hbm, ut_ref, r_hbm, q_hbm, xbuf, rbuf, qbuf, isem, osem, ybuf, xobuf = refs
        ins = [(srcs, xbuf), ([y_hbm], ybuf)]
        outs = [(xo_hbm, xobuf, 0), (r_hbm, rbuf, 0), (q_hbm, qbuf, 0)]
    else:
        ut_ref, r_hbm, q_hbm, xbuf, rbuf, qbuf, isem, osem = refs
        ins = [(srcs, xbuf)]
        outs = [(r_hbm, rbuf, 0), (q_hbm, qbuf, 0)]
    st = _TileStreams(ins, outs, isem, osem, cc, 0, nsteps)
    st.begin()
    ct, slot = st.ct, st.slot
    x = xbuf[slot]
    if has_y:
        x = x + _mod_pick(ct, gab_ref, gac_ref) * ybuf[slot]
        xobuf[slot] = x
    nb, _, d = x.shape
    d_lru = rbuf.shape[-1]
    h = _rms(x, g_ref[...]) * (1.0 + _mod_pick(ct, scb_ref, scc_ref)) + _mod_pick(ct, shb_ref, shc_ref)
    hb = h.reshape(nb * cc, d).astype(BF16)
    ut = lax.dot_general(wu_ref[...], hb, (((1,), (1,)), ((), ())), preferred_element_type=F32)
    ut_ref[...] = ut.reshape(ngrp, ut.shape[0] // ngrp, nb * cc)
    p = jnp.dot(hb, wrq_ref[...], preferred_element_type=F32)
    rbuf[slot] = p[:, :d_lru].reshape(nb, cc, d_lru)
    qbuf[slot] = p[:, d_lru:].reshape(nb, cc, d_lru)
    st.end()


def _inproj(xs, y, mods, layer, g1, w_in, ngrp, d_s5, d_lru, cc):
    q = S5_CHUNK
    nb, d = xs[0].shape[0], xs[0].shape[-1]
    n = sum(a.shape[1] for a in xs)
    nchunk = n // q
    has_y = y is not None
    nh = d_s5 // ngrp
    nsteps = (nchunk // cc) * q
    anyspec = pl.BlockSpec(memory_space=pl.ANY)
    in_specs = [anyspec] * len(xs)
    args = [_view4(a) for a in xs]
    if has_y:
        in_specs += [anyspec] + _mod_specs(layer - 1, 5, d, nb)
        args += [_view4(y), mods, mods]
    wu = w_in[:, :d_s5].T.astype(BF16)
    wrq = w_in[:, d_s5:].astype(BF16)
    full = lambda a: pl.BlockSpec(a.shape, lambda ct, t: (0,) * a.ndim)
    in_specs += _mod_specs(layer, 0, d, nb) + _mod_specs(layer, 1, d, nb) + [
        pl.BlockSpec((1, d), lambda ct, t: (0, 0)), full(wu), full(wrq)]
    args += [mods, mods, mods, mods, g1.reshape(1, d), wu, wrq]
    out_specs = [pl.BlockSpec((ngrp, nh, nb * cc), lambda ct, t: (0, t, ct)), anyspec, anyspec]
    out_shape = [jax.ShapeDtypeStruct((ngrp, q * nh, nb * nchunk), F32),
                 jax.ShapeDtypeStruct((nb, nchunk, q, d_lru), F32),
                 jax.ShapeDtypeStruct((nb, nchunk, q, d_lru), F32)]
    buf = lambda w: pltpu.VMEM((2, nb, cc, w), F32)
    scratch = [buf(d), buf(d_lru), buf(d_lru), pltpu.SemaphoreType.DMA((2,)), pltpu.SemaphoreType.DMA((2,))]
    if has_y:
        out_specs = [anyspec] + out_specs
        out_shape = [jax.ShapeDtypeStruct((nb, nchunk, q, d), F32)] + out_shape
        scratch += [buf(d), buf(d)]
    res = pl.pallas_call(
        functools.partial(_inproj_kernel, has_y, len(xs), ngrp, cc, nsteps),
        grid=(nchunk // cc, q), in_specs=in_specs, out_specs=out_specs, out_shape=out_shape,
        scratch_shapes=scratch,
        compiler_params=_cparams("arbitrary", "arbitrary"),
        name="inproj",
    )(*args)
    if has_y:
        xs = [res[0].reshape(nb, n, d)]
        res = res[1:]
    return [xs, res[0], res[1].reshape(nb, n, d_lru), res[2].reshape(nb, n, d_lru)]


def _s5_weights(a_re, a_im, log_dt, b_re, b_im, c_re, c_im):
    q = S5_CHUNK
    lam = lax.complex(a_re.astype(F32), a_im.astype(F32))
    dt = jnp.exp(log_dt.astype(F32))[..., None]
    a_bar = jnp.exp(lam * dt)
    bmat = lax.complex(b_re.astype(F32), b_im.astype(F32))
    b_bar = ((a_bar - 1) / lam)[..., None] * bmat
    cmat = lax.complex(c_re.astype(F32), c_im.astype(F32))
    k = jnp.arange(q + 1, dtype=F32)
    apow = jnp.exp((lam * dt)[..., None] * k)
    ngrp, nst = a_re.shape[1], a_re.shape[2]
    nh = b_re.shape[-1]

    kern = jnp.einsum('dgop,dgpk,dgpi->dgkoi', cmat, apow[..., :q], b_bar, precision=HIGHEST).real
    sig = jnp.arange(q)[:, None]
    tau = jnp.arange(q)[None, :]
    lag_f = jnp.clip(tau - sig, 0, q - 1)
    lag_b = jnp.clip(sig - tau, 0, q - 1)
    tf = jnp.where((sig <= tau)[None, :, :, None, None], kern[0][:, lag_f], 0.0)
    tb = jnp.where((sig >= tau)[None, :, :, None, None], kern[1][:, lag_b], 0.0)
    toep = (tf + tb).transpose(0, 1, 4, 2, 3).reshape(ngrp, q * nh, q * nh)

    win_f = jnp.einsum('gps,gpi->gsip', apow[0][..., :q][..., ::-1], b_bar[0])
    win_b = jnp.einsum('gps,gpi->gsip', apow[1][..., :q], b_bar[1])
    win = jnp.concatenate([win_f.real, win_b.real, win_f.imag, win_b.imag], axis=-1)
    win = win.reshape(ngrp, q * nh, 4 * nst)

    cf = jnp.einsum('gop,gpt->gpto', cmat[0], apow[0][..., 1:])
    cb = jnp.einsum('gop,gpt->gpto', cmat[1], apow[1][..., 1:][..., ::-1])
    z = jnp.zeros_like(cf.real)
    wof = jnp.concatenate([cf.real, z, -cf.imag, z], axis=1).reshape(ngrp, 4 * nst, q * nh)
    wob = jnp.concatenate([z, cb.real, z, -cb.imag], axis=1).reshape(ngrp, 4 * nst, q * nh)

    aq = apow[..., q]
    a16 = jnp.stack([jnp.concatenate([aq[0].real, aq[1].real], -1),
                     jnp.concatenate([aq[0].imag, aq[1].imag], -1)], axis=1)
    w1 = jnp.concatenate([toep, win], axis=-1)
    return w1.astype(BF16), wof.astype(BF16), wob.astype(BF16), a16.astype(F32)


def _s5_kernel(nb, cc, nc_ctx, nc_lat, u_ref, w1_ref, wof_ref, wob_ref, a_ref, y_ref,
               yrow_ref, sre_ref, sim_ref, fre_ref, fim_ref, bre_ref, bim_ref):
    qh, m = u_ref.shape
    ns2 = a_ref.shape[-1]
    mt = nb * cc
    pitch = cc + S5_ROW_PAD
    w1 = w1_ref[...]

    for i in range(m // mt):
        r0, p0 = i * mt, i * nb * pitch
        u = u_ref[:, r0:r0 + mt].T.astype(BF16)
        res = jnp.dot(u, w1, preferred_element_type=F32)
        yrow_ref[r0:r0 + mt, :] = res[:, :qh]
        for b in range(nb):
            sre_ref[p0 + b * pitch:p0 + b * pitch + cc, :] = res[b * cc:(b + 1) * cc, qh:qh + ns2]
            sim_ref[p0 + b * pitch:p0 + b * pitch + cc, :] = res[b * cc:(b + 1) * cc, qh + ns2:]

    are = jnp.broadcast_to(a_ref[0:1, :], (nb, ns2))
    aim = jnp.broadcast_to(a_ref[1:2, :], (nb, ns2))
    fwd_lane = lax.broadcasted_iota(jnp.int32, (nb, ns2), 1) < ns2 // 2

    def rows(c):
        tile = lax.div(c, cc)
        return pl.ds(tile * (nb * pitch) + (c - tile * cc), nb, stride=pitch)

    def scan(base, n, carry):
        def step(k, hc):
            hre, him = hc
            rf = rows(base + k)
            rb = rows(base + n - 1 - k)
            fre_ref[rf, :] = hre
            fim_ref[rf, :] = him
            bre_ref[rb, :] = hre
            bim_ref[rb, :] = him
            sre = jnp.where(fwd_lane, sre_ref[rf, :], sre_ref[rb, :])
            sim = jnp.where(fwd_lane, sim_ref[rf, :], sim_ref[rb, :])
            return (are * hre - aim * him + sre, are * him + aim * hre + sim)
        return lax.fori_loop(0, n, step, carry)

    zero = jnp.zeros((nb, ns2), F32)
    carry = scan(0, nc_ctx, (zero, zero))
    scan(nc_ctx, nc_lat, carry)

    wof = wof_ref[...]
    wob = wob_ref[...]

    for i in range(m // mt):
        r0, p0 = i * mt, i * nb * pitch
        tile_rows = lambda ref: jnp.concatenate(
            [ref[p0 + b * pitch:p0 + b * pitch + cc, :] for b in range(nb)], axis=0)
        hf = jnp.concatenate([tile_rows(fre_ref), tile_rows(fim_ref)], axis=1).astype(BF16)
        hb = jnp.concatenate([tile_rows(bre_ref), tile_rows(bim_ref)], axis=1).astype(BF16)
        y = (yrow_ref[r0:r0 + mt, :] + jnp.dot(hf, wof, preferred_element_type=F32)
             + jnp.dot(hb, wob, preferred_element_type=F32))
        y_ref[:, r0:r0 + mt] = y.T


def _s5_scan(u_t, w1, wof, wob, a16, layer, nb, cc, nc_ctx, nc_lat):
    ngrp, qh, m = u_t.shape
    ns2 = a16.shape[-1]
    grp = lambda a: pl.BlockSpec((None,) + a.shape[1:], lambda g: (g, 0, 0))
    lgrp = lambda a: pl.BlockSpec((None, None) + a.shape[2:], lambda g: (layer, g, 0, 0))
    return pl.pallas_call(
        functools.partial(_s5_kernel, nb, cc, nc_ctx, nc_lat),
        grid=(ngrp,),
        in_specs=[grp(u_t), lgrp(w1), lgrp(wof), lgrp(wob), lgrp(a16)],
        out_specs=grp(u_t),
        out_shape=jax.ShapeDtypeStruct(u_t.shape, F32),
        scratch_shapes=[pltpu.VMEM((m, qh), F32)] + [pltpu.VMEM((m // cc * (cc + S5_ROW_PAD), ns2), F32)] * 6,
        compiler_params=_cparams("parallel"),
        name="s5_chunked",
    )(u_t, w1, wof, wob, a16)


def _lru_chunk(s, reverse, nc_ctx, nc):
    if not reverse:
        return s
    return jnp.where(s < nc_ctx, nc_ctx - 1 - s, nc - 1 - (s - nc_ctx))


def _lru_kernel(reverse, combine, nc_ctx, nc, rows, *refs):
    if combine:
        (r_hbm, q_hbm, ho_hbm, cw_ref, cb_ref, wa_ref, ba_ref, wx_ref, bx_ref, cn_ref, o_hbm,
         rbuf, obuf, h_ref, a_s, b_s, isem, osem, qbuf, hbuf) = refs
    else:
        (r_hbm, cw_ref, cb_ref, wa_ref, ba_ref, wx_ref, bx_ref, cn_ref, o_hbm,
         rbuf, obuf, h_ref, a_s, b_s, isem, osem) = refs
    _, jt, nb, ch = obuf.shape
    gw = r_hbm.shape[2]
    cblk = (nc_ctx * jt) // gw
    ncol = jt // rows
    s = pl.program_id(0)
    slot = lax.rem(s, 2)

    def chunk_of(step):
        return _lru_chunk(jnp.clip(step, 0, nc - 1), reverse, nc_ctx, nc)

    def main_copies(hbm, buf, row0, c, sl, sem, fn):
        @pl.when(c < nc_ctx)
        def _():
            j0 = c * jt
            for b in range(nb):
                fn(hbm.at[b, j0 // gw, pl.ds(j0 % gw, jt), :], buf.at[sl, pl.ds(row0, jt), b, :], sem.at[sl])

        @pl.when(c >= nc_ctx)
        def _():
            w0 = (c - nc_ctx) * ncol
            for b in range(nb):
                for k in range(ncol):
                    fn(hbm.at[b, pl.ds(cblk, rows), w0 + k, :],
                       buf.at[sl, pl.ds(row0 + k * rows, rows), b, :], sem.at[sl])

    def halo_copies(c, sl, fn):
        first = jnp.logical_or(c == 0, c == nc_ctx)
        last = jnp.logical_or(c == nc_ctx - 1, c == nc - 1)
        in_ctx = c < nc_ctx

        @pl.when(jnp.logical_and(in_ctx, jnp.logical_not(first)))
        def _():
            j = c * jt - CONV_LEFT
            for b in range(nb):
                fn(r_hbm.at[b, j // gw, pl.ds(j % gw, CONV_LEFT), :], rbuf.at[sl, pl.ds(0, CONV_LEFT), b, :],
                   isem.at[sl])

        @pl.when(jnp.logical_and(in_ctx, jnp.logical_not(last)))
        def _():
            j = c * jt + jt
            for b in range(nb):
                fn(r_hbm.at[b, j // gw, pl.ds(j % gw, 1), :], rbuf.at[sl, pl.ds(jt + CONV_LEFT, 1), b, :],
                   isem.at[sl])

        @pl.when(jnp.logical_and(jnp.logical_not(in_ctx), jnp.logical_not(first)))
        def _():
            w = (c - nc_ctx) * ncol - 1
            for b in range(nb):
                fn(r_hbm.at[b, pl.ds(cblk + rows - CONV_LEFT, CONV_LEFT), w, :],
                   rbuf.at[sl, pl.ds(0, CONV_LEFT), b, :], isem.at[sl])

        @pl.when(jnp.logical_and(jnp.logical_not(in_ctx), jnp.logical_not(last)))
        def _():
            w = (c - nc_ctx) * ncol + ncol
            for b in range(nb):
                fn(r_hbm.at[b, pl.ds(cblk, 1), w, :], rbuf.at[sl, pl.ds(jt + CONV_LEFT, 1), b, :], isem.at[sl])

    def loads(c, sl, fn):
        main_copies(r_hbm, rbuf, CONV_LEFT, c, sl, isem, fn)
        halo_copies(c, sl, fn)
        if combine:
            main_copies(q_hbm, qbuf, 0, c, sl, isem, fn)
            main_copies(ho_hbm, hbuf, 0, c, sl, isem, fn)

    def stores(c, sl, fn):
        main_copies(o_hbm, obuf, 0, c, sl, osem, lambda hbm, buf, sem: fn(buf, hbm, sem))

    start = lambda src, dst, sem: pltpu.make_async_copy(src, dst, sem).start()
    wait = lambda src, dst, sem: pltpu.make_async_copy(src, dst, sem).wait()

    c = chunk_of(s)
    first = jnp.logical_or(c == 0, c == nc_ctx)
    last = jnp.logical_or(c == nc_ctx - 1, c == nc - 1)

    @pl.when(s == 0)
    def _():
        h_ref[...] = jnp.zeros_like(h_ref)
        rbuf[...] = jnp.zeros_like(rbuf)
        loads(c, 0, start)

    @pl.when(s + 1 < nc)
    def _():
        loads(chunk_of(s + 1), 1 - slot, start)

    loads(c, slot, wait)

    @pl.when(s >= 2)
    def _():
        stores(chunk_of(s - 2), slot, wait)

    rb = rbuf[slot]
    prev = jnp.where(first, 0.0, rb[0:CONV_LEFT])
    nxt = jnp.where(last, 0.0, rb[jt + CONV_LEFT:jt + CONV_LEFT + 1])
    xe = jnp.concatenate([prev, rb[CONV_LEFT:jt + CONV_LEFT], nxt], axis=0)
    cw = cw_ref[...]
    xc = cb_ref[...].reshape(1, 1, ch)
    for k in range(cw.shape[0]):
        xc = xc + cw[k:k + 1, :].reshape(1, 1, ch) * xe[k:k + jt]
    xc2 = xc.reshape(jt * nb, ch)

    xb = xc2.astype(BF16)
    nblk, wb = wa_ref.shape[0], wa_ref.shape[1]
    za = jnp.concatenate([jnp.dot(xb[:, i * wb:(i + 1) * wb], wa_ref[i], preferred_element_type=F32)
                          for i in range(nblk)], axis=-1)
    zx = jnp.concatenate([jnp.dot(xb[:, i * wb:(i + 1) * wb], wx_ref[i], preferred_element_type=F32)
                          for i in range(nblk)], axis=-1)
    rg = jax.nn.sigmoid(za + ba_ref[...])
    ig = jax.nn.sigmoid(zx + bx_ref[...])
    log_a = cn_ref[...] * rg
    a = jnp.exp(log_a)
    bb = jnp.sqrt(1.0 - a * a) * (ig * xc2)
    a_s[...] = a.reshape(jt, nb, ch)
    b_s[...] = bb.reshape(jt, nb, ch)

    def step(t, h):
        tt = jt - 1 - t if reverse else t
        h = a_s[tt] * h + b_s[tt]
        if combine:
            obuf[slot, tt] = (h + hbuf[slot, tt]) * jax.nn.gelu(qbuf[slot, tt])
        else:
            obuf[slot, tt] = h
        return h

    h_ref[...] = lax.fori_loop(0, jt, step, h_ref[...])

    stores(c, slot, start)

    @pl.when(s == nc - 1)
    def _():
        if nc >= 2:
            stores(chunk_of(s - 1), 1 - slot, wait)
        stores(c, slot, wait)


def _lru_pass(reverse, r4, q4, h_other4, conv_w, conv_b, wa, ba, wx, bx, cneg, n_ctx, rows):
    nb, nblk, gw, ch = r4.shape
    jt = LRU_STEPS
    nc, nc_ctx = (nblk * gw) // jt, n_ctx // jt
    combine = h_other4 is not None
    assert gw % jt == 0 and n_ctx % gw == 0 and jt % rows == 0 and rows >= CONV_LEFT
    anyspec = pl.BlockSpec(memory_space=pl.ANY)
    full = lambda a: pl.BlockSpec(a.shape, lambda s: (0,) * a.ndim)
    consts = [conv_w, conv_b, wa, ba, wx, bx, cneg]
    big = [r4, q4, h_other4] if combine else [r4]
    buf = lambda n: pltpu.VMEM((2, n, nb, ch), F32)
    scratch = [buf(jt + CONV_LEFT + 1), buf(jt), pltpu.VMEM((nb, ch), F32),
               pltpu.VMEM((jt, nb, ch), F32), pltpu.VMEM((jt, nb, ch), F32),
               pltpu.SemaphoreType.DMA((2,)), pltpu.SemaphoreType.DMA((2,))]
    if combine:
        scratch += [buf(jt), buf(jt)]
    return pl.pallas_call(
        functools.partial(_lru_kernel, reverse, combine, nc_ctx, nc, rows),
        grid=(nc,), in_specs=[anyspec] * len(big) + [full(a) for a in consts], out_specs=anyspec,
        out_shape=jax.ShapeDtypeStruct(r4.shape, F32),
        scratch_shapes=scratch,
        compiler_params=_cparams("arbitrary"),
        name="lru_bwd" if reverse else "lru_fwd",
    )(*big, *consts)


def _block_diag_halves(w):
    nh, hd, _ = w.shape
    half = nh // 2
    eye = jnp.eye(half, dtype=w.dtype)
    wh = w.reshape(2, half, hd, hd)
    bd = jnp.einsum('bhij,hk->bhikj', wh, eye).reshape(2, half * hd, half * hd)
    return bd.astype(BF16)


def _post_kernel(n_groups, n_per, nsrc, off, cc, nsteps, *refs):
    srcs, refs = list(refs[:nsrc]), refs[nsrc:]
    (lr_hbm, yt_ref, ut_ref, gab_ref, gac_ref, shb_ref, shc_ref, scb_ref, scc_ref, d_ref, wg_ref, bg_ref,
     wo_ref, g2_ref, wr_ref, br_ref, x1_hbm, hm_hbm, meta_ref, xbuf, lbuf, x1buf, hmbuf, isem, osem) = refs
    st = _TileStreams([(srcs, xbuf), ([lr_hbm], lbuf)], [(x1_hbm, x1buf, off), (hm_hbm, hmbuf, off)],
                      isem, osem, cc, off, nsteps)
    st.begin()
    ct, slot = st.ct, st.slot
    x_in = xbuf[slot]
    nb, _, d = x_in.shape
    tm = nb * cc
    d_s5 = d_ref.shape[0]
    ys = yt_ref[...].reshape(d_s5, tm) + d_ref[...] * ut_ref[...].reshape(d_s5, tm)
    z = jax.nn.gelu(ys)
    s5t = z * jax.nn.sigmoid(jnp.dot(wg_ref[...], z.astype(BF16), preferred_element_type=F32) + bg_ref[...])
    mix = jnp.concatenate([s5t.T.astype(BF16), lbuf[slot].reshape(tm, -1).astype(BF16)], axis=-1)
    proj = jnp.dot(mix, wo_ref[...], preferred_element_type=F32).reshape(nb, cc, d)
    x1 = x_in + _mod_pick(ct, gab_ref, gac_ref) * proj
    x1buf[slot] = x1
    hm3 = _rms(x1, g2_ref[...]) * (1.0 + _mod_pick(ct, scb_ref, scc_ref)) + _mod_pick(ct, shb_ref, shc_ref)
    hmbuf[slot] = hm3
    hm = hm3.reshape(tm, d)

    hi = hm.astype(BF16)
    lo = (hm - hi.astype(F32)).astype(BF16)
    two = jnp.dot(jnp.concatenate([hi, lo], axis=-1), wr_ref[...], preferred_element_type=F32)
    logits = two[:, :LANES] + two[:, LANES:] + br_ref[...]

    lane = lax.broadcasted_iota(jnp.int32, (tm, LANES), 1).astype(F32)
    neg = jnp.float32(-jnp.inf)
    big = jnp.float32(LANES)
    lg = jnp.where(lane < n_groups, logits, neg)
    mg = jnp.max(lg, axis=-1, keepdims=True)
    g_p = 1.0 / jnp.sum(jnp.exp(lg - mg), axis=-1, keepdims=True)
    gidx = jnp.min(jnp.where(lg == mg, lane, big), axis=-1, keepdims=True)
    e0 = n_groups + n_per * gidx
    le = jnp.where(jnp.logical_and(lane >= e0, lane < e0 + n_per), logits, neg)
    m1 = jnp.max(le, axis=-1, keepdims=True)
    i1 = jnp.min(jnp.where(le == m1, lane, big), axis=-1, keepdims=True)
    le2 = jnp.where(lane == i1, neg, le)
    m2 = jnp.max(le2, axis=-1, keepdims=True)
    i2 = jnp.min(jnp.where(le2 == m2, lane, big), axis=-1, keepdims=True)
    r2 = jnp.exp(m2 - m1)
    w1 = g_p / (1.0 + r2)
    w2 = g_p * r2 / (1.0 + r2)
    j1 = i1 - e0
    j2 = i2 - e0
    jlo = jnp.minimum(j1, j2)
    jhi = jnp.maximum(j1, j2)
    wlo = jnp.where(j1 < j2, w1, w2)
    whi = jnp.where(j1 < j2, w2, w1)
    pair = jlo * (2 * n_per - 1 - jlo) * 0.5 + (jhi - jlo - 1.0)
    bucket = gidx * (n_per * (n_per - 1) // 2) + pair
    meta = jnp.where(lane == 0, bucket, jnp.where(lane == 1, wlo, jnp.where(lane == 2, whi, 0.0)))
    meta_ref[...] = meta.T[0:8, :]
    st.end()


def _post(lat_only, yt, ut, lr, xs, mods, layer, s5_d, w_glu, b_glu, w_out_bf, g2, wr2, br, n_groups, n_per, cc):
    q = S5_CHUNK
    nb, d = xs[0].shape[0], xs[0].shape[-1]
    n = sum(a.shape[1] for a in xs)
    nchunk = n // q
    ngrp, qh, _ = yt.shape
    nh = qh // q
    d_s5 = ngrp * nh
    d_lru = lr.shape[-1]
    off = 1 if lat_only else 0
    assert not (lat_only and len(xs) == 2)
    nct = nchunk // cc - off
    tm = nb * cc
    anyspec = pl.BlockSpec(memory_space=pl.ANY)
    tspec = pl.BlockSpec((ngrp, nh, tm), lambda ct, t: (0, t, ct + off))
    full = lambda a: pl.BlockSpec(a.shape, lambda ct, t: (0,) * a.ndim)
    consts = [s5_d.reshape(d_s5, 1), w_glu.T.astype(BF16), b_glu.reshape(d_s5, 1), w_out_bf, g2.reshape(1, d),
              wr2, br]
    buf = lambda w: pltpu.VMEM((2, nb, cc, w), F32)
    x1, hm, meta = pl.pallas_call(
        functools.partial(_post_kernel, n_groups, n_per, len(xs), off, cc, nct * q),
        grid=(nct, q),
        in_specs=([anyspec] * (len(xs) + 1) + [tspec, tspec]
                  + _mod_specs(layer, 2, d, nb) + _mod_specs(layer, 3, d, nb) + _mod_specs(layer, 4, d, nb)
                  + [full(a) for a in consts]),
        out_specs=[anyspec, anyspec, pl.BlockSpec((None, None, 8, tm), lambda ct, t: (t, ct, 0, 0))],
        out_shape=[jax.ShapeDtypeStruct((nb, nct * cc, q, d), F32), jax.ShapeDtypeStruct((nb, nct * cc, q, d), F32),
                   jax.ShapeDtypeStruct((q, nct, 8, tm), F32)],
        scratch_shapes=[buf(d), buf(d_lru), buf(d), buf(d),
                        pltpu.SemaphoreType.DMA((2,)), pltpu.SemaphoreType.DMA((2,))],
        compiler_params=_cparams("arbitrary", "arbitrary"),
        name="post_mixer",
    )(*[_view4(a) for a in xs], _view4(lr), yt, ut, mods, mods, mods, mods, mods, mods, *consts)
    n_out = nct * cc * q
    return x1.reshape(nb, n_out, d), hm.reshape(nb, n_out, d), meta


def _moe_kernel(n_rows, nt, tb_ref, src0_ref, srcn_ref, dst_ref, hm_hbm, wt_ref, w1a, w3a, w2a, w1b, w3b, w2b,
                y_hbm, xbuf, ybuf, gsem, ssem):
    i = pl.program_id(0)
    slot = lax.rem(i, 2)
    tmr = xbuf.shape[1]

    def valid(j):
        return tb_ref[2, jnp.clip(j, 0, nt - 1)] > 0

    def start_gathers(ids_ref, s):
        for r in range(tmr):
            pltpu.make_async_copy(hm_hbm.at[pl.ds(ids_ref[0, r], 1), :], xbuf.at[s, pl.ds(r, 1), :],
                                  gsem.at[s]).start(priority=r % 2)

    def start_scatters(s):
        for r in range(tmr):
            pltpu.make_async_copy(ybuf.at[s, pl.ds(r, 1), :], y_hbm.at[pl.ds(dst_ref[0, r], 1), :],
                                  ssem.at[s]).start(priority=r % 2)

    def wait_gathers(s):
        pltpu.make_async_copy(hm_hbm.at[pl.ds(0, tmr), :], xbuf.at[s], gsem.at[s]).wait()

    def wait_scatters(s):
        pltpu.make_async_copy(ybuf.at[s], y_hbm.at[pl.ds(0, tmr), :], ssem.at[s]).wait()

    @pl.when(i == 0)
    def _():
        xbuf[...] = jnp.zeros_like(xbuf)
        ybuf[...] = jnp.zeros_like(ybuf)
        for s in range((y_hbm.shape[0] - n_rows) // tmr):
            cp = pltpu.make_async_copy(ybuf.at[0], y_hbm.at[pl.ds(n_rows + s * tmr, tmr), :], ssem.at[0])
            cp.start()
            cp.wait()
        start_gathers(src0_ref, 0)

    def step(s):
        @pl.when(jnp.logical_and(i + 1 < nt, valid(i + 1)))
        def _():
            start_gathers(srcn_ref, 1 - s)

        @pl.when(jnp.logical_and(i >= 2, valid(i - 2)))
        def _():
            wait_scatters(s)

        @pl.when(valid(i))
        def _():
            wait_gathers(s)
            xb = xbuf[s].astype(BF16)
            wt = wt_ref[...]

            def expert(w1, w3, w2, gate):
                h1 = jnp.dot(xb, w1[...], preferred_element_type=F32)
                h3 = jnp.dot(xb, w3[...], preferred_element_type=F32)
                hid = (h1 * jax.nn.sigmoid(h1)) * h3
                return gate * jnp.dot(hid.astype(BF16), w2[...], preferred_element_type=F32)

            half = LANES // 2
            ybuf[s] = expert(w1a, w3a, w2a, wt[:, 0:1]) + expert(w1b, w3b, w2b, wt[:, half:half + 1])
            start_scatters(s)

        @pl.when(i == nt - 1)
        def _():
            @pl.when(jnp.logical_and(nt >= 2, valid(i - 1)))
            def _():
                wait_scatters(1 - s)

            @pl.when(valid(i))
            def _():
                wait_scatters(s)

    for s in range(2):
        pl.when(slot == s)(functools.partial(step, s))


def _moe(hm2, bucket, wlo, whi, tok_rows, w1_bf, w3_bf, w2_bf, expert_base, n_groups, n_per, tmr, spare_rows):
    n_rows, d = hm2.shape
    assert spare_rows >= 2 * tmr and spare_rows % tmr == 0
    t = bucket.shape[0]
    npairs = n_per * (n_per - 1) // 2
    nbuck = n_groups * npairs
    ntiles = t // tmr + nbuck

    _, s_tok, s_lo, s_hi = lax.sort((bucket, tok_rows, wlo, whi), num_keys=1, is_stable=False)
    counts = jnp.sum((bucket[None, :] == jnp.arange(nbuck, dtype=jnp.int32)[:, None]).astype(jnp.int32), axis=1)
    padded = ((counts + tmr - 1) // tmr) * tmr
    pend = jnp.cumsum(padded)
    cend = jnp.cumsum(counts)
    shift = (pend - padded) - (cend - counts)
    tile_start = jnp.arange(ntiles, dtype=jnp.int32) * tmr
    valid = (tile_start < pend[-1]).astype(jnp.int32)
    tb = jnp.sum((tile_start[:, None] >= pend[None, :]).astype(jnp.int32), axis=1)
    tb = jnp.minimum(tb, jnp.sum((pend < pend[-1]).astype(jnp.int32)))
    tb = jnp.minimum(tb, nbuck - 1)
    pos = (tile_start[:, None] + jnp.arange(tmr, dtype=jnp.int32)[None, :]) - shift[tb][:, None]
    real = jnp.logical_and(pos < cend[tb][:, None], valid[:, None] > 0)
    pos = jnp.clip(pos, 0, t - 1)
    tok = s_tok[pos]
    src = jnp.where(real, tok, 0)
    spare = n_rows + (jnp.arange(ntiles, dtype=jnp.int32) % 2)[:, None] * tmr + jnp.arange(tmr, dtype=jnp.int32)
    dst = jnp.where(real, tok, spare)
    g_lo = jnp.where(real, s_lo[pos], 0.0).reshape(-1, 1)
    g_hi = jnp.where(real, s_hi[pos], 0.0).reshape(-1, 1)
    half = LANES // 2
    wts = jnp.concatenate([jnp.broadcast_to(g_lo, (ntiles * tmr, half)),
                           jnp.broadcast_to(g_hi, (ntiles * tmr, half))], axis=1)
    grp, pr = tb // npairs, tb % npairs
    pairs = [(i, j) for i in range(n_per) for j in range(i + 1, n_per)]
    plo = jnp.array([p[0] for p in pairs], jnp.int32)[pr]
    phi = jnp.array([p[1] for p in pairs], jnp.int32)[pr]
    first = expert_base + grp * n_per
    tinfo = jnp.stack([first + plo, first + phi, valid])

    wspec = lambda a, row: pl.BlockSpec((None,) + a.shape[1:], lambda i, tb_: (tb_[row, i], 0, 0))
    ids = lambda f: pl.BlockSpec((None, 1, tmr), lambda i, tb_: (f(i), 0, 0), memory_space=pltpu.SMEM)
    grid_spec = pltpu.PrefetchScalarGridSpec(
        num_scalar_prefetch=1,
        grid=(ntiles,),
        in_specs=[ids(lambda i: 0), ids(lambda i: jnp.minimum(i + 1, ntiles - 1)), ids(lambda i: i),
                  pl.BlockSpec(memory_space=pl.ANY),
                  pl.BlockSpec((tmr, LANES), lambda i, tb_: (i, 0)),
                  wspec(w1_bf, 0), wspec(w3_bf, 0), wspec(w2_bf, 0),
                  wspec(w1_bf, 1), wspec(w3_bf, 1), wspec(w2_bf, 1)],
        out_specs=pl.BlockSpec(memory_space=pl.ANY),
        scratch_shapes=[pltpu.VMEM((2, tmr, d), F32), pltpu.VMEM((2, tmr, d), F32),
                        pltpu.SemaphoreType.DMA((2,)), pltpu.SemaphoreType.DMA((2,))],
    )
    src3 = src.reshape(ntiles, 1, tmr)
    return pl.pallas_call(
        functools.partial(_moe_kernel, n_rows, ntiles),
        grid_spec=grid_spec,
        out_shape=jax.ShapeDtypeStruct((n_rows + spare_rows, d), F32),
        compiler_params=_cparams("arbitrary"),
        name="moe_pairs",
    )(tinfo, src3, src3, dst.reshape(ntiles, 1, tmr), hm2, wts, w1_bf, w3_bf, w2_bf, w1_bf, w3_bf, w2_bf)


def _final_kernel(x_ref, y_ref, ga_ref, g_ref, o_ref):
    o_ref[...] = _rms(x_ref[...] + ga_ref[...] * y_ref[...], g_ref[...])


def _final(x1, y, mods, layer, final_g, tm):
    nb, seq, d = x1.shape
    tok = pl.BlockSpec((None, tm, d), lambda b, i: (b, i, 0))
    return pl.pallas_call(
        _final_kernel,
        grid=(nb, seq // tm),
        in_specs=[tok, pl.BlockSpec((tm, d), lambda b, i: (b * (seq // tm) + i, 0)),
                  _mod_spec_lat(layer, 5, d), pl.BlockSpec((1, d), lambda b, i: (0, 0))],
        out_specs=tok,
        out_shape=jax.ShapeDtypeStruct((nb, seq, d), F32),
        compiler_params=_cparams("parallel", "parallel"),
        name="final_norm",
    )(x1, y, mods, final_g.reshape(1, d))


def kernel(x, c, ctx, c_ctx, w_mod, b_mod, norm1_g, norm2_g, w_in, w_out, s5_a_re, s5_a_im, s5_log_dt, s5_b_re, s5_b_im, s5_c_re, s5_c_im, s5_d, s5_w_glu, s5_b_glu, lru_conv_w, lru_conv_b, lru_w_a, lru_b_a, lru_w_x, lru_b_x, lru_lam, moe_w_group, moe_b_group, moe_w_router, moe_b_router, moe_w1, moe_w3, moe_w2, final_g):
    nb, seq, d = x.shape
    n_ctx = ctx.shape[1]
    depth = w_mod.shape[0]
    n = n_ctx + seq
    d_s5 = s5_d.shape[-1]
    d_lru = lru_conv_b.shape[-1]
    ngrp, nh = s5_b_re.shape[2], s5_b_re.shape[4]
    n_groups, n_per = moe_w_router.shape[1], moe_w_router.shape[3]
    rows = seq // GRID_W
    tm = n_ctx
    q = S5_CHUNK
    nc_ctx, nc_lat = n_ctx // q, seq // q
    moe_rows = 256 if nb * seq >= 16384 else 32
    assert seq % GRID_W == 0 and seq % tm == 0 and n_ctx % GRID_W == 0 and seq % LRU_STEPS == 0
    assert d_s5 == ngrp * nh and q * nh == 2 * LANES and n_groups + n_groups * n_per <= LANES

    pad = (-(nb + 1)) % 8
    c_rows = jnp.concatenate([c, c_ctx[None, :], jnp.zeros((pad, d), F32)], axis=0)
    mods = _modulation(c_rows, w_mod, b_mod).reshape(depth, nb + 1 + pad, 1, 6 * d)

    s5w = jax.vmap(_s5_weights)(s5_a_re, s5_a_im, s5_log_dt, s5_b_re, s5_b_im, s5_c_re, s5_c_im)
    bdiag = jax.vmap(jax.vmap(_block_diag_halves))
    wa_bd, wx_bd = bdiag(lru_w_a), bdiag(lru_w_x)
    cneg_all = (-LRU_C * jax.nn.softplus(-lru_lam.astype(F32))).reshape(depth, 2, 1, d_lru)
    wr = jnp.concatenate([moe_w_group, moe_w_router.transpose(0, 2, 1, 3).reshape(depth, d, n_groups * n_per)], -1)
    wr = jnp.pad(wr.astype(F32), ((0, 0), (0, 0), (0, LANES - wr.shape[-1])))
    wr_hi = wr.astype(BF16)
    wr_lo = (wr - wr_hi.astype(F32)).astype(BF16)
    wr2_all = jnp.concatenate([jnp.concatenate([wr_hi, wr_lo], -1),
                               jnp.concatenate([wr_hi, jnp.zeros_like(wr_lo)], -1)], axis=1)
    br_all = jnp.concatenate([moe_b_group, moe_b_router.reshape(depth, -1)], axis=-1)
    br_all = jnp.pad(br_all.astype(F32), ((0, 0), (0, LANES - br_all.shape[-1]))).reshape(depth, 1, LANES)
    w_out_bf = w_out.astype(BF16)
    n_exp = n_groups * n_per
    moe_bf = [w.astype(BF16).reshape((depth * n_exp,) + w.shape[2:]) for w in (moe_w1, moe_w3, moe_w2)]

    cc = nc_ctx
    xs = [ctx, x]
    y = None
    for l in range(depth):
        need_ctx = l < depth - 1
        xs, ut, r, qg = _inproj(xs, y, mods, l, norm1_g[l], w_in[l], ngrp, d_s5, d_lru, cc)

        yt = _s5_scan(ut, *s5w, l, nb, cc, nc_ctx, nc_lat)

        view = lambda a: a.reshape(nb, n // GRID_W, GRID_W, d_lru)
        cw, cb = lru_conv_w[l].astype(F32), lru_conv_b[l].reshape(1, d_lru).astype(F32)
        gates = [(wa_bd[l, dd], lru_b_a[l, dd].reshape(1, d_lru),
                  wx_bd[l, dd], lru_b_x[l, dd].reshape(1, d_lru), cneg_all[l, dd]) for dd in range(2)]
        h_b = _lru_pass(True, view(r), None, None, cw, cb, *gates[1], n_ctx, rows)
        lr = _lru_pass(False, view(r), view(qg), h_b, cw, cb, *gates[0], n_ctx, rows).reshape(nb, n, d_lru)

        x1, hm, meta = _post(not need_ctx, yt, ut, lr, xs, mods, l, s5_d[l], s5_w_glu[l], s5_b_glu[l],
                             w_out_bf[l], norm2_g[l], wr2_all[l], br_all[l], n_groups, n_per, cc)

        n_out = x1.shape[1]
        nct = n_out // (cc * q)
        meta_t = meta.reshape(q, nct, 8, nb, cc).transpose(2, 3, 1, 4, 0).reshape(8, nb * n_out)
        bucket = meta_t[0].astype(jnp.int32)
        y = _moe(hm.reshape(nb * n_out, d), bucket, meta_t[1], meta_t[2], jnp.arange(nb * n_out, dtype=jnp.int32),
                 *moe_bf, l * n_exp, n_groups, n_per, moe_rows, n_out)
        xs = [x1]
        if need_ctx:
            y = y.reshape(nb + 1, n_out, d)
    return _final(xs[0], y, mods, depth - 1, final_g, 4 * tm if seq % (4 * tm) == 0 else tm)
```

```python
import functools

import jax
import jax.numpy as jnp
from jax import lax
from jax.experimental import pallas as pl
from jax.experimental.pallas import tpu as pltpu

F32 = jnp.float32
BF16 = jnp.bfloat16
HIGHEST = lax.Precision.HIGHEST

EPS = 1e-6
GRID_W = 64
LRU_C = 8.0
CONV_LEFT = 2
S5_CHUNK = 16
S5_ROW_PAD = 4
LRU_STEPS = 32
LANES = 128
VMEM_LIMIT = 56 * 1024 * 1024


def _cparams(*sem):
    return pltpu.CompilerParams(dimension_semantics=sem, vmem_limit_bytes=VMEM_LIMIT)


def _rms(x, g):
    return x * lax.rsqrt(jnp.mean(x * x, axis=-1, keepdims=True) + EPS) * g


def _mod_kernel(c_ref, w_ref, b_ref, o_ref):
    c = c_ref[...]
    s = c * jax.nn.sigmoid(c)
    o_ref[...] = jnp.dot(s, w_ref[...], preferred_element_type=F32, precision=HIGHEST) + b_ref[...]


def _modulation(c_rows, w_mod, b_mod):
    depth, d, n6 = w_mod.shape
    rows = c_rows.shape[0]
    tn = n6 // 4
    return pl.pallas_call(
        _mod_kernel,
        grid=(depth, n6 // tn),
        in_specs=[pl.BlockSpec((rows, d), lambda l, j: (0, 0)),
                  pl.BlockSpec((None, d, tn), lambda l, j: (l, 0, j)),
                  pl.BlockSpec((None, 1, tn), lambda l, j: (l, 0, j))],
        out_specs=pl.BlockSpec((None, rows, tn), lambda l, j: (l, 0, j)),
        out_shape=jax.ShapeDtypeStruct((depth, rows, n6), F32),
        compiler_params=_cparams("parallel", "parallel"),
        name="modulation",
    )(c_rows, w_mod, b_mod.reshape(depth, 1, n6))


def _mod_specs(layer, part, d, nb):
    return [pl.BlockSpec((None, nb, 1, d), lambda ct, t: (layer, 0, 0, part)),
            pl.BlockSpec((None, None, 1, d), lambda ct, t: (layer, nb, 0, part))]


def _mod_pick(ct, batch_ref, ctx_ref):
    return jnp.where(ct == 0, ctx_ref[...][None], batch_ref[...])


def _mod_spec_lat(layer, part, d):
    return pl.BlockSpec((None, None, 1, d), lambda b, i: (layer, b, 0, part))


def _view4(a):
    nb, n, w = a.shape
    return a.reshape(nb, n // S5_CHUNK, S5_CHUNK, w)


def _tile_slice(a4, nb, ct, t, cc, off=0):
    return a4.at[pl.ds(0, nb), pl.ds((ct - off) * cc, cc), t, :]


class _TileStreams:
    def __init__(self, ins, outs, isem, osem, cc, in_off, nsteps):
        self.ins, self.outs, self.isem, self.osem = ins, outs, isem, osem
        self.cc, self.in_off, self.nsteps = cc, in_off, nsteps
        self.nb = ins[0][1].shape[1]
        self.ct = pl.program_id(0) + in_off
        self.t = pl.program_id(1)
        self.q = pl.num_programs(1)
        self.s = pl.program_id(0) * self.q + self.t
        self.slot = lax.rem(self.s, 2)

    def _start_inputs(self, ct, t, sl):
        for srcs, buf in self.ins:
            if len(srcs) == 2:
                @pl.when(ct == 0)
                def _():
                    pltpu.make_async_copy(_tile_slice(srcs[0], self.nb, 0, t, self.cc), buf.at[sl], self.isem.at[sl]).start()

                @pl.when(ct > 0)
                def _():
                    pltpu.make_async_copy(_tile_slice(srcs[1], self.nb, ct, t, self.cc, 1), buf.at[sl],
                                          self.isem.at[sl]).start()
            else:
                pltpu.make_async_copy(_tile_slice(srcs[0], self.nb, ct, t, self.cc), buf.at[sl], self.isem.at[sl]).start()

    def _wait_outputs(self, sl):
        for arr, buf, _ in self.outs:
            pltpu.make_async_copy(buf.at[sl], _tile_slice(arr, self.nb, 0, 0, self.cc), self.osem.at[sl]).wait()

    def begin(self):
        ct, t, s, slot = self.ct, self.t, self.s, self.slot

        @pl.when(s == 0)
        def _():
            self._start_inputs(ct, t, 0)

        @pl.when(s + 1 < self.nsteps)
        def _():
            wrap = t == self.q - 1
            self._start_inputs(jnp.where(wrap, ct + 1, ct), jnp.where(wrap, 0, t + 1), 1 - slot)

        for srcs, buf in self.ins:
            pltpu.make_async_copy(_tile_slice(srcs[0], self.nb, 0, 0, self.cc), buf.at[slot], self.isem.at[slot]).wait()

        @pl.when(s >= 2)
        def _():
            self._wait_outputs(slot)

    def end(self):
        for arr, buf, off in self.outs:
            pltpu.make_async_copy(buf.at[self.slot], _tile_slice(arr, self.nb, self.ct, self.t, self.cc, off),
                                  self.osem.at[self.slot]).start()

        @pl.when(self.s == self.nsteps - 1)
        def _():
            if self.nsteps >= 2:
                self._wait_outputs(1 - self.slot)
            self._wait_outputs(self.slot)


def _inproj_kernel(has_y, nsrc, ngrp, cc, nsteps, *refs):
    srcs, refs = list(refs[:nsrc]), refs[nsrc:]
    if has_y:
        y_hbm, gab_ref, gac_ref = refs[:3]
        refs = refs[3:]
    shb_ref, shc_ref, scb_ref, scc_ref, g_ref, wu_ref, wrq_ref = refs[:7]
    refs = refs[7:]
    if has_y:
        xo_hbm, ut_ref, r_hbm, q_hbm, xbuf, rbuf, qbuf, isem, osem, ybuf, xobuf = refs
        ins = [(srcs, xbuf), ([y_hbm], ybuf)]
        outs = [(xo_hbm, xobuf, 0), (r_hbm, rbuf, 0), (q_hbm, qbuf, 0)]
    else:
        ut_ref, r_hbm, q_hbm, xbuf, rbuf, qbuf, isem, osem = refs
        ins = [(srcs, xbuf)]
        outs = [(r_hbm, rbuf, 0), (q_hbm, qbuf, 0)]
    st = _TileStreams(ins, outs, isem, osem, cc, 0, nsteps)
    st.begin()
    ct, slot = st.ct, st.slot
    x = xbuf[slot]
    if has_y:
        x = x + _mod_pick(ct, gab_ref, gac_ref) * ybuf[slot]
        xobuf[slot] = x
    nb, _, d = x.shape
    d_lru = rbuf.shape[-1]
    h = _rms(x, g_ref[...]) * (1.0 + _mod_pick(ct, scb_ref, scc_ref)) + _mod_pick(ct, shb_ref, shc_ref)
    hb = h.reshape(nb * cc, d).astype(BF16)
    ut = lax.dot_general(wu_ref[...], hb, (((1,), (1,)), ((), ())), preferred_element_type=F32)
    ut_ref[...] = ut.reshape(ngrp, ut.shape[0] // ngrp, nb * cc)
    p = jnp.dot(hb, wrq_ref[...], preferred_element_type=F32)
    rbuf[slot] = p[:, :d_lru].reshape(nb, cc, d_lru)
    qbuf[slot] = p[:, d_lru:].reshape(nb, cc, d_lru)
    st.end()


def _inproj(xs, y, mods, layer, g1, w_in, ngrp, d_s5, d_lru, cc):
    q = S5_CHUNK
    nb, d = xs[0].shape[0], xs[0].shape[-1]
    n = sum(a.shape[1] for a in xs)
    nchunk = n // q
    has_y = y is not None
    nh = d_s5 // ngrp
    nsteps = (nchunk // cc) * q
    anyspec = pl.BlockSpec(memory_space=pl.ANY)
    in_specs = [anyspec] * len(xs)
    args = [_view4(a) for a in xs]
    if has_y:
        in_specs += [anyspec] + _mod_specs(layer - 1, 5, d, nb)
        args += [_view4(y), mods, mods]
    wu = w_in[:, :d_s5].T.astype(BF16)
    wrq = w_in[:, d_s5:].astype(BF16)
    full = lambda a: pl.BlockSpec(a.shape, lambda ct, t: (0,) * a.ndim)
    in_specs += _mod_specs(layer, 0, d, nb) + _mod_specs(layer, 1, d, nb) + [
        pl.BlockSpec((1, d), lambda ct, t: (0, 0)), full(wu), full(wrq)]
    args += [mods, mods, mods, mods, g1.reshape(1, d), wu, wrq]
    out_specs = [pl.BlockSpec((ngrp, nh, nb * cc), lambda ct, t: (0, t, ct)), anyspec, anyspec]
    out_shape = [jax.ShapeDtypeStruct((ngrp, q * nh, nb * nchunk), F32),
                 jax.ShapeDtypeStruct((nb, nchunk, q, d_lru), F32),
                 jax.ShapeDtypeStruct((nb, nchunk, q, d_lru), F32)]
    buf = lambda w: pltpu.VMEM((2, nb, cc, w), F32)
    scratch = [buf(d), buf(d_lru), buf(d_lru), pltpu.SemaphoreType.DMA((2,)), pltpu.SemaphoreType.DMA((2,))]
    if has_y:
        out_specs = [anyspec] + out_specs
        out_shape = [jax.ShapeDtypeStruct((nb, nchunk, q, d), F32)] + out_shape
        scratch += [buf(d), buf(d)]
    res = pl.pallas_call(
        functools.partial(_inproj_kernel, has_y, len(xs), ngrp, cc, nsteps),
        grid=(nchunk // cc, q), in_specs=in_specs, out_specs=out_specs, out_shape=out_shape,
        scratch_shapes=scratch,
        compiler_params=_cparams("arbitrary", "arbitrary"),
        name="inproj",
    )(*args)
    if has_y:
        xs = [res[0].reshape(nb, n, d)]
        res = res[1:]
    return [xs, res[0], res[1].reshape(nb, n, d_lru), res[2].reshape(nb, n, d_lru)]


def _s5_weights(a_re, a_im, log_dt, b_re, b_im, c_re, c_im):
    q = S5_CHUNK
    lam = lax.complex(a_re.astype(F32), a_im.astype(F32))
    dt = jnp.exp(log_dt.astype(F32))[..., None]
    a_bar = jnp.exp(lam * dt)
    bmat = lax.complex(b_re.astype(F32), b_im.astype(F32))
    b_bar = ((a_bar - 1) / lam)[..., None] * bmat
    cmat = lax.complex(c_re.astype(F32), c_im.astype(F32))
    k = jnp.arange(q + 1, dtype=F32)
    apow = jnp.exp((lam * dt)[..., None] * k)
    ngrp, nst = a_re.shape[1], a_re.shape[2]
    nh = b_re.shape[-1]

    kern = jnp.einsum('dgop,dgpk,dgpi->dgkoi', cmat, apow[..., :q], b_bar, precision=HIGHEST).real
    sig = jnp.arange(q)[:, None]
    tau = jnp.arange(q)[None, :]
    lag_f = jnp.clip(tau - sig, 0, q - 1)
    lag_b = jnp.clip(sig - tau, 0, q - 1)
    tf = jnp.where((sig <= tau)[None, :, :, None, None], kern[0][:, lag_f], 0.0)
    tb = jnp.where((sig >= tau)[None, :, :, None, None], kern[1][:, lag_b], 0.0)
    toep = (tf + tb).transpose(0, 1, 4, 2, 3).reshape(ngrp, q * nh, q * nh)

    win_f = jnp.einsum('gps,gpi->gsip', apow[0][..., :q][..., ::-1], b_bar[0])
    win_b = jnp.einsum('gps,gpi->gsip', apow[1][..., :q], b_bar[1])
    win = jnp.concatenate([win_f.real, win_b.real, win_f.imag, win_b.imag], axis=-1)
    win = win.reshape(ngrp, q * nh, 4 * nst)

    cf = jnp.einsum('gop,gpt->gpto', cmat[0], apow[0][..., 1:])
    cb = jnp.einsum('gop,gpt->gpto', cmat[1], apow[1][..., 1:][..., ::-1])
    z = jnp.zeros_like(cf.real)
    wof = jnp.concatenate([cf.real, z, -cf.imag, z], axis=1).reshape(ngrp, 4 * nst, q * nh)
    wob = jnp.concatenate([z, cb.real, z, -cb.imag], axis=1).reshape(ngrp, 4 * nst, q * nh)

    aq = apow[..., q]
    a16 = jnp.stack([jnp.concatenate([aq[0].real, aq[1].real], -1),
                     jnp.concatenate([aq[0].imag, aq[1].imag], -1)], axis=1)
    w1 = jnp.concatenate([toep, win], axis=-1)
    return w1.astype(BF16), wof.astype(BF16), wob.astype(BF16), a16.astype(F32)


def _s5_kernel(nb, cc, nc_ctx, nc_lat, u_ref, w1_ref, wof_ref, wob_ref, a_ref, y_ref,
               yrow_ref, sre_ref, sim_ref, fre_ref, fim_ref, bre_ref, bim_ref):
    qh, m = u_ref.shape
    ns2 = a_ref.shape[-1]
    mt = nb * cc
    pitch = cc + S5_ROW_PAD
    w1 = w1_ref[...]

    for i in range(m // mt):
        r0, p0 = i * mt, i * nb * pitch
        u = u_ref[:, r0:r0 + mt].T.astype(BF16)
        res = jnp.dot(u, w1, preferred_element_type=F32)
        yrow_ref[r0:r0 + mt, :] = res[:, :qh]
        for b in range(nb):
            sre_ref[p0 + b * pitch:p0 + b * pitch + cc, :] = res[b * cc:(b + 1) * cc, qh:qh + ns2]
            sim_ref[p0 + b * pitch:p0 + b * pitch + cc, :] = res[b * cc:(b + 1) * cc, qh + ns2:]

    are = jnp.broadcast_to(a_ref[0:1, :], (nb, ns2))
    aim = jnp.broadcast_to(a_ref[1:2, :], (nb, ns2))
    fwd_lane = lax.broadcasted_iota(jnp.int32, (nb, ns2), 1) < ns2 // 2

    def rows(c):
        tile = lax.div(c, cc)
        return pl.ds(tile * (nb * pitch) + (c - tile * cc), nb, stride=pitch)

    def scan(base, n, carry):
        def step(k, hc):
            hre, him = hc
            rf = rows(base + k)
            rb = rows(base + n - 1 - k)
            fre_ref[rf, :] = hre
            fim_ref[rf, :] = him
            bre_ref[rb, :] = hre
            bim_ref[rb, :] = him
            sre = jnp.where(fwd_lane, sre_ref[rf, :], sre_ref[rb, :])
            sim = jnp.where(fwd_lane, sim_ref[rf, :], sim_ref[rb, :])
            return (are * hre - aim * him + sre, are * him + aim * hre + sim)
        return lax.fori_loop(0, n, step, carry)

    zero = jnp.zeros((nb, ns2), F32)
    carry = scan(0, nc_ctx, (zero, zero))
    scan(nc_ctx, nc_lat, carry)

    wof = wof_ref[...]
    wob = wob_ref[...]

    for i in range(m // mt):
        r0, p0 = i * mt, i * nb * pitch
        tile_rows = lambda ref: jnp.concatenate(
            [ref[p0 + b * pitch:p0 + b * pitch + cc, :] for b in range(nb)], axis=0)
        hf = jnp.concatenate([tile_rows(fre_ref), tile_rows(fim_ref)], axis=1).astype(BF16)
        hb = jnp.concatenate([tile_rows(bre_ref), tile_rows(bim_ref)], axis=1).astype(BF16)
        y = (yrow_ref[r0:r0 + mt, :] + jnp.dot(hf, wof, preferred_element_type=F32)
             + jnp.dot(hb, wob, preferred_element_type=F32))
        y_ref[:, r0:r0 + mt] = y.T


def _s5_scan(u_t, w1, wof, wob, a16, layer, nb, cc, nc_ctx, nc_lat):
    ngrp, qh, m = u_t.shape
    ns2 = a16.shape[-1]
    grp = lambda a: pl.BlockSpec((None,) + a.shape[1:], lambda g: (g, 0, 0))
    lgrp = lambda a: pl.BlockSpec((None, None) + a.shape[2:], lambda g: (layer, g, 0, 0))
    return pl.pallas_call(
        functools.partial(_s5_kernel, nb, cc, nc_ctx, nc_lat),
        grid=(ngrp,),
        in_specs=[grp(u_t), lgrp(w1), lgrp(wof), lgrp(wob), lgrp(a16)],
        out_specs=grp(u_t),
        out_shape=jax.ShapeDtypeStruct(u_t.shape, F32),
        scratch_shapes=[pltpu.VMEM((m, qh), F32)] + [pltpu.VMEM((m // cc * (cc + S5_ROW_PAD), ns2), F32)] * 6,
        compiler_params=_cparams("parallel"),
        name="s5_chunked",
    )(u_t, w1, wof, wob, a16)


def _lru_chunk(s, reverse, nc_ctx, nc):
    if not reverse:
        return s
    return jnp.where(s < nc_ctx, nc_ctx - 1 - s, nc - 1 - (s - nc_ctx))


def _lru_kernel(reverse, combine, nc_ctx, nc, rows, *refs):
    if combine:
        (r_hbm, q_hbm, ho_hbm, cw_ref, cb_ref, wa_ref, ba_ref, wx_ref, bx_ref, cn_ref, o_hbm,
         rbuf, obuf, h_ref, a_s, b_s, isem, osem, qbuf, hbuf) = refs
    else:
        (r_hbm, cw_ref, cb_ref, wa_ref, ba_ref, wx_ref, bx_ref, cn_ref, o_hbm,
         rbuf, obuf, h_ref, a_s, b_s, isem, osem) = refs
    _, jt, nb, ch = obuf.shape
    gw = r_hbm.shape[2]
    cblk = (nc_ctx * jt) // gw
    ncol = jt // rows
    s = pl.program_id(0)
    slot = lax.rem(s, 2)

    def chunk_of(step):
        return _lru_chunk(jnp.clip(step, 0, nc - 1), reverse, nc_ctx, nc)

    def main_copies(hbm, buf, row0, c, sl, sem, fn):
        @pl.when(c < nc_ctx)
        def _():
            j0 = c * jt
            for b in range(nb):
                fn(hbm.at[b, j0 // gw, pl.ds(j0 % gw, jt), :], buf.at[sl, pl.ds(row0, jt), b, :], sem.at[sl])

        @pl.when(c >= nc_ctx)
        def _():
            w0 = (c - nc_ctx) * ncol
            for b in range(nb):
                for k in range(ncol):
                    fn(hbm.at[b, pl.ds(cblk, rows), w0 + k, :],
                       buf.at[sl, pl.ds(row0 + k * rows, rows), b, :], sem.at[sl])

    def halo_copies(c, sl, fn):
        first = jnp.logical_or(c == 0, c == nc_ctx)
        last = jnp.logical_or(c == nc_ctx - 1, c == nc - 1)
        in_ctx = c < nc_ctx

        @pl.when(jnp.logical_and(in_ctx, jnp.logical_not(first)))
        def _():
            j = c * jt - CONV_LEFT
            for b in range(nb):
                fn(r_hbm.at[b, j // gw, pl.ds(j % gw, CONV_LEFT), :], rbuf.at[sl, pl.ds(0, CONV_LEFT), b, :],
                   isem.at[sl])

        @pl.when(jnp.logical_and(in_ctx, jnp.logical_not(last)))
        def _():
            j = c * jt + jt
            for b in range(nb):
                fn(r_hbm.at[b, j // gw, pl.ds(j % gw, 1), :], rbuf.at[sl, pl.ds(jt + CONV_LEFT, 1), b, :],
                   isem.at[sl])

        @pl.when(jnp.logical_and(jnp.logical_not(in_ctx), jnp.logical_not(first)))
        def _():
            w = (c - nc_ctx) * ncol - 1
            for b in range(nb):
                fn(r_hbm.at[b, pl.ds(cblk + rows - CONV_LEFT, CONV_LEFT), w, :],
                   rbuf.at[sl, pl.ds(0, CONV_LEFT), b, :], isem.at[sl])

        @pl.when(jnp.logical_and(jnp.logical_not(in_ctx), jnp.logical_not(last)))
        def _():
            w = (c - nc_ctx) * ncol + ncol
            for b in range(nb):
                fn(r_hbm.at[b, pl.ds(cblk, 1), w, :], rbuf.at[sl, pl.ds(jt + CONV_LEFT, 1), b, :], isem.at[sl])

    def loads(c, sl, fn):
        main_copies(r_hbm, rbuf, CONV_LEFT, c, sl, isem, fn)
        halo_copies(c, sl, fn)
        if combine:
            main_copies(q_hbm, qbuf, 0, c, sl, isem, fn)
            main_copies(ho_hbm, hbuf, 0, c, sl, isem, fn)

    def stores(c, sl, fn):
        main_copies(o_hbm, obuf, 0, c, sl, osem, lambda hbm, buf, sem: fn(buf, hbm, sem))

    start = lambda src, dst, sem: pltpu.make_async_copy(src, dst, sem).start()
    wait = lambda src, dst, sem: pltpu.make_async_copy(src, dst, sem).wait()

    c = chunk_of(s)
    first = jnp.logical_or(c == 0, c == nc_ctx)
    last = jnp.logical_or(c == nc_ctx - 1, c == nc - 1)

    @pl.when(s == 0)
    def _():
        h_ref[...] = jnp.zeros_like(h_ref)
        rbuf[...] = jnp.zeros_like(rbuf)
        loads(c, 0, start)

    @pl.when(s + 1 < nc)
    def _():
        loads(chunk_of(s + 1), 1 - slot, start)

    loads(c, slot, wait)

    @pl.when(s >= 2)
    def _():
        stores(chunk_of(s - 2), slot, wait)

    rb = rbuf[slot]
    prev = jnp.where(first, 0.0, rb[0:CONV_LEFT])
    nxt = jnp.where(last, 0.0, rb[jt + CONV_LEFT:jt + CONV_LEFT + 1])
    xe = jnp.concatenate([prev, rb[CONV_LEFT:jt + CONV_LEFT], nxt], axis=0)
    cw = cw_ref[...]
    xc = cb_ref[...].reshape(1, 1, ch)
    for k in range(cw.shape[0]):
        xc = xc + cw[k:k + 1, :].reshape(1, 1, ch) * xe[k:k + jt]
    xc2 = xc.reshape(jt * nb, ch)

    xb = xc2.astype(BF16)
    nblk, wb = wa_ref.shape[0], wa_ref.shape[1]
    za = jnp.concatenate([jnp.dot(xb[:, i * wb:(i + 1) * wb], wa_ref[i], preferred_element_type=F32)
                          for i in range(nblk)], axis=-1)
    zx = jnp.concatenate([jnp.dot(xb[:, i * wb:(i + 1) * wb], wx_ref[i], preferred_element_type=F32)
                          for i in range(nblk)], axis=-1)
    rg = jax.nn.sigmoid(za + ba_ref[...])
    ig = jax.nn.sigmoid(zx + bx_ref[...])
    log_a = cn_ref[...] * rg
    a = jnp.exp(log_a)
    bb = jnp.sqrt(1.0 - a * a) * (ig * xc2)
    a_s[...] = a.reshape(jt, nb, ch)
    b_s[...] = bb.reshape(jt, nb, ch)

    def step(t, h):
        tt = jt - 1 - t if reverse else t
        h = a_s[tt] * h + b_s[tt]
        if combine:
            obuf[slot, tt] = (h + hbuf[slot, tt]) * jax.nn.gelu(qbuf[slot, tt])
        else:
            obuf[slot, tt] = h
        return h

    h_ref[...] = lax.fori_loop(0, jt, step, h_ref[...])

    stores(c, slot, start)

    @pl.when(s == nc - 1)
    def _():
        if nc >= 2:
            stores(chunk_of(s - 1), 1 - slot, wait)
        stores(c, slot, wait)


def _lru_pass(reverse, r4, q4, h_other4, conv_w, conv_b, wa, ba, wx, bx, cneg, n_ctx, rows):
    nb, nblk, gw, ch = r4.shape
    jt = LRU_STEPS
    nc, nc_ctx = (nblk * gw) // jt, n_ctx // jt
    combine = h_other4 is not None
    assert gw % jt == 0 and n_ctx % gw == 0 and jt % rows == 0 and rows >= CONV_LEFT
    anyspec = pl.BlockSpec(memory_space=pl.ANY)
    full = lambda a: pl.BlockSpec(a.shape, lambda s: (0,) * a.ndim)
    consts = [conv_w, conv_b, wa, ba, wx, bx, cneg]
    big = [r4, q4, h_other4] if combine else [r4]
    buf = lambda n: pltpu.VMEM((2, n, nb, ch), F32)
    scratch = [buf(jt + CONV_LEFT + 1), buf(jt), pltpu.VMEM((nb, ch), F32),
               pltpu.VMEM((jt, nb, ch), F32), pltpu.VMEM((jt, nb, ch), F32),
               pltpu.SemaphoreType.DMA((2,)), pltpu.SemaphoreType.DMA((2,))]
    if combine:
        scratch += [buf(jt), buf(jt)]
    return pl.pallas_call(
        functools.partial(_lru_kernel, reverse, combine, nc_ctx, nc, rows),
        grid=(nc,), in_specs=[anyspec] * len(big) + [full(a) for a in consts], out_specs=anyspec,
        out_shape=jax.ShapeDtypeStruct(r4.shape, F32),
        scratch_shapes=scratch,
        compiler_params=_cparams("arbitrary"),
        name="lru_bwd" if reverse else "lru_fwd",
    )(*big, *consts)


def _block_diag_halves(w):
    nh, hd, _ = w.shape
    half = nh // 2
    eye = jnp.eye(half, dtype=w.dtype)
    wh = w.reshape(2, half, hd, hd)
    bd = jnp.einsum('bhij,hk->bhikj', wh, eye).reshape(2, half * hd, half * hd)
    return bd.astype(BF16)


def _post_kernel(n_groups, n_per, nsrc, off, cc, nsteps, *refs):
    srcs, refs = list(refs[:nsrc]), refs[nsrc:]
    (lr_hbm, yt_ref, ut_ref, gab_ref, gac_ref, shb_ref, shc_ref, scb_ref, scc_ref, d_ref, wg_ref, bg_ref,
     wo_ref, g2_ref, wr_ref, br_ref, x1_hbm, hm_hbm, meta_ref, xbuf, lbuf, x1buf, hmbuf, isem, osem) = refs
    st = _TileStreams([(srcs, xbuf), ([lr_hbm], lbuf)], [(x1_hbm, x1buf, off), (hm_hbm, hmbuf, off)],
                      isem, osem, cc, off, nsteps)
    st.begin()
    ct, slot = st.ct, st.slot
    x_in = xbuf[slot]
    nb, _, d = x_in.shape
    tm = nb * cc
    d_s5 = d_ref.shape[0]
    ys = yt_ref[...].reshape(d_s5, tm) + d_ref[...] * ut_ref[...].reshape(d_s5, tm)
    z = jax.nn.gelu(ys)
    s5t = z * jax.nn.sigmoid(jnp.dot(wg_ref[...], z.astype(BF16), preferred_element_type=F32) + bg_ref[...])
    mix = jnp.concatenate([s5t.T.astype(BF16), lbuf[slot].reshape(tm, -1).astype(BF16)], axis=-1)
    proj = jnp.dot(mix, wo_ref[...], preferred_element_type=F32).reshape(nb, cc, d)
    x1 = x_in + _mod_pick(ct, gab_ref, gac_ref) * proj
    x1buf[slot] = x1
    hm3 = _rms(x1, g2_ref[...]) * (1.0 + _mod_pick(ct, scb_ref, scc_ref)) + _mod_pick(ct, shb_ref, shc_ref)
    hmbuf[slot] = hm3
    hm = hm3.reshape(tm, d)

    hi = hm.astype(BF16)
    lo = (hm - hi.astype(F32)).astype(BF16)
    two = jnp.dot(jnp.concatenate([hi, lo], axis=-1), wr_ref[...], preferred_element_type=F32)
    logits = two[:, :LANES] + two[:, LANES:] + br_ref[...]

    lane = lax.broadcasted_iota(jnp.int32, (tm, LANES), 1).astype(F32)
    neg = jnp.float32(-jnp.inf)
    big = jnp.float32(LANES)
    lg = jnp.where(lane < n_groups, logits, neg)
    mg = jnp.max(lg, axis=-1, keepdims=True)
    g_p = 1.0 / jnp.sum(jnp.exp(lg - mg), axis=-1, keepdims=True)
    gidx = jnp.min(jnp.where(lg == mg, lane, big), axis=-1, keepdims=True)
    e0 = n_groups + n_per * gidx
    le = jnp.where(jnp.logical_and(lane >= e0, lane < e0 + n_per), logits, neg)
    m1 = jnp.max(le, axis=-1, keepdims=True)
    i1 = jnp.min(jnp.where(le == m1, lane, big), axis=-1, keepdims=True)
    le2 = jnp.where(lane == i1, neg, le)
    m2 = jnp.max(le2, axis=-1, keepdims=True)
    i2 = jnp.min(jnp.where(le2 == m2, lane, big), axis=-1, keepdims=True)
    r2 = jnp.exp(m2 - m1)
    w1 = g_p / (1.0 + r2)
    w2 = g_p * r2 / (1.0 + r2)
    j1 = i1 - e0
    j2 = i2 - e0
    jlo = jnp.minimum(j1, j2)
    jhi = jnp.maximum(j1, j2)
    wlo = jnp.where(j1 < j2, w1, w2)
    whi = jnp.where(j1 < j2, w2, w1)
    pair = jlo * (2 * n_per - 1 - jlo) * 0.5 + (jhi - jlo - 1.0)
    bucket = gidx * (n_per * (n_per - 1) // 2) + pair
    meta = jnp.where(lane == 0, bucket, jnp.where(lane == 1, wlo, jnp.where(lane == 2, whi, 0.0)))
    meta_ref[...] = meta.T[0:8, :]
    st.end()


def _post(lat_only, yt, ut, lr, xs, mods, layer, s5_d, w_glu, b_glu, w_out_bf, g2, wr2, br, n_groups, n_per, cc):
    q = S5_CHUNK
    nb, d = xs[0].shape[0], xs[0].shape[-1]
    n = sum(a.shape[1] for a in xs)
    nchunk = n // q
    ngrp, qh, _ = yt.shape
    nh = qh // q
    d_s5 = ngrp * nh
    d_lru = lr.shape[-1]
    off = 1 if lat_only else 0
    assert not (lat_only and len(xs) == 2)
    nct = nchunk // cc - off
    tm = nb * cc
    anyspec = pl.BlockSpec(memory_space=pl.ANY)
    tspec = pl.BlockSpec((ngrp, nh, tm), lambda ct, t: (0, t, ct + off))
    full = lambda a: pl.BlockSpec(a.shape, lambda ct, t: (0,) * a.ndim)
    consts = [s5_d.reshape(d_s5, 1), w_glu.T.astype(BF16), b_glu.reshape(d_s5, 1), w_out_bf, g2.reshape(1, d),
              wr2, br]
    buf = lambda w: pltpu.VMEM((2, nb, cc, w), F32)
    x1, hm, meta = pl.pallas_call(
        functools.partial(_post_kernel, n_groups, n_per, len(xs), off, cc, nct * q),
        grid=(nct, q),
        in_specs=([anyspec] * (len(xs) + 1) + [tspec, tspec]
                  + _mod_specs(layer, 2, d, nb) + _mod_specs(layer, 3, d, nb) + _mod_specs(layer, 4, d, nb)
                  + [full(a) for a in consts]),
        out_specs=[anyspec, anyspec, pl.BlockSpec((None, None, 8, tm), lambda ct, t: (t, ct, 0, 0))],
        out_shape=[jax.ShapeDtypeStruct((nb, nct * cc, q, d), F32), jax.ShapeDtypeStruct((nb, nct * cc, q, d), F32),
                   jax.ShapeDtypeStruct((q, nct, 8, tm), F32)],
        scratch_shapes=[buf(d), buf(d_lru), buf(d), buf(d),
                        pltpu.SemaphoreType.DMA((2,)), pltpu.SemaphoreType.DMA((2,))],
        compiler_params=_cparams("arbitrary", "arbitrary"),
        name="post_mixer",
    )(*[_view4(a) for a in xs], _view4(lr), yt, ut, mods, mods, mods, mods, mods, mods, *consts)
    n_out = nct * cc * q
    return x1.reshape(nb, n_out, d), hm.reshape(nb, n_out, d), meta


def _moe_kernel(n_rows, nt, tb_ref, src0_ref, srcn_ref, dst_ref, hm_hbm, wt_ref, w1a, w3a, w2a, w1b, w3b, w2b,
                y_hbm, xbuf, ybuf, gsem, ssem):
    i = pl.program_id(0)
    slot = lax.rem(i, 2)
    tmr = xbuf.shape[1]

    def valid(j):
        return tb_ref[2, jnp.clip(j, 0, nt - 1)] > 0

    def start_gathers(ids_ref, s):
        for r in range(tmr):
            pltpu.make_async_copy(hm_hbm.at[pl.ds(ids_ref[0, r], 1), :], xbuf.at[s, pl.ds(r, 1), :],
                                  gsem.at[s]).start(priority=r % 2)

    def start_scatters(s):
        for r in range(tmr):
            pltpu.make_async_copy(ybuf.at[s, pl.ds(r, 1), :], y_hbm.at[pl.ds(dst_ref[0, r], 1), :],
                                  ssem.at[s]).start(priority=r % 2)

    def wait_gathers(s):
        pltpu.make_async_copy(hm_hbm.at[pl.ds(0, tmr), :], xbuf.at[s], gsem.at[s]).wait()

    def wait_scatters(s):
        pltpu.make_async_copy(ybuf.at[s], y_hbm.at[pl.ds(0, tmr), :], ssem.at[s]).wait()

    @pl.when(i == 0)
    def _():
        xbuf[...] = jnp.zeros_like(xbuf)
        ybuf[...] = jnp.zeros_like(ybuf)
        for s in range((y_hbm.shape[0] - n_rows) // tmr):
            cp = pltpu.make_async_copy(ybuf.at[0], y_hbm.at[pl.ds(n_rows + s * tmr, tmr), :], ssem.at[0])
            cp.start()
            cp.wait()
        start_gathers(src0_ref, 0)

    def step(s):
        @pl.when(jnp.logical_and(i + 1 < nt, valid(i + 1)))
        def _():
            start_gathers(srcn_ref, 1 - s)

        @pl.when(jnp.logical_and(i >= 2, valid(i - 2)))
        def _():
            wait_scatters(s)

        @pl.when(valid(i))
        def _():
            wait_gathers(s)
            xb = xbuf[s].astype(BF16)
            wt = wt_ref[...]

            def gated_hidden(w1, w3, gate):
                h1 = jnp.dot(xb, w1[...], preferred_element_type=F32)
                h3 = jnp.dot(xb, w3[...], preferred_element_type=F32)
                return ((h1 * jax.nn.sigmoid(h1)) * (h3 * gate)).astype(BF16)

            half = LANES // 2
            ybuf[s] = (jnp.dot(gated_hidden(w1a, w3a, wt[:, 0:1]), w2a[...], preferred_element_type=F32)
                       + jnp.dot(gated_hidden(w1b, w3b, wt[:, half:half + 1]), w2b[...],
                                 preferred_element_type=F32))
            start_scatters(s)

        @pl.when(i == nt - 1)
        def _():
            @pl.when(jnp.logical_and(nt >= 2, valid(i - 1)))
            def _():
                wait_scatters(1 - s)

            @pl.when(valid(i))
            def _():
                wait_scatters(s)

    for s in range(2):
        pl.when(slot == s)(functools.partial(step, s))


def _moe(hm2, bucket, wlo, whi, tok_rows, w1_bf, w3_bf, w2_bf, expert_base, n_groups, n_per, tmr, spare_rows):
    n_rows, d = hm2.shape
    assert spare_rows >= 2 * tmr and spare_rows % tmr == 0
    t = bucket.shape[0]
    npairs = n_per * (n_per - 1) // 2
    nbuck = n_groups * npairs
    ntiles = t // tmr + nbuck

    _, s_tok, s_lo, s_hi = lax.sort((bucket, tok_rows, wlo, whi), num_keys=1, is_stable=False)
    counts = jnp.sum((bucket[None, :] == jnp.arange(nbuck, dtype=jnp.int32)[:, None]).astype(jnp.int32), axis=1)
    padded = ((counts + tmr - 1) // tmr) * tmr
    pend = jnp.cumsum(padded)
    cend = jnp.cumsum(counts)
    shift = (pend - padded) - (cend - counts)
    tile_start = jnp.arange(ntiles, dtype=jnp.int32) * tmr
    valid = (tile_start < pend[-1]).astype(jnp.int32)
    tb = jnp.sum((tile_start[:, None] >= pend[None, :]).astype(jnp.int32), axis=1)
    tb = jnp.minimum(tb, jnp.sum((pend < pend[-1]).astype(jnp.int32)))
    tb = jnp.minimum(tb, nbuck - 1)
    pos = (tile_start[:, None] + jnp.arange(tmr, dtype=jnp.int32)[None, :]) - shift[tb][:, None]
    real = jnp.logical_and(pos < cend[tb][:, None], valid[:, None] > 0)
    pos = jnp.clip(pos, 0, t - 1)
    tok = s_tok[pos]
    src = jnp.where(real, tok, 0)
    spare = n_rows + (jnp.arange(ntiles, dtype=jnp.int32) % 2)[:, None] * tmr + jnp.arange(tmr, dtype=jnp.int32)
    dst = jnp.where(real, tok, spare)
    g_lo = jnp.where(real, s_lo[pos], 0.0).reshape(-1, 1)
    g_hi = jnp.where(real, s_hi[pos], 0.0).reshape(-1, 1)
    half = LANES // 2
    wts = jnp.concatenate([jnp.broadcast_to(g_lo, (ntiles * tmr, half)),
                           jnp.broadcast_to(g_hi, (ntiles * tmr, half))], axis=1)
    grp, pr = tb // npairs, tb % npairs
    pairs = [(i, j) for i in range(n_per) for j in range(i + 1, n_per)]
    plo = jnp.array([p[0] for p in pairs], jnp.int32)[pr]
    phi = jnp.array([p[1] for p in pairs], jnp.int32)[pr]
    first = expert_base + grp * n_per
    tinfo = jnp.stack([first + plo, first + phi, valid])

    wspec = lambda a, row: pl.BlockSpec((None,) + a.shape[1:], lambda i, tb_: (tb_[row, i], 0, 0))
    ids = lambda f: pl.BlockSpec((None, 1, tmr), lambda i, tb_: (f(i), 0, 0), memory_space=pltpu.SMEM)
    grid_spec = pltpu.PrefetchScalarGridSpec(
        num_scalar_prefetch=1,
        grid=(ntiles,),
        in_specs=[ids(lambda i: 0), ids(lambda i: jnp.minimum(i + 1, ntiles - 1)), ids(lambda i: i),
                  pl.BlockSpec(memory_space=pl.ANY),
                  pl.BlockSpec((tmr, LANES), lambda i, tb_: (i, 0)),
                  wspec(w1_bf, 0), wspec(w3_bf, 0), wspec(w2_bf, 0),
                  wspec(w1_bf, 1), wspec(w3_bf, 1), wspec(w2_bf, 1)],
        out_specs=pl.BlockSpec(memory_space=pl.ANY),
        scratch_shapes=[pltpu.VMEM((2, tmr, d), F32), pltpu.VMEM((2, tmr, d), F32),
                        pltpu.SemaphoreType.DMA((2,)), pltpu.SemaphoreType.DMA((2,))],
    )
    src3 = src.reshape(ntiles, 1, tmr)
    return pl.pallas_call(
        functools.partial(_moe_kernel, n_rows, ntiles),
        grid_spec=grid_spec,
        out_shape=jax.ShapeDtypeStruct((n_rows + spare_rows, d), F32),
        compiler_params=_cparams("arbitrary"),
        name="moe_pairs",
    )(tinfo, src3, src3, dst.reshape(ntiles, 1, tmr), hm2, wts, w1_bf, w3_bf, w2_bf, w1_bf, w3_bf, w2_bf)


def _final_kernel(x_ref, y_ref, ga_ref, g_ref, o_ref):
    o_ref[...] = _rms(x_ref[...] + ga_ref[...] * y_ref[...], g_ref[...])


def _final(x1, y, mods, layer, final_g, tm):
    nb, seq, d = x1.shape
    tok = pl.BlockSpec((None, tm, d), lambda b, i: (b, i, 0))
    return pl.pallas_call(
        _final_kernel,
        grid=(nb, seq // tm),
        in_specs=[tok, pl.BlockSpec((tm, d), lambda b, i: (b * (seq // tm) + i, 0)),
                  _mod_spec_lat(layer, 5, d), pl.BlockSpec((1, d), lambda b, i: (0, 0))],
        out_specs=tok,
        out_shape=jax.ShapeDtypeStruct((nb, seq, d), F32),
        compiler_params=_cparams("parallel", "parallel"),
        name="final_norm",
    )(x1, y, mods, final_g.reshape(1, d))


def kernel(x, c, ctx, c_ctx, w_mod, b_mod, norm1_g, norm2_g, w_in, w_out, s5_a_re, s5_a_im, s5_log_dt, s5_b_re, s5_b_im, s5_c_re, s5_c_im, s5_d, s5_w_glu, s5_b_glu, lru_conv_w, lru_conv_b, lru_w_a, lru_b_a, lru_w_x, lru_b_x, lru_lam, moe_w_group, moe_b_group, moe_w_router, moe_b_router, moe_w1, moe_w3, moe_w2, final_g):
    nb, seq, d = x.shape
    n_ctx = ctx.shape[1]
    depth = w_mod.shape[0]
    n = n_ctx + seq
    d_s5 = s5_d.shape[-1]
    d_lru = lru_conv_b.shape[-1]
    ngrp, nh = s5_b_re.shape[2], s5_b_re.shape[4]
    n_groups, n_per = moe_w_router.shape[1], moe_w_router.shape[3]
    rows = seq // GRID_W
    tm = n_ctx
    q = S5_CHUNK
    nc_ctx, nc_lat = n_ctx // q, seq // q
    moe_rows = 256 if nb * seq >= 16384 else 32
    assert seq % GRID_W == 0 and seq % tm == 0 and n_ctx % GRID_W == 0 and seq % LRU_STEPS == 0
    assert d_s5 == ngrp * nh and q * nh == 2 * LANES and n_groups + n_groups * n_per <= LANES

    pad = (-(nb + 1)) % 8
    c_rows = jnp.concatenate([c, c_ctx[None, :], jnp.zeros((pad, d), F32)], axis=0)
    mods = _modulation(c_rows, w_mod, b_mod).reshape(depth, nb + 1 + pad, 1, 6 * d)

    s5w = jax.vmap(_s5_weights)(s5_a_re, s5_a_im, s5_log_dt, s5_b_re, s5_b_im, s5_c_re, s5_c_im)
    bdiag = jax.vmap(jax.vmap(_block_diag_halves))
    wa_bd, wx_bd = bdiag(lru_w_a), bdiag(lru_w_x)
    cneg_all = (-LRU_C * jax.nn.softplus(-lru_lam.astype(F32))).reshape(depth, 2, 1, d_lru)
    wr = jnp.concatenate([moe_w_group, moe_w_router.transpose(0, 2, 1, 3).reshape(depth, d, n_groups * n_per)], -1)
    wr = jnp.pad(wr.astype(F32), ((0, 0), (0, 0), (0, LANES - wr.shape[-1])))
    wr_hi = wr.astype(BF16)
    wr_lo = (wr - wr_hi.astype(F32)).astype(BF16)
    wr2_all = jnp.concatenate([jnp.concatenate([wr_hi, wr_lo], -1),
                               jnp.concatenate([wr_hi, jnp.zeros_like(wr_lo)], -1)], axis=1)
    br_all = jnp.concatenate([moe_b_group, moe_b_router.reshape(depth, -1)], axis=-1)
    br_all = jnp.pad(br_all.astype(F32), ((0, 0), (0, LANES - br_all.shape[-1]))).reshape(depth, 1, LANES)
    w_out_bf = w_out.astype(BF16)
    n_exp = n_groups * n_per
    moe_bf = [w.astype(BF16).reshape((depth * n_exp,) + w.shape[2:]) for w in (moe_w1, moe_w3, moe_w2)]

    cc = nc_ctx
    xs = [ctx, x]
    y = None
    for l in range(depth):
        need_ctx = l < depth - 1
        xs, ut, r, qg = _inproj(xs, y, mods, l, norm1_g[l], w_in[l], ngrp, d_s5, d_lru, cc)

        yt = _s5_scan(ut, *s5w, l, nb, cc, nc_ctx, nc_lat)

        view = lambda a: a.reshape(nb, n // GRID_W, GRID_W, d_lru)
        cw, cb = lru_conv_w[l].astype(F32), lru_conv_b[l].reshape(1, d_lru).astype(F32)
        gates = [(wa_bd[l, dd], lru_b_a[l, dd].reshape(1, d_lru),
                  wx_bd[l, dd], lru_b_x[l, dd].reshape(1, d_lru), cneg_all[l, dd]) for dd in range(2)]
        h_b = _lru_pass(True, view(r), None, None, cw, cb, *gates[1], n_ctx, rows)
        lr = _lru_pass(False, view(r), view(qg), h_b, cw, cb, *gates[0], n_ctx, rows).reshape(nb, n, d_lru)

        x1, hm, meta = _post(not need_ctx, yt, ut, lr, xs, mods, l, s5_d[l], s5_w_glu[l], s5_b_glu[l],
                             w_out_bf[l], norm2_g[l], wr2_all[l], br_all[l], n_groups, n_per, cc)

        n_out = x1.shape[1]
        nct = n_out // (cc * q)
        meta_t = meta.reshape(q, nct, 8, nb, cc).transpose(2, 3, 1, 4, 0).reshape(8, nb * n_out)
        bucket = meta_t[0].astype(jnp.int32)
        y = _moe(hm.reshape(nb * n_out, d), bucket, meta_t[1], meta_t[2], jnp.arange(nb * n_out, dtype=jnp.int32),
                 *moe_bf, l * n_exp, n_groups, n_per, moe_rows, n_out)
        xs = [x1]
        if need_ctx:
            y = y.reshape(nb + 1, n_out, d)
    return _final(xs[0], y, mods, depth - 1, final_g, 4 * tm if seq % (4 * tm) == 0 else tm)
```
